```python
import math
import jax, jax.numpy as jnp
from jax import lax
import numpy as np

D_MODEL = 1024
BATCH = 16
SEQ = 2048
DEPTH = 2

CTX_LEN = 256
GRID_W = 64
D_MIX = D_MODEL
HEAD_DIM = 64
S5_WIDTH = D_MIX // 4
S5_GROUP = 16
S5_GROUPS = S5_WIDTH // S5_GROUP
S5_STATE = 64
NA_WIDTH = (D_MIX - S5_WIDTH) // 2
NA_HEADS = NA_WIDTH // HEAD_DIM
NA_WIN_R = 8
NA_WIN_C = 16
NA_KEY_SPAN = 2 * NA_WIN_C
ROPE_BASE = 10000.0
RK_WIDTH = D_MIX - S5_WIDTH - NA_WIDTH
RK_HEADS = RK_WIDTH // HEAD_DIM
RK_DECAY_RANK = 64
RK_A_RANK = 64
RK_GATE_RANK = 128
RK_IN_WIDTH = 3 * RK_WIDTH + 2 * RK_DECAY_RANK + 2 * RK_A_RANK + RK_GATE_RANK
RK_GN_EPS = 64e-5
N_IN = S5_WIDTH + 3 * NA_WIDTH + RK_IN_WIDTH
N_EXPERTS = 16
EC_CAPACITY_FACTOR = 2
D_EXPERT = 2816
RMS_EPS = 1e-6
NEG_INF = -1e30

kernel_name = 'hybrid_s5_natten_rwkv7_ec_dit'


def rmsnorm(x, g):
    xf = x.astype(jnp.float32)
    y = xf * lax.rsqrt(jnp.mean(xf * xf, axis=-1, keepdims=True) + RMS_EPS)
    return (y * g.astype(jnp.float32)).astype(x.dtype)


def axial_rope(x):
    n = x.shape[1]
    t = jnp.arange(n)
    row = (t // GRID_W).astype(jnp.float32)
    col = (t % GRID_W).astype(jnp.float32)
    half = x.shape[-1] // 2
    nf = half // 2
    inv_freq = ROPE_BASE ** (-jnp.arange(nf, dtype=jnp.float32) / nf)

    def rot(xp, pos):
        ang = pos[:, None] * inv_freq
        cos = jnp.cos(ang)[None, :, None, :]
        sin = jnp.sin(ang)[None, :, None, :]
        x1, x2 = xp[..., :nf], xp[..., nf:]
        return jnp.concatenate([x1 * cos - x2 * sin, x1 * sin + x2 * cos], axis=-1)

    xf = x.astype(jnp.float32)
    return jnp.concatenate([rot(xf[..., :half], row), rot(xf[..., half:], col)], axis=-1).astype(x.dtype)


def s5_discretise(lam_re, lam_im, log_dt, b_re, b_im):
    lam = lax.complex(lam_re.astype(jnp.float32), lam_im.astype(jnp.float32))
    dt = jnp.exp(log_dt.astype(jnp.float32))[:, None]
    lam_bar = jnp.exp(lam * dt)
    b = lax.complex(b_re.astype(jnp.float32), b_im.astype(jnp.float32))
    b_bar = ((lam_bar - 1.0) / lam)[..., None] * b
    return lam_bar, b_bar


def _linear_recurrence_op(e1, e2):
    a1, b1 = e1
    a2, b2 = e2
    return a1 * a2, a2 * b1 + b2


def s5_scan(u, lam_bar, b_bar, s0, reverse):
    n = u.shape[1]
    bu = jnp.einsum('bngh,gph->nbgp', u, b_bar)
    if s0 is not None:
        bu = bu.at[-1 if reverse else 0].add(lam_bar * s0)
    a = jnp.broadcast_to(lam_bar, (n, 1) + lam_bar.shape)
    _, xs = lax.associative_scan(_linear_recurrence_op, (a, bu), reverse=reverse)
    return xs, (xs[0] if reverse else xs[-1])


def s5_readout(xs, c_re, c_im):
    c_re = c_re.astype(jnp.float32)
    c_im = c_im.astype(jnp.float32)
    return jnp.einsum('nbgp,ghp->bngh', xs.real, c_re) - jnp.einsum('nbgp,ghp->bngh', xs.imag, c_im)


def s5_mixer(u_lat, u_ctx, lam_re, lam_im, log_dt, b_re, b_im, c_re, c_im, d_skip, glu_w, glu_b, need_ctx):
    def groups(u):
        return u.astype(jnp.float32).reshape(u.shape[0], u.shape[1], S5_GROUPS, S5_GROUP)

    ul, uc = groups(u_lat), groups(u_ctx)
    dsk = d_skip.astype(jnp.float32).reshape(S5_GROUPS, S5_GROUP)
    y_lat = dsk * ul
    y_ctx = dsk * uc
    for d, rev in enumerate((False, True)):
        lam_bar, b_bar = s5_discretise(lam_re[d], lam_im[d], log_dt[d], b_re, b_im)
        xc, s_ctx = s5_scan(uc, lam_bar, b_bar, None, rev)
        xl, _ = s5_scan(ul, lam_bar, b_bar, s_ctx, rev)
        y_lat = y_lat + s5_readout(xl, c_re[d], c_im[d])
        if need_ctx:
            y_ctx = y_ctx + s5_readout(xc, c_re[d], c_im[d])

    def glu(y):
        y = jax.nn.gelu(y.reshape(y.shape[0], y.shape[1], S5_WIDTH))
        return y * jax.nn.sigmoid(y @ glu_w.astype(jnp.float32) + glu_b.astype(jnp.float32))

    out_lat = glu(y_lat).astype(u_lat.dtype)
    out_ctx = glu(y_ctx).astype(u_ctx.dtype) if need_ctx else None
    return out_lat, out_ctx


def na_geometry():
    n_blk = GRID_W // NA_WIN_C
    q_cols = np.arange(GRID_W).reshape(n_blk, NA_WIN_C)
    start = np.clip(np.arange(n_blk) * NA_WIN_C - NA_WIN_C // 2, 0, GRID_W - NA_KEY_SPAN)
    key_cols = start[:, None] + np.arange(NA_KEY_SPAN)[None, :]
    c0 = np.clip(q_cols - NA_WIN_C // 2, 0, GRID_W - NA_WIN_C)[..., None]
    kc = key_cols[:, None, :]
    valid = (kc >= c0) & (kc < c0 + NA_WIN_C)
    dc = np.clip(kc - q_cols[..., None], 1 - NA_WIN_C, NA_WIN_C - 1) + NA_WIN_C - 1
    return key_cols.astype(np.int32), valid, dc.astype(np.int32)


def na_mixer(q, k, v, qc, kc, vc, rpb, need_ctx):
    bsz, n, h, dh = q.shape
    rows = n // GRID_W
    win_r = min(NA_WIN_R, rows)
    scale = dh ** -0.5
    key_cols, valid, dc = na_geometry()
    n_blk = key_cols.shape[0]
    qg = q.reshape(bsz, rows, n_blk, NA_WIN_C, h, dh)
    kg = k.reshape(bsz, rows, GRID_W, h, dh)
    vg = v.reshape(bsz, rows, GRID_W, h, dh)
    bias_c = rpb.astype(jnp.float32)[:, :, dc]
    mask = jnp.asarray(valid)[:, :, None, :]
    n_lat = win_r * NA_KEY_SPAN

    def row_block(i):
        r0 = jnp.clip(i - win_r // 2, 0, rows - win_r)
        k_blk = lax.dynamic_slice_in_dim(kg, r0, win_r, axis=1)[:, :, key_cols]
        v_blk = lax.dynamic_slice_in_dim(vg, r0, win_r, axis=1)[:, :, key_cols]
        q_i = qg[:, i]
        s_lat = jnp.einsum('bnqhd,brnkhd->bhnqrk', q_i, k_blk).astype(jnp.float32) * scale
        dr = r0 + jnp.arange(win_r) - i + NA_WIN_R - 1
        bias = jnp.take(bias_c, dr, axis=1).transpose(0, 2, 3, 1, 4)
        s_lat = jnp.where(mask, s_lat + bias, NEG_INF)
        s_ctx = jnp.einsum('bnqhd,blhd->bhnql', q_i, kc).astype(jnp.float32) * scale
        s = jnp.concatenate([s_lat.reshape(bsz, h, n_blk, NA_WIN_C, n_lat), s_ctx], axis=-1)
        p = jax.nn.softmax(s, axis=-1).astype(v.dtype)
        p_lat = p[..., :n_lat].reshape(bsz, h, n_blk, NA_WIN_C, win_r, NA_KEY_SPAN)
        o = jnp.einsum('bhnqrk,brnkhd->bnqhd', p_lat, v_blk) + jnp.einsum('bhnql,blhd->bnqhd', p[..., n_lat:], vc)
        return o.reshape(bsz, GRID_W, h * dh)

    out = lax.map(row_block, jnp.arange(rows))
    out = jnp.swapaxes(out, 0, 1).reshape(bsz, n, h * dh)
    out_ctx = None
    if need_ctx:
        s = jnp.einsum('blhd,bmhd->bhlm', qc, kc).astype(jnp.float32) * scale
        p = jax.nn.softmax(s, axis=-1).astype(vc.dtype)
        out_ctx = jnp.einsum('bhlm,bmhd->blhd', p, vc).reshape(bsz, qc.shape[1], h * dh)
    return out, out_ctx


def centred_shift(x, mu):
    pad = jnp.pad(x, ((0, 0), (1, 1), (0, 0)))
    return x + (0.5 * (pad[:, :-2] + pad[:, 2:]) - x) * mu


def rwkv_prepare(z, w0, w2, a0, a2, g2, k_k, k_a):
    bsz, n, _ = z.shape
    zf = z.astype(jnp.float32)
    W, R, RA = RK_WIDTH, RK_DECAY_RANK, RK_A_RANK

    def heads(t):
        return t.reshape(bsz, n, RK_HEADS, HEAD_DIM)

    r, k, v = zf[..., :W], zf[..., W:2 * W], zf[..., 2 * W:3 * W]
    o = 3 * W
    zw = (zf[..., o:o + R], zf[..., o + R:o + 2 * R])
    o = o + 2 * R
    za = (zf[..., o:o + RA], zf[..., o + RA:o + 2 * RA])
    o = o + 2 * RA
    g = jax.nn.sigmoid(zf[..., o:o + RK_GATE_RANK]) @ g2.astype(jnp.float32)
    kk = heads(k * k_k.astype(jnp.float32))
    kk = kk / jnp.maximum(jnp.sqrt(jnp.sum(kk * kk, axis=-1, keepdims=True)), 1e-12)
    dirs = []
    for d in range(2):
        w = -jax.nn.softplus(-(w0[d].astype(jnp.float32) + jnp.tanh(zw[d]) @ w2[d].astype(jnp.float32))) - 0.5
        a = jax.nn.sigmoid(a0[d].astype(jnp.float32) + za[d] @ a2[d].astype(jnp.float32))
        kd = k * (1.0 + (a - 1.0) * k_a.astype(jnp.float32))
        dirs.append((heads(jnp.exp(-jnp.exp(w))), heads(kd), kk * heads(a)))
    return heads(r), heads(v), kk, g, dirs


def rwkv_scan(r, decay, k, v, a, b, s0, reverse):
    xs = tuple(jnp.moveaxis(t, 1, 0) for t in (r, decay, k, v, a, b))

    def step(S, inp):
        r_t, w_t, k_t, v_t, a_t, b_t = inp
        sa = jnp.einsum('bhij,bhj->bhi', S, a_t)
        S = S * w_t[:, :, None, :] + sa[..., None] * b_t[:, :, None, :] + v_t[..., None] * k_t[:, :, None, :]
        return S, jnp.einsum('bhij,bhj->bhi', S, r_t)

    S, ys = lax.scan(step, s0, xs, reverse=reverse)
    return jnp.moveaxis(ys, 0, 1), S


def rwkv_mixer(z_lat, z_ctx, mu, w0, w2, a0, a2, g2, k_k, k_a, r_k, ln_w, ln_b, need_ctx):
    mu = mu.astype(jnp.float32)
    lat = rwkv_prepare(centred_shift(z_lat.astype(jnp.float32), mu), w0, w2, a0, a2, g2, k_k, k_a)
    cx = rwkv_prepare(centred_shift(z_ctx.astype(jnp.float32), mu), w0, w2, a0, a2, g2, k_k, k_a)
    s_zero = jnp.zeros((z_ctx.shape[0], RK_HEADS, HEAD_DIM, HEAD_DIM), jnp.float32)
    ys_lat, ys_ctx = [], []
    for d, rev in enumerate((False, True)):
        dec, kd, b = cx[4][d]
        yc, s_ctx = rwkv_scan(cx[0], dec, kd, cx[1], -cx[2], b, s_zero, rev)
        dec, kd, b = lat[4][d]
        yl, _ = rwkv_scan(lat[0], dec, kd, lat[1], -lat[2], b, s_ctx, rev)
        ys_lat.append(yl)
        ys_ctx.append(yc)
    r_k = r_k.astype(jnp.float32)

    def finish(prep, y):
        r, v, _, g, dirs = prep
        bsz, n = y.shape[:2]
        mean = jnp.mean(y, axis=-1, keepdims=True)
        var = jnp.mean(jnp.square(y - mean), axis=-1, keepdims=True)
        yn = ((y - mean) * lax.rsqrt(var + RK_GN_EPS)).reshape(bsz, n, RK_WIDTH)
        yn = yn * ln_w.astype(jnp.float32) + ln_b.astype(jnp.float32)
        bonus = (jnp.sum(r * dirs[0][1] * r_k, axis=-1, keepdims=True)
                 + jnp.sum(r * dirs[1][1] * r_k, axis=-1, keepdims=True)) * v
        return (yn + bonus.reshape(bsz, n, RK_WIDTH)) * g

    out_lat = finish(lat, ys_lat[0] + ys_lat[1]).astype(z_lat.dtype)
    out_ctx = finish(cx, ys_ctx[0] + ys_ctx[1]).astype(z_ctx.dtype) if need_ctx else None
    return out_lat, out_ctx


def expert_choice_ffn(h, w_router, w_gate, w_up, w_down):
    bsz, n, d = h.shape
    cap = EC_CAPACITY_FACTOR * n // N_EXPERTS
    aff = jax.nn.softmax((h @ w_router).astype(jnp.float32), axis=-1)
    gate, idx = lax.top_k(jnp.swapaxes(aff, 1, 2), cap)
    xin = jax.vmap(lambda hb, ib: hb[ib])(h, idx)
    hid = jax.nn.silu(jnp.einsum('becd,edf->becf', xin, w_gate)) * jnp.einsum('becd,edf->becf', xin, w_up)
    y = jnp.einsum('becf,efd->becd', hid, w_down) * gate[..., None].astype(h.dtype)
    return jax.vmap(lambda ib, yb: jnp.zeros((n, d), yb.dtype).at[ib.reshape(-1)].add(yb.reshape(-1, d)))(idx, y)


def setup_inputs(seed: int = 0) -> dict:
    key = jax.random.key(seed)
    ks = list(jax.random.split(key, 40))
    ctr = [0]

    def nk():
        ctr[0] += 1
        return ks[ctr[0] - 1]

    def nrm(shape, s):
        return jax.random.normal(nk(), shape, jnp.float32) * s

    L, D, P, G, GH = DEPTH, D_MODEL, S5_STATE, S5_GROUPS, S5_GROUP
    x = nrm((BATCH, SEQ, D), 1.0)
    c = nrm((BATCH, D), 1.0)
    ctx = nrm((BATCH, CTX_LEN, D), 1.0)
    c_ctx = nrm((D,), 1.0)
    ada_w = nrm((L, D, 6 * D), 0.5 * D ** -0.5)
    ada_b = nrm((L, 6 * D), 0.02)
    norm_g = 1.0 + nrm((L, 4, D), 0.05)
    w_in = nrm((L, D, N_IN), D ** -0.5)
    w_out = nrm((L, D_MIX, D), D_MIX ** -0.5)
    s5_lam_re = -0.5 + nrm((L, 2, G, P), 0.01)
    s5_lam_im = jnp.pi * jnp.arange(P, dtype=jnp.float32) + nrm((L, 2, G, P), 0.01)
    s5_log_dt = jax.random.uniform(nk(), (L, 2, G), jnp.float32, math.log(1e-3), math.log(1e-1))
    s5_b_re = nrm((L, G, P, GH), (2 * GH) ** -0.5)
    s5_b_im = nrm((L, G, P, GH), (2 * GH) ** -0.5)
    s5_c_re = nrm((L, 2, G, GH, P), (2 * P) ** -0.5)
    s5_c_im = nrm((L, 2, G, GH, P), (2 * P) ** -0.5)
    s5_d = nrm((L, S5_WIDTH), 1.0)
    s5_glu_w = nrm((L, S5_WIDTH, S5_WIDTH), S5_WIDTH ** -0.5)
    s5_glu_b = nrm((L, S5_WIDTH), 0.02)
    na_rpb = nrm((L, NA_HEADS, 2 * NA_WIN_R - 1, 2 * NA_WIN_C - 1), 0.02)
    rk_mu = jax.random.uniform(nk(), (L, RK_IN_WIDTH), jnp.float32)
    ratio = jnp.arange(RK_WIDTH, dtype=jnp.float32) / (RK_WIDTH - 1)
    rk_w0 = -6.5 + 5.0 * ratio ** 0.85 + nrm((L, 2, RK_WIDTH), 0.1)
    rk_w2 = nrm((L, 2, RK_DECAY_RANK, RK_WIDTH), 0.1 * RK_DECAY_RANK ** -0.5)
    rk_a0 = nrm((L, 2, RK_WIDTH), 0.1)
    rk_a2 = nrm((L, 2, RK_A_RANK, RK_WIDTH), 0.1 * RK_A_RANK ** -0.5)
    rk_g2 = nrm((L, RK_GATE_RANK, RK_WIDTH), RK_GATE_RANK ** -0.5)
    rk_k_k = 0.85 + nrm((L, RK_WIDTH), 0.05)
    rk_k_a = 1.0 + nrm((L, RK_WIDTH), 0.05)
    rk_r_k = nrm((L, RK_HEADS, HEAD_DIM), 0.1)
    rk_ln_w = 1.0 + nrm((L, RK_WIDTH), 0.05)
    rk_ln_b = nrm((L, RK_WIDTH), 0.02)
    ec_router = nrm((L, D, N_EXPERTS), D ** -0.5)
    ec_w_gate = nrm((L, N_EXPERTS, D, D_EXPERT), D ** -0.5)
    ec_w_up = nrm((L, N_EXPERTS, D, D_EXPERT), D ** -0.5)
    ec_w_down = nrm((L, N_EXPERTS, D_EXPERT, D), D_EXPERT ** -0.5)
    return {'x': x, 'c': c, 'ctx': ctx, 'c_ctx': c_ctx, 'ada_w': ada_w, 'ada_b': ada_b, 'norm_g': norm_g,
            'w_in': w_in, 'w_out': w_out, 's5_lam_re': s5_lam_re, 's5_lam_im': s5_lam_im,
            's5_log_dt': s5_log_dt, 's5_b_re': s5_b_re, 's5_b_im': s5_b_im, 's5_c_re': s5_c_re,
            's5_c_im': s5_c_im, 's5_d': s5_d, 's5_glu_w': s5_glu_w, 's5_glu_b': s5_glu_b, 'na_rpb': na_rpb,
            'rk_mu': rk_mu, 'rk_w0': rk_w0, 'rk_w2': rk_w2, 'rk_a0': rk_a0, 'rk_a2': rk_a2, 'rk_g2': rk_g2,
            'rk_k_k': rk_k_k, 'rk_k_a': rk_k_a, 'rk_r_k': rk_r_k, 'rk_ln_w': rk_ln_w, 'rk_ln_b': rk_ln_b,
            'ec_router': ec_router, 'ec_w_gate': ec_w_gate, 'ec_w_up': ec_w_up, 'ec_w_down': ec_w_down}


def reference(x, c, ctx, c_ctx, ada_w, ada_b, norm_g, w_in, w_out, s5_lam_re, s5_lam_im, s5_log_dt, s5_b_re,
              s5_b_im, s5_c_re, s5_c_im, s5_d, s5_glu_w, s5_glu_b, na_rpb, rk_mu, rk_w0, rk_w2, rk_a0, rk_a2,
              rk_g2, rk_k_k, rk_k_a, rk_r_k, rk_ln_w, rk_ln_b, ec_router, ec_w_gate, ec_w_up, ec_w_down):
    o_na = S5_WIDTH
    o_rk = S5_WIDTH + 3 * NA_WIDTH

    def na_heads(p, i):
        t = p[..., o_na + i * NA_WIDTH:o_na + (i + 1) * NA_WIDTH]
        return t.reshape(t.shape[0], t.shape[1], NA_HEADS, HEAD_DIM)

    for l in range(DEPTH):
        need_ctx = l < DEPTH - 1
        mod = (jax.nn.silu(c) @ ada_w[l] + ada_b[l])[:, None, :]
        mod_c = jax.nn.silu(c_ctx) @ ada_w[l] + ada_b[l]
        sh1, sc1, gt1, sh2, sc2, gt2 = jnp.split(mod, 6, axis=-1)
        csh1, csc1, cgt1, csh2, csc2, cgt2 = jnp.split(mod_c, 6, axis=-1)

        h = rmsnorm(x, norm_g[l, 0]) * (1.0 + sc1) + sh1
        hc = rmsnorm(ctx, norm_g[l, 0]) * (1.0 + csc1) + csh1
        p_lat = h @ w_in[l]
        p_ctx = hc @ w_in[l]

        y_s5, yc_s5 = s5_mixer(p_lat[..., :S5_WIDTH], p_ctx[..., :S5_WIDTH], s5_lam_re[l], s5_lam_im[l],
                               s5_log_dt[l], s5_b_re[l], s5_b_im[l], s5_c_re[l], s5_c_im[l], s5_d[l],
                               s5_glu_w[l], s5_glu_b[l], need_ctx)
        q, k, v = (na_heads(p_lat, i) for i in range(3))
        qc, kc, vc = (na_heads(p_ctx, i) for i in range(3))
        y_na, yc_na = na_mixer(axial_rope(q), axial_rope(k), v, qc, kc, vc, na_rpb[l], need_ctx)
        y_rk, yc_rk = rwkv_mixer(p_lat[..., o_rk:], p_ctx[..., o_rk:], rk_mu[l], rk_w0[l], rk_w2[l], rk_a0[l],
                                 rk_a2[l], rk_g2[l], rk_k_k[l], rk_k_a[l], rk_r_k[l], rk_ln_w[l], rk_ln_b[l],
                                 need_ctx)

        y = jnp.concatenate([y_s5, y_na, y_rk], axis=-1) @ w_out[l]
        x = x + gt1 * rmsnorm(y, norm_g[l, 1])

        h2 = rmsnorm(x, norm_g[l, 2]) * (1.0 + sc2) + sh2
        f = expert_choice_ffn(h2, ec_router[l], ec_w_gate[l], ec_w_up[l], ec_w_down[l])
        x = x + gt2 * rmsnorm(f, norm_g[l, 3])

        if need_ctx:
            yc = jnp.concatenate([yc_s5, yc_na, yc_rk], axis=-1) @ w_out[l]
            ctx = ctx + cgt1 * rmsnorm(yc, norm_g[l, 1])
            hc2 = rmsnorm(ctx, norm_g[l, 2]) * (1.0 + csc2) + csh2
            fc = expert_choice_ffn(hc2, ec_router[l], ec_w_gate[l], ec_w_up[l], ec_w_down[l])
            ctx = ctx + cgt2 * rmsnorm(fc, norm_g[l, 3])
    return x
```

```python
import functools
import math

import jax
import jax.numpy as jnp
import numpy as np
from jax import lax
from jax.experimental import pallas as pl
from jax.experimental.pallas import tpu as pltpu

D_MODEL = 1024
DEPTH = 2
GRID_W = 64
D_MIX = D_MODEL
HEAD_DIM = 64
S5_WIDTH = D_MIX // 4
S5_GROUP = 16
S5_GROUPS = S5_WIDTH // S5_GROUP
S5_STATE = 64
NA_WIDTH = (D_MIX - S5_WIDTH) // 2
NA_HEADS = NA_WIDTH // HEAD_DIM
NA_WIN_R = 8
NA_WIN_C = 16
NA_KEY_SPAN = 2 * NA_WIN_C
ROPE_BASE = 10000.0
RK_WIDTH = D_MIX - S5_WIDTH - NA_WIDTH
RK_HEADS = RK_WIDTH // HEAD_DIM
RK_DECAY_RANK = 64
RK_A_RANK = 64
RK_GATE_RANK = 128
RK_IN_WIDTH = 3 * RK_WIDTH + 2 * RK_DECAY_RANK + 2 * RK_A_RANK + RK_GATE_RANK
RK_GN_EPS = 64e-5
N_IN = S5_WIDTH + 3 * NA_WIDTH + RK_IN_WIDTH
N_EXPERTS = 16
EC_CAPACITY_FACTOR = 2
RMS_EPS = 1e-6
NEG_INF = -1e30

VMEM_LIMIT_BYTES = 48 * 1024 * 1024


def _compiler_params(semantics):
    return pltpu.CompilerParams(dimension_semantics=semantics, vmem_limit_bytes=VMEM_LIMIT_BYTES)


def _inproj_kernel(x_ref, g_ref, sc_ref, sh_ref, w_ref, o_ref):
    x = x_ref[...]
    y = x * lax.rsqrt(jnp.mean(x * x, axis=-1, keepdims=True) + RMS_EPS)
    h = (y * g_ref[...]) * (1.0 + sc_ref[...]) + sh_ref[...]
    o_ref[...] = jnp.dot(h.astype(jnp.bfloat16), w_ref[...], preferred_element_type=jnp.float32)


def _inproj(x, g, sc, sh, w_bf16, tn):
    bsz, n, d = x.shape
    nw = w_bf16.shape[1]
    return pl.pallas_call(
        _inproj_kernel,
        grid=(bsz, n // tn),
        in_specs=[
            pl.BlockSpec((None, tn, d), lambda b, i: (b, i, 0)),
            pl.BlockSpec((1, d), lambda b, i: (0, 0)),
            pl.BlockSpec((None, 1, d), lambda b, i: (b, 0, 0)),
            pl.BlockSpec((None, 1, d), lambda b, i: (b, 0, 0)),
            pl.BlockSpec((d, nw), lambda b, i: (0, 0)),
        ],
        out_specs=pl.BlockSpec((None, tn, nw), lambda b, i: (b, i, 0)),
        out_shape=jax.ShapeDtypeStruct((bsz, n, nw), jnp.float32),
        compiler_params=_compiler_params(("parallel", "parallel")),
    )(x, g.reshape(1, d), sc.reshape(bsz, 1, d), sh.reshape(bsz, 1, d), w_bf16)


def _outproj_kernel(y_ref, w_ref, g_ref, gt_ref, x_ref, o_ref):
    z = jnp.dot(y_ref[...].astype(jnp.bfloat16), w_ref[...], preferred_element_type=jnp.float32)
    zn = z * lax.rsqrt(jnp.mean(z * z, axis=-1, keepdims=True) + RMS_EPS)
    o_ref[...] = x_ref[...] + gt_ref[...] * (zn * g_ref[...])


def _outproj(y, w_bf16, g, gt, x, tn):
    bsz, n, k = y.shape
    d = w_bf16.shape[1]
    return pl.pallas_call(
        _outproj_kernel,
        grid=(bsz, n // tn),
        in_specs=[
            pl.BlockSpec((None, tn, k), lambda b, i: (b, i, 0)),
            pl.BlockSpec((k, d), lambda b, i: (0, 0)),
            pl.BlockSpec((1, d), lambda b, i: (0, 0)),
            pl.BlockSpec((None, 1, d), lambda b, i: (b, 0, 0)),
            pl.BlockSpec((None, tn, d), lambda b, i: (b, i, 0)),
        ],
        out_specs=pl.BlockSpec((None, tn, d), lambda b, i: (b, i, 0)),
        out_shape=jax.ShapeDtypeStruct((bsz, n, d), jnp.float32),
        compiler_params=_compiler_params(("parallel", "parallel")),
    )(y, w_bf16, g.reshape(1, d), gt.reshape(bsz, 1, d), x)


def rmsnorm(x, g):
    xf = x.astype(jnp.float32)
    y = xf * lax.rsqrt(jnp.mean(xf * xf, axis=-1, keepdims=True) + RMS_EPS)
    return (y * g.astype(jnp.float32)).astype(x.dtype)


def axial_rope(x):
    n = x.shape[1]
    t = jnp.arange(n)
    row = (t // GRID_W).astype(jnp.float32)
    col = (t % GRID_W).astype(jnp.float32)
    half = x.shape[-1] // 2
    nf = half // 2
    inv_freq = ROPE_BASE ** (-jnp.arange(nf, dtype=jnp.float32) / nf)

    def rot(xp, pos):
        ang = pos[:, None] * inv_freq
        cos = jnp.cos(ang)[None, :, None, :]
        sin = jnp.sin(ang)[None, :, None, :]
        x1, x2 = xp[..., :nf], xp[..., nf:]
        return jnp.concatenate([x1 * cos - x2 * sin, x1 * sin + x2 * cos], axis=-1)

    xf = x.astype(jnp.float32)
    return jnp.concatenate([rot(xf[..., :half], row), rot(xf[..., half:], col)], axis=-1).astype(x.dtype)


def s5_discretise(lam_re, lam_im, log_dt, b_re, b_im):
    lam = lax.complex(lam_re.astype(jnp.float32), lam_im.astype(jnp.float32))
    dt = jnp.exp(log_dt.astype(jnp.float32))[:, None]
    lam_bar = jnp.exp(lam * dt)
    b = lax.complex(b_re.astype(jnp.float32), b_im.astype(jnp.float32))
    b_bar = ((lam_bar - 1.0) / lam)[..., None] * b
    return lam_bar, b_bar


def _linear_recurrence_op(e1, e2):
    a1, b1 = e1
    a2, b2 = e2
    return a1 * a2, a2 * b1 + b2


def s5_scan(u, lam_bar, b_bar, s0, reverse):
    n = u.shape[1]
    bu = jnp.einsum('bngh,gph->nbgp', u, b_bar)
    if s0 is not None:
        bu = bu.at[-1 if reverse else 0].add(lam_bar * s0)
    a = jnp.broadcast_to(lam_bar, (n, 1) + lam_bar.shape)
    _, xs = lax.associative_scan(_linear_recurrence_op, (a, bu), reverse=reverse)
    return xs, (xs[0] if reverse else xs[-1])


def s5_readout(xs, c_re, c_im):
    c_re = c_re.astype(jnp.float32)
    c_im = c_im.astype(jnp.float32)
    return jnp.einsum('nbgp,ghp->bngh', xs.real, c_re) - jnp.einsum('nbgp,ghp->bngh', xs.imag, c_im)


def s5_mixer(u_lat, u_ctx, lam_re, lam_im, log_dt, b_re, b_im, c_re, c_im, d_skip, glu_w, glu_b, need_ctx):
    def groups(u):
        return u.astype(jnp.float32).reshape(u.shape[0], u.shape[1], S5_GROUPS, S5_GROUP)

    ul, uc = groups(u_lat), groups(u_ctx)
    dsk = d_skip.astype(jnp.float32).reshape(S5_GROUPS, S5_GROUP)
    y_lat = dsk * ul
    y_ctx = dsk * uc
    for d, rev in enumerate((False, True)):
        lam_bar, b_bar = s5_discretise(lam_re[d], lam_im[d], log_dt[d], b_re, b_im)
        xc, s_ctx = s5_scan(uc, lam_bar, b_bar, None, rev)
        xl, _ = s5_scan(ul, lam_bar, b_bar, s_ctx, rev)
        y_lat = y_lat + s5_readout(xl, c_re[d], c_im[d])
        if need_ctx:
            y_ctx = y_ctx + s5_readout(xc, c_re[d], c_im[d])

    def glu(y):
        y = jax.nn.gelu(y.reshape(y.shape[0], y.shape[1], S5_WIDTH))
        return y * jax.nn.sigmoid(y @ glu_w.astype(jnp.float32) + glu_b.astype(jnp.float32))

    out_lat = glu(y_lat).astype(u_lat.dtype)
    out_ctx = glu(y_ctx).astype(u_ctx.dtype) if need_ctx else None
    return out_lat, out_ctx


def na_geometry():
    n_blk = GRID_W // NA_WIN_C
    q_cols = np.arange(GRID_W).reshape(n_blk, NA_WIN_C)
    start = np.clip(np.arange(n_blk) * NA_WIN_C - NA_WIN_C // 2, 0, GRID_W - NA_KEY_SPAN)
    key_cols = start[:, None] + np.arange(NA_KEY_SPAN)[None, :]
    c0 = np.clip(q_cols - NA_WIN_C // 2, 0, GRID_W - NA_WIN_C)[..., None]
    kc = key_cols[:, None, :]
    valid = (kc >= c0) & (kc < c0 + NA_WIN_C)
    dc = np.clip(kc - q_cols[..., None], 1 - NA_WIN_C, NA_WIN_C - 1) + NA_WIN_C - 1
    return key_cols.astype(np.int32), valid, dc.astype(np.int32)


def na_mixer(q, k, v, qc, kc, vc, rpb, need_ctx):
    bsz, n, h, dh = q.shape
    rows = n // GRID_W
    win_r = min(NA_WIN_R, rows)
    scale = dh ** -0.5
    key_cols, valid, dc = na_geometry()
    n_blk = key_cols.shape[0]
    qg = q.reshape(bsz, rows, n_blk, NA_WIN_C, h, dh)
    kg = k.reshape(bsz, rows, GRID_W, h, dh)
    vg = v.reshape(bsz, rows, GRID_W, h, dh)
    bias_c = rpb.astype(jnp.float32)[:, :, dc]
    mask = jnp.asarray(valid)[:, :, None, :]
    n_lat = win_r * NA_KEY_SPAN

    def row_block(i):
        r0 = jnp.clip(i - win_r // 2, 0, rows - win_r)
        k_blk = lax.dynamic_slice_in_dim(kg, r0, win_r, axis=1)[:, :, key_cols]
        v_blk = lax.dynamic_slice_in_dim(vg, r0, win_r, axis=1)[:, :, key_cols]
        q_i = qg[:, i]
        s_lat = jnp.einsum('bnqhd,brnkhd->bhnqrk', q_i, k_blk).astype(jnp.float32) * scale
        dr = r0 + jnp.arange(win_r) - i + NA_WIN_R - 1
        bias = jnp.take(bias_c, dr, axis=1).transpose(0, 2, 3, 1, 4)
        s_lat = jnp.where(mask, s_lat + bias, NEG_INF)
        s_ctx = jnp.einsum('bnqhd,blhd->bhnql', q_i, kc).astype(jnp.float32) * scale
        s = jnp.concatenate([s_lat.reshape(bsz, h, n_blk, NA_WIN_C, n_lat), s_ctx], axis=-1)
        p = jax.nn.softmax(s, axis=-1).astype(v.dtype)
        p_lat = p[..., :n_lat].reshape(bsz, h, n_blk, NA_WIN_C, win_r, NA_KEY_SPAN)
        o = jnp.einsum('bhnqrk,brnkhd->bnqhd', p_lat, v_blk) + jnp.einsum('bhnql,blhd->bnqhd', p[..., n_lat:], vc)
        return o.reshape(bsz, GRID_W, h * dh)

    out = lax.map(row_block, jnp.arange(rows))
    out = jnp.swapaxes(out, 0, 1).reshape(bsz, n, h * dh)
    out_ctx = None
    if need_ctx:
        s = jnp.einsum('blhd,bmhd->bhlm', qc, kc).astype(jnp.float32) * scale
        p = jax.nn.softmax(s, axis=-1).astype(vc.dtype)
        out_ctx = jnp.einsum('bhlm,bmhd->blhd', p, vc).reshape(bsz, qc.shape[1], h * dh)
    return out, out_ctx


def centred_shift(x, mu):
    pad = jnp.pad(x, ((0, 0), (1, 1), (0, 0)))
    return x + (0.5 * (pad[:, :-2] + pad[:, 2:]) - x) * mu


def rwkv_prepare(z, w0, w2, a0, a2, g2, k_k, k_a):
    bsz, n, _ = z.shape
    zf = z.astype(jnp.float32)
    W, R, RA = RK_WIDTH, RK_DECAY_RANK, RK_A_RANK

    def heads(t):
        return t.reshape(bsz, n, RK_HEADS, HEAD_DIM)

    r, k, v = zf[..., :W], zf[..., W:2 * W], zf[..., 2 * W:3 * W]
    o = 3 * W
    zw = (zf[..., o:o + R], zf[..., o + R:o + 2 * R])
    o = o + 2 * R
    za = (zf[..., o:o + RA], zf[..., o + RA:o + 2 * RA])
    o = o + 2 * RA
    g = jax.nn.sigmoid(zf[..., o:o + RK_GATE_RANK]) @ g2.astype(jnp.float32)
    kk = heads(k * k_k.astype(jnp.float32))
    kk = kk / jnp.maximum(jnp.sqrt(jnp.sum(kk * kk, axis=-1, keepdims=True)), 1e-12)
    dirs = []
    for d in range(2):
        w = -jax.nn.softplus(-(w0[d].astype(jnp.float32) + jnp.tanh(zw[d]) @ w2[d].astype(jnp.float32))) - 0.5
        a = jax.nn.sigmoid(a0[d].astype(jnp.float32) + za[d] @ a2[d].astype(jnp.float32))
        kd = k * (1.0 + (a - 1.0) * k_a.astype(jnp.float32))
        dirs.append((heads(jnp.exp(-jnp.exp(w))), heads(kd), kk * heads(a)))
    return heads(r), heads(v), kk, g, dirs


def rwkv_scan(r, decay, k, v, a, b, s0, reverse):
    xs = tuple(jnp.moveaxis(t, 1, 0) for t in (r, decay, k, v, a, b))

    def step(S, inp):
        r_t, w_t, k_t, v_t, a_t, b_t = inp
        sa = jnp.einsum('bhij,bhj->bhi', S, a_t)
        S = S * w_t[:, :, None, :] + sa[..., None] * b_t[:, :, None, :] + v_t[..., None] * k_t[:, :, None, :]
        return S, jnp.einsum('bhij,bhj->bhi', S, r_t)

    S, ys = lax.scan(step, s0, xs, reverse=reverse)
    return jnp.moveaxis(ys, 0, 1), S


def rwkv_mixer(z_lat, z_ctx, mu, w0, w2, a0, a2, g2, k_k, k_a, r_k, ln_w, ln_b, need_ctx):
    mu = mu.astype(jnp.float32)
    lat = rwkv_prepare(centred_shift(z_lat.astype(jnp.float32), mu), w0, w2, a0, a2, g2, k_k, k_a)
    cx = rwkv_prepare(centred_shift(z_ctx.astype(jnp.float32), mu), w0, w2, a0, a2, g2, k_k, k_a)
    s_zero = jnp.zeros((z_ctx.shape[0], RK_HEADS, HEAD_DIM, HEAD_DIM), jnp.float32)
    ys_lat, ys_ctx = [], []
    for d, rev in enumerate((False, True)):
        dec, kd, b = cx[4][d]
        yc, s_ctx = rwkv_scan(cx[0], dec, kd, cx[1], -cx[2], b, s_zero, rev)
        dec, kd, b = lat[4][d]
        yl, _ = rwkv_scan(lat[0], dec, kd, lat[1], -lat[2], b, s_ctx, rev)
        ys_lat.append(yl)
        ys_ctx.append(yc)
    r_k = r_k.astype(jnp.float32)

    def finish(prep, y):
        r, v, _, g, dirs = prep
        bsz, n = y.shape[:2]
        mean = jnp.mean(y, axis=-1, keepdims=True)
        var = jnp.mean(jnp.square(y - mean), axis=-1, keepdims=True)
        yn = ((y - mean) * lax.rsqrt(var + RK_GN_EPS)).reshape(bsz, n, RK_WIDTH)
        yn = yn * ln_w.astype(jnp.float32) + ln_b.astype(jnp.float32)
        bonus = (jnp.sum(r * dirs[0][1] * r_k, axis=-1, keepdims=True)
                 + jnp.sum(r * dirs[1][1] * r_k, axis=-1, keepdims=True)) * v
        return (yn + bonus.reshape(bsz, n, RK_WIDTH)) * g

    out_lat = finish(lat, ys_lat[0] + ys_lat[1]).astype(z_lat.dtype)
    out_ctx = finish(cx, ys_ctx[0] + ys_ctx[1]).astype(z_ctx.dtype) if need_ctx else None
    return out_lat, out_ctx


def expert_choice_ffn(h, w_router, w_gate, w_up, w_down):
    bsz, n, d = h.shape
    cap = EC_CAPACITY_FACTOR * n // N_EXPERTS
    aff = jax.nn.softmax((h @ w_router).astype(jnp.float32), axis=-1)
    gate, idx = lax.top_k(jnp.swapaxes(aff, 1, 2), cap)
    xin = jax.vmap(lambda hb, ib: hb[ib])(h, idx)
    hid = jax.nn.silu(jnp.einsum('becd,edf->becf', xin, w_gate)) * jnp.einsum('becd,edf->becf', xin, w_up)
    y = jnp.einsum('becf,efd->becd', hid, w_down) * gate[..., None].astype(h.dtype)
    return jax.vmap(lambda ib, yb: jnp.zeros((n, d), yb.dtype).at[ib.reshape(-1)].add(yb.reshape(-1, d)))(idx, y)


def kernel(x, c, ctx, c_ctx, ada_w, ada_b, norm_g, w_in, w_out, s5_lam_re, s5_lam_im, s5_log_dt, s5_b_re, s5_b_im, s5_c_re, s5_c_im, s5_d, s5_glu_w, s5_glu_b, na_rpb, rk_mu, rk_w0, rk_w2, rk_a0, rk_a2, rk_g2, rk_k_k, rk_k_a, rk_r_k, rk_ln_w, rk_ln_b, ec_router, ec_w_gate, ec_w_up, ec_w_down):
    o_na = S5_WIDTH
    o_rk = S5_WIDTH + 3 * NA_WIDTH
    bsz = x.shape[0]

    def na_heads(p, i):
        t = p[..., o_na + i * NA_WIDTH:o_na + (i + 1) * NA_WIDTH]
        return t.reshape(t.shape[0], t.shape[1], NA_HEADS, HEAD_DIM)

    for l in range(DEPTH):
        need_ctx = l < DEPTH - 1
        mod = jax.nn.silu(c) @ ada_w[l] + ada_b[l]
        mod_c = jax.nn.silu(c_ctx) @ ada_w[l] + ada_b[l]
        sh1, sc1, gt1, sh2, sc2, gt2 = jnp.split(mod, 6, axis=-1)
        csh1, csc1, cgt1, csh2, csc2, cgt2 = (jnp.broadcast_to(t, (bsz, D_MODEL))
                                              for t in jnp.split(mod_c, 6, axis=-1))
        w_in_l = w_in[l].astype(jnp.bfloat16)
        w_out_l = w_out[l].astype(jnp.bfloat16)

        p_lat = _inproj(x, norm_g[l, 0], sc1, sh1, w_in_l, 512)
        p_ctx = _inproj(ctx, norm_g[l, 0], csc1, csh1, w_in_l, 256)

        y_s5, yc_s5 = s5_mixer(p_lat[..., :S5_WIDTH], p_ctx[..., :S5_WIDTH], s5_lam_re[l], s5_lam_im[l],
                               s5_log_dt[l], s5_b_re[l], s5_b_im[l], s5_c_re[l], s5_c_im[l], s5_d[l],
                               s5_glu_w[l], s5_glu_b[l], need_ctx)
        q, k, v = (na_heads(p_lat, i) for i in range(3))
        qc, kc, vc = (na_heads(p_ctx, i) for i in range(3))
        y_na, yc_na = na_mixer(axial_rope(q), axial_rope(k), v, qc, kc, vc, na_rpb[l], need_ctx)
        y_rk, yc_rk = rwkv_mixer(p_lat[..., o_rk:], p_ctx[..., o_rk:], rk_mu[l], rk_w0[l], rk_w2[l], rk_a0[l],
                                 rk_a2[l], rk_g2[l], rk_k_k[l], rk_k_a[l], rk_r_k[l], rk_ln_w[l], rk_ln_b[l],
                                 need_ctx)

        x = _outproj(jnp.concatenate([y_s5, y_na, y_rk], axis=-1), w_out_l, norm_g[l, 1], gt1, x, 512)

        h2 = rmsnorm(x, norm_g[l, 2]) * (1.0 + sc2[:, None, :]) + sh2[:, None, :]
        f = expert_choice_ffn(h2, ec_router[l], ec_w_gate[l], ec_w_up[l], ec_w_down[l])
        x = x + gt2[:, None, :] * rmsnorm(f, norm_g[l, 3])

        if need_ctx:
            ctx = _outproj(jnp.concatenate([yc_s5, yc_na, yc_rk], axis=-1), w_out_l, norm_g[l, 1], cgt1, ctx, 256)
            hc2 = rmsnorm(ctx, norm_g[l, 2]) * (1.0 + csc2[:, None, :]) + csh2[:, None, :]
            fc = expert_choice_ffn(hc2, ec_router[l], ec_w_gate[l], ec_w_up[l], ec_w_down[l])
            ctx = ctx + cgt2[:, None, :] * rmsnorm(fc, norm_g[l, 3])
    return x
```

```python
import functools
import math

import jax
import jax.numpy as jnp
import numpy as np
from jax import lax
from jax.experimental import pallas as pl
from jax.experimental.pallas import tpu as pltpu

D_MODEL = 1024
DEPTH = 2
GRID_W = 64
D_MIX = D_MODEL
HEAD_DIM = 64
S5_WIDTH = D_MIX // 4
S5_GROUP = 16
S5_GROUPS = S5_WIDTH // S5_GROUP
S5_STATE = 64
NA_WIDTH = (D_MIX - S5_WIDTH) // 2
NA_HEADS = NA_WIDTH // HEAD_DIM
NA_WIN_R = 8
NA_WIN_C = 16
NA_KEY_SPAN = 2 * NA_WIN_C
ROPE_BASE = 10000.0
RK_WIDTH = D_MIX - S5_WIDTH - NA_WIDTH
RK_HEADS = RK_WIDTH // HEAD_DIM
RK_DECAY_RANK = 64
RK_A_RANK = 64
RK_GATE_RANK = 128
RK_IN_WIDTH = 3 * RK_WIDTH + 2 * RK_DECAY_RANK + 2 * RK_A_RANK + RK_GATE_RANK
RK_GN_EPS = 64e-5
N_IN = S5_WIDTH + 3 * NA_WIDTH + RK_IN_WIDTH
N_EXPERTS = 16
EC_CAPACITY_FACTOR = 2
RMS_EPS = 1e-6
NEG_INF = -1e30

VMEM_LIMIT_BYTES = 48 * 1024 * 1024


def _compiler_params(semantics):
    return pltpu.CompilerParams(dimension_semantics=semantics, vmem_limit_bytes=VMEM_LIMIT_BYTES)


def _inproj_kernel(x_ref, g_ref, sc_ref, sh_ref, w_ref, o_ref):
    x = x_ref[...]
    y = x * lax.rsqrt(jnp.mean(x * x, axis=-1, keepdims=True) + RMS_EPS)
    h = (y * g_ref[...]) * (1.0 + sc_ref[...]) + sh_ref[...]
    o_ref[...] = jnp.dot(h.astype(jnp.bfloat16), w_ref[...], preferred_element_type=jnp.float32)


def _inproj(x, g, sc, sh, w_bf16, tn):
    bsz, n, d = x.shape
    nw = w_bf16.shape[1]
    tn = min(tn, n)
    return pl.pallas_call(
        _inproj_kernel,
        grid=(bsz, n // tn),
        in_specs=[
            pl.BlockSpec((None, tn, d), lambda b, i: (b, i, 0)),
            pl.BlockSpec((1, d), lambda b, i: (0, 0)),
            pl.BlockSpec((None, 1, d), lambda b, i: (b, 0, 0)),
            pl.BlockSpec((None, 1, d), lambda b, i: (b, 0, 0)),
            pl.BlockSpec((d, nw), lambda b, i: (0, 0)),
        ],
        out_specs=pl.BlockSpec((None, tn, nw), lambda b, i: (b, i, 0)),
        out_shape=jax.ShapeDtypeStruct((bsz, n, nw), jnp.float32),
        compiler_params=_compiler_params(("parallel", "parallel")),
    )(x, g.reshape(1, d), sc.reshape(bsz, 1, d), sh.reshape(bsz, 1, d), w_bf16)


def _outproj_kernel(y_ref, w_ref, g_ref, gt_ref, x_ref, o_ref):
    z = jnp.dot(y_ref[...].astype(jnp.bfloat16), w_ref[...], preferred_element_type=jnp.float32)
    zn = z * lax.rsqrt(jnp.mean(z * z, axis=-1, keepdims=True) + RMS_EPS)
    o_ref[...] = x_ref[...] + gt_ref[...] * (zn * g_ref[...])


def _outproj(y, w_bf16, g, gt, x, tn):
    bsz, n, k = y.shape
    d = w_bf16.shape[1]
    tn = min(tn, n)
    return pl.pallas_call(
        _outproj_kernel,
        grid=(bsz, n // tn),
        in_specs=[
            pl.BlockSpec((None, tn, k), lambda b, i: (b, i, 0)),
            pl.BlockSpec((k, d), lambda b, i: (0, 0)),
            pl.BlockSpec((1, d), lambda b, i: (0, 0)),
            pl.BlockSpec((None, 1, d), lambda b, i: (b, 0, 0)),
            pl.BlockSpec((None, tn, d), lambda b, i: (b, i, 0)),
        ],
        out_specs=pl.BlockSpec((None, tn, d), lambda b, i: (b, i, 0)),
        out_shape=jax.ShapeDtypeStruct((bsz, n, d), jnp.float32),
        compiler_params=_compiler_params(("parallel", "parallel")),
    )(y, w_bf16, g.reshape(1, d), gt.reshape(bsz, 1, d), x)


def rmsnorm(x, g):
    xf = x.astype(jnp.float32)
    y = xf * lax.rsqrt(jnp.mean(xf * xf, axis=-1, keepdims=True) + RMS_EPS)
    return (y * g.astype(jnp.float32)).astype(x.dtype)


def axial_rope(x):
    n = x.shape[1]
    t = jnp.arange(n)
    row = (t // GRID_W).astype(jnp.float32)
    col = (t % GRID_W).astype(jnp.float32)
    half = x.shape[-1] // 2
    nf = half // 2
    inv_freq = ROPE_BASE ** (-jnp.arange(nf, dtype=jnp.float32) / nf)

    def rot(xp, pos):
        ang = pos[:, None] * inv_freq
        cos = jnp.cos(ang)[None, :, None, :]
        sin = jnp.sin(ang)[None, :, None, :]
        x1, x2 = xp[..., :nf], xp[..., nf:]
        return jnp.concatenate([x1 * cos - x2 * sin, x1 * sin + x2 * cos], axis=-1)

    xf = x.astype(jnp.float32)
    return jnp.concatenate([rot(xf[..., :half], row), rot(xf[..., half:], col)], axis=-1).astype(x.dtype)


S5_GP = S5_GROUPS * S5_STATE
S5_CHUNK = 64
S5_LANES = 512


def _split_bf16(w):
    hi = w.astype(jnp.bfloat16)
    lo = (w - hi.astype(jnp.float32)).astype(jnp.bfloat16)
    return hi, lo


def _dot3(a, w_hi, w_lo):
    a_hi = a.astype(jnp.bfloat16)
    a_lo = (a - a_hi.astype(jnp.float32)).astype(jnp.bfloat16)
    f = functools.partial(jnp.dot, preferred_element_type=jnp.float32)
    return f(a_hi, w_hi) + (f(a_lo, w_hi) + f(a_hi, w_lo))


def _s5_kernel(uf_ref, ub_ref, lam_ref, bhi_ref, blo_ref, chi_ref, clo_ref, yf_ref, yb_ref, sbuf, st, *, T, B):
    c = pl.program_id(0)

    @pl.when(c == 0)
    def _():
        st[...] = jnp.zeros_like(st)

    for d, (u_ref, y_ref) in enumerate(((uf_ref, yf_ref), (ub_ref, yb_ref))):
        sbuf[d] = _dot3(u_ref[...], bhi_ref[d], blo_ref[d])
        for h in range(S5_GP // S5_LANES):
            re_sl = pl.ds(h * S5_LANES, S5_LANES)
            im_sl = pl.ds(S5_GP + h * S5_LANES, S5_LANES)
            lr = jnp.broadcast_to(lam_ref[d, 0:1, h * S5_LANES:(h + 1) * S5_LANES], (B, S5_LANES))
            li = jnp.broadcast_to(lam_ref[d, 1:2, h * S5_LANES:(h + 1) * S5_LANES], (B, S5_LANES))

            def step(i, carry, d=d, re_sl=re_sl, im_sl=im_sl, lr=lr, li=li):
                s_re, s_im = carry
                t = i if d == 0 else T - 1 - i
                rows = pl.ds(pl.multiple_of(t * B, B), B)
                n_re = lr * s_re - li * s_im + sbuf[d, rows, re_sl]
                n_im = lr * s_im + li * s_re + sbuf[d, rows, im_sl]
                sbuf[d, rows, re_sl] = n_re
                sbuf[d, rows, im_sl] = n_im
                return n_re, n_im

            s_re, s_im = lax.fori_loop(0, T, step, (st[d, :, re_sl], st[d, :, im_sl]), unroll=4)
            st[d, :, re_sl] = s_re
            st[d, :, im_sl] = s_im
        y_ref[...] = _dot3(sbuf[d], chi_ref[d], clo_ref[d])


def _s5_scan(uf, lam, bblk, cblk, bsz, n_ctx):
    T = S5_CHUNK
    nc = uf.shape[0] // (T * bsz)
    ncc = n_ctx // T
    bhi, blo = _split_bf16(bblk)
    chi, clo = _split_bf16(cblk)

    def bmap(c):
        return jnp.where(c < ncc, ncc - 1 - c, nc + ncc - 1 - c)

    blk = (T * bsz, S5_WIDTH)

    def full(shape):
        return pl.BlockSpec(shape, lambda c: (0,) * len(shape))

    return pl.pallas_call(
        functools.partial(_s5_kernel, T=T, B=bsz),
        grid=(nc,),
        in_specs=[pl.BlockSpec(blk, lambda c: (c, 0)), pl.BlockSpec(blk, lambda c: (bmap(c), 0)),
                  full(lam.shape), full(bhi.shape), full(blo.shape), full(chi.shape), full(clo.shape)],
        out_specs=[pl.BlockSpec(blk, lambda c: (c, 0)), pl.BlockSpec(blk, lambda c: (bmap(c), 0))],
        out_shape=[jax.ShapeDtypeStruct(uf.shape, jnp.float32)] * 2,
        scratch_shapes=[pltpu.VMEM((2, T * bsz, 2 * S5_GP), jnp.float32),
                        pltpu.VMEM((2, bsz, 2 * S5_GP), jnp.float32)],
        compiler_params=_compiler_params(("arbitrary",)),
    )(uf, uf, lam, bhi, blo, chi, clo)


def _s5_params(lam_re, lam_im, log_dt, b_re, b_im, c_re, c_im):
    lams, bs, cs = [], [], []
    eye = jnp.eye(S5_GROUPS, dtype=jnp.float32)
    for d in range(2):
        dt = jnp.exp(log_dt[d])[:, None]
        mag = jnp.exp(lam_re[d] * dt)
        lb_re, lb_im = mag * jnp.cos(lam_im[d] * dt), mag * jnp.sin(lam_im[d] * dt)
        den = lam_re[d] ** 2 + lam_im[d] ** 2
        nr, ni = lb_re - 1.0, lb_im
        f_re = (nr * lam_re[d] + ni * lam_im[d]) / den
        f_im = (ni * lam_re[d] - nr * lam_im[d]) / den
        bb_re = f_re[..., None] * b_re - f_im[..., None] * b_im
        bb_im = f_re[..., None] * b_im + f_im[..., None] * b_re

        def blockdiag_in(m):
            return jnp.einsum('gph,gk->ghkp', m, eye).reshape(S5_WIDTH, S5_GP)

        def blockdiag_out(m):
            return jnp.einsum('ghp,gk->gpkh', m, eye).reshape(S5_GP, S5_WIDTH)

        bs.append(jnp.concatenate([blockdiag_in(bb_re), blockdiag_in(bb_im)], axis=1))
        cs.append(jnp.concatenate([blockdiag_out(c_re[d]), -blockdiag_out(c_im[d])], axis=0))
        lams.append(jnp.stack([lb_re.reshape(S5_GP), lb_im.reshape(S5_GP)]))
    return jnp.stack(lams), jnp.stack(bs), jnp.stack(cs)


def s5_mixer(u_lat, u_ctx, lam_re, lam_im, log_dt, b_re, b_im, c_re, c_im, d_skip, glu_w, glu_b, need_ctx):
    bsz, n_ctx = u_ctx.shape[0], u_ctx.shape[1]
    u = jnp.concatenate([u_ctx, u_lat], axis=1)
    nt = u.shape[1]
    lam, bblk, cblk = _s5_params(lam_re, lam_im, log_dt, b_re, b_im, c_re, c_im)
    yf, yb = _s5_scan(jnp.swapaxes(u, 0, 1).reshape(nt * bsz, S5_WIDTH), lam, bblk, cblk, bsz, n_ctx)
    y = d_skip * u + jnp.swapaxes((yf + yb).reshape(nt, bsz, S5_WIDTH), 0, 1)

    def glu(y):
        y = jax.nn.gelu(y)
        return y * jax.nn.sigmoid(y @ glu_w.astype(jnp.float32) + glu_b.astype(jnp.float32))

    out_lat = glu(y[:, n_ctx:])
    out_ctx = glu(y[:, :n_ctx]) if need_ctx else None
    return out_lat, out_ctx


def na_geometry():
    n_blk = GRID_W // NA_WIN_C
    q_cols = np.arange(GRID_W).reshape(n_blk, NA_WIN_C)
    start = np.clip(np.arange(n_blk) * NA_WIN_C - NA_WIN_C // 2, 0, GRID_W - NA_KEY_SPAN)
    key_cols = start[:, None] + np.arange(NA_KEY_SPAN)[None, :]
    c0 = np.clip(q_cols - NA_WIN_C // 2, 0, GRID_W - NA_WIN_C)[..., None]
    kc = key_cols[:, None, :]
    valid = (kc >= c0) & (kc < c0 + NA_WIN_C)
    dc = np.clip(kc - q_cols[..., None], 1 - NA_WIN_C, NA_WIN_C - 1) + NA_WIN_C - 1
    return key_cols.astype(np.int32), valid, dc.astype(np.int32)


def na_mixer(q, k, v, qc, kc, vc, rpb, need_ctx):
    bsz, n, h, dh = q.shape
    rows = n // GRID_W
    win_r = min(NA_WIN_R, rows)
    scale = dh ** -0.5
    key_cols, valid, dc = na_geometry()
    n_blk = key_cols.shape[0]
    qg = q.reshape(bsz, rows, n_blk, NA_WIN_C, h, dh)
    kg = k.reshape(bsz, rows, GRID_W, h, dh)
    vg = v.reshape(bsz, rows, GRID_W, h, dh)
    bias_c = rpb.astype(jnp.float32)[:, :, dc]
    mask = jnp.asarray(valid)[:, :, None, :]
    n_lat = win_r * NA_KEY_SPAN

    def row_block(i):
        r0 = jnp.clip(i - win_r // 2, 0, rows - win_r)
        k_blk = lax.dynamic_slice_in_dim(kg, r0, win_r, axis=1)[:, :, key_cols]
        v_blk = lax.dynamic_slice_in_dim(vg, r0, win_r, axis=1)[:, :, key_cols]
        q_i = qg[:, i]
        s_lat = jnp.einsum('bnqhd,brnkhd->bhnqrk', q_i, k_blk).astype(jnp.float32) * scale
        dr = r0 + jnp.arange(win_r) - i + NA_WIN_R - 1
        bias = jnp.take(bias_c, dr, axis=1).transpose(0, 2, 3, 1, 4)
        s_lat = jnp.where(mask, s_lat + bias, NEG_INF)
        s_ctx = jnp.einsum('bnqhd,blhd->bhnql', q_i, kc).astype(jnp.float32) * scale
        s = jnp.concatenate([s_lat.reshape(bsz, h, n_blk, NA_WIN_C, n_lat), s_ctx], axis=-1)
        p = jax.nn.softmax(s, axis=-1).astype(v.dtype)
        p_lat = p[..., :n_lat].reshape(bsz, h, n_blk, NA_WIN_C, win_r, NA_KEY_SPAN)
        o = jnp.einsum('bhnqrk,brnkhd->bnqhd', p_lat, v_blk) + jnp.einsum('bhnql,blhd->bnqhd', p[..., n_lat:], vc)
        return o.reshape(bsz, GRID_W, h * dh)

    out = lax.map(row_block, jnp.arange(rows))
    out = jnp.swapaxes(out, 0, 1).reshape(bsz, n, h * dh)
    out_ctx = None
    if need_ctx:
        s = jnp.einsum('blhd,bmhd->bhlm', qc, kc).astype(jnp.float32) * scale
        p = jax.nn.softmax(s, axis=-1).astype(vc.dtype)
        out_ctx = jnp.einsum('bhlm,bmhd->blhd', p, vc).reshape(bsz, qc.shape[1], h * dh)
    return out, out_ctx


def centred_shift(x, mu):
    pad = jnp.pad(x, ((0, 0), (1, 1), (0, 0)))
    return x + (0.5 * (pad[:, :-2] + pad[:, 2:]) - x) * mu


def rwkv_prepare(z, w0, w2, a0, a2, g2, k_k, k_a):
    bsz, n, _ = z.shape
    zf = z.astype(jnp.float32)
    W, R, RA = RK_WIDTH, RK_DECAY_RANK, RK_A_RANK

    def heads(t):
        return t.reshape(bsz, n, RK_HEADS, HEAD_DIM)

    r, k, v = zf[..., :W], zf[..., W:2 * W], zf[..., 2 * W:3 * W]
    o = 3 * W
    zw = (zf[..., o:o + R], zf[..., o + R:o + 2 * R])
    o = o + 2 * R
    za = (zf[..., o:o + RA], zf[..., o + RA:o + 2 * RA])
    o = o + 2 * RA
    g = jax.nn.sigmoid(zf[..., o:o + RK_GATE_RANK]) @ g2.astype(jnp.float32)
    kk = heads(k * k_k.astype(jnp.float32))
    kk = kk / jnp.maximum(jnp.sqrt(jnp.sum(kk * kk, axis=-1, keepdims=True)), 1e-12)
    dirs = []
    for d in range(2):
        w = -jax.nn.softplus(-(w0[d].astype(jnp.float32) + jnp.tanh(zw[d]) @ w2[d].astype(jnp.float32))) - 0.5
        a = jax.nn.sigmoid(a0[d].astype(jnp.float32) + za[d] @ a2[d].astype(jnp.float32))
        kd = k * (1.0 + (a - 1.0) * k_a.astype(jnp.float32))
        dirs.append((heads(-jnp.exp(w)), heads(kd), kk * heads(a)))
    return heads(r), heads(v), kk, g, dirs


RK_CHUNK = 64


def _bdot(x, y):
    return jnp.dot(x.astype(jnp.bfloat16), y.astype(jnp.bfloat16), preferred_element_type=jnp.float32)


def _bdot_nt(x, y):
    return lax.dot_general(x.astype(jnp.bfloat16), y.astype(jnp.bfloat16), (((1,), (1,)), ((), ())),
                           preferred_element_type=jnp.float32)


def _bdot_tn(x, y):
    return lax.dot_general(x.astype(jnp.bfloat16), y.astype(jnp.bfloat16), (((0,), (0,)), ((), ())),
                           preferred_element_type=jnp.float32)


def _rwkv_units(units):
    T = units[0][0].shape[0]
    U = range(len(units))
    steps = int(math.log2(T))
    f = functools.partial(jnp.dot, preferred_element_type=jnp.float32)
    ti = lax.broadcasted_iota(jnp.int32, (T, T), 0)
    si = lax.broadcasted_iota(jnp.int32, (T, T), 1)
    ti2 = lax.broadcasted_iota(jnp.int32, (T, 2 * T), 0)
    si2 = lax.broadcasted_iota(jnp.int32, (T, 2 * T), 1) & (T - 1)
    eye = (lax.broadcasted_iota(jnp.int32, (HEAD_DIM, HEAD_DIM), 0)
           == lax.broadcasted_iota(jnp.int32, (HEAD_DIM, HEAD_DIM), 1))
    masks = {}
    for rev in (False, True):
        incl = (si >= ti) if rev else (si <= ti)
        masks[rev] = (incl.astype(jnp.bfloat16), (si2 >= ti2) if rev else (si2 <= ti2),
                      (si2 > ti2) if rev else (si2 < ti2))
    cum = []
    for (r, kk, v, lw, kd, b, h0, rev) in units:
        l1 = lw.astype(jnp.bfloat16)
        rem = lw - l1.astype(jnp.float32)
        l2 = rem.astype(jnp.bfloat16)
        l3 = (rem - l2.astype(jnp.float32)).astype(jnp.bfloat16)
        tri = masks[rev][0]
        cum.append(f(tri, l1) + (f(tri, l2) + f(tri, l3)))
    at, rt, p, cl = [], [], [], []
    for u, (r, kk, v, lw, kd, b, h0, rev) in enumerate(units):
        c = cum[u]
        cl.append(c[0:1, :] if rev else c[T - 1:T, :])
        e_neg = jnp.exp(-c)
        at.append(-kk * jnp.exp(c - lw))
        rt.append(r * jnp.exp(c))
        p.append(_bdot_nt(jnp.concatenate([at[u], rt[u]], axis=0), jnp.concatenate([b * e_neg, kd * e_neg], axis=0)))
    top = [jnp.where(masks[units[u][7]][2], p[u][:T], 0.0) for u in U]
    l2m = [jnp.where(masks[units[u][7]][1], p[u][T:], 0.0) for u in U]
    npow = [top[u][:, :T] for u in U]
    x = [jnp.concatenate([at[u], _bdot(top[u][:, T:], units[u][2])], axis=1) for u in U]
    for i in range(steps):
        x = [x[u] + _bdot(npow[u], x[u]) for u in U]
        if i < steps - 1:
            npow = [_bdot(npow[u], npow[u]) for u in U]
    z = [jnp.concatenate([x[u], jnp.concatenate([jnp.zeros_like(units[u][2]), units[u][2]], axis=1)], axis=0)
         for u in U]
    ry = [_bdot(l2m[u], z[u]) for u in U]
    gj = []
    for u, (r, kk, v, lw, kd, b, h0, rev) in enumerate(units):
        e_end = jnp.exp(cl[u] - cum[u])
        gj.append(_bdot_tn(jnp.concatenate([b * e_end, kd * e_end], axis=0), z[u]))
    out = []
    for u in U:
        g = jnp.where(eye, jnp.exp(cl[u]), 0.0) + gj[u][:, :HEAD_DIM]
        yh = _bdot(jnp.concatenate([rt[u] + ry[u][:, :HEAD_DIM], g], axis=0), units[u][6])
        out.append((yh[:T] + ry[u][:, HEAD_DIM:], yh[T:] + gj[u][:, HEAD_DIM:]))
    return out


def _rwkv_kernel(f_ref, b_ref, yf_ref, yb_ref, hs, *, H):
    c = pl.program_id(1)

    @pl.when(c == 0)
    def _():
        hs[...] = jnp.zeros_like(hs)

    units = []
    for d, i_ref in enumerate((f_ref, b_ref)):
        for h in range(H):
            units.append(tuple(i_ref[j, h] for j in range(6)) + (hs[d, h], d == 1))
    res = _rwkv_units(units)
    for d, y_ref in enumerate((yf_ref, yb_ref)):
        for h in range(H):
            y, hn = res[d * H + h]
            y_ref[h] = y
            hs[d, h] = hn


def _rwkv_scan(fwd, bwd, n_ctx):
    bsz, nf, H, nt, _ = fwd.shape
    T = RK_CHUNK
    nc, ncc = nt // T, n_ctx // T

    def bmap(c):
        return jnp.where(c < ncc, ncc - 1 - c, nc + ncc - 1 - c)

    iblk = (None, nf, H, T, HEAD_DIM)
    oblk = (None, H, T, HEAD_DIM)
    return pl.pallas_call(
        functools.partial(_rwkv_kernel, H=H),
        grid=(bsz, nc),
        in_specs=[pl.BlockSpec(iblk, lambda b, c: (b, 0, 0, c, 0)),
                  pl.BlockSpec(iblk, lambda b, c: (b, 0, 0, bmap(c), 0))],
        out_specs=[pl.BlockSpec(oblk, lambda b, c: (b, 0, c, 0)),
                   pl.BlockSpec(oblk, lambda b, c: (b, 0, bmap(c), 0))],
        out_shape=[jax.ShapeDtypeStruct((bsz, H, nt, HEAD_DIM), jnp.float32)] * 2,
        scratch_shapes=[pltpu.VMEM((2, H, HEAD_DIM, HEAD_DIM), jnp.float32)],
        compiler_params=_compiler_params(("parallel", "arbitrary")),
    )(fwd, bwd)


def rwkv_mixer(z_lat, z_ctx, mu, w0, w2, a0, a2, g2, k_k, k_a, r_k, ln_w, ln_b, need_ctx):
    mu = mu.astype(jnp.float32)
    lat = rwkv_prepare(centred_shift(z_lat.astype(jnp.float32), mu), w0, w2, a0, a2, g2, k_k, k_a)
    cx = rwkv_prepare(centred_shift(z_ctx.astype(jnp.float32), mu), w0, w2, a0, a2, g2, k_k, k_a)
    n_ctx = z_ctx.shape[1]

    def pack(d):
        fields = [jnp.concatenate([cx[i], lat[i]], axis=1) for i in (0, 2, 1)]
        fields += [jnp.concatenate([cx[4][d][j], lat[4][d][j]], axis=1) for j in range(3)]
        return jnp.stack([t.transpose(0, 2, 1, 3) for t in fields], axis=1)

    yf, yb = _rwkv_scan(pack(0), pack(1), n_ctx)
    y_all = (yf + yb).transpose(0, 2, 1, 3)
    ys_lat = [y_all[:, n_ctx:], 0.0]
    ys_ctx = [y_all[:, :n_ctx], 0.0]
    r_k = r_k.astype(jnp.float32)

    def finish(prep, y):
        r, v, _, g, dirs = prep
        bsz, n = y.shape[:2]
        mean = jnp.mean(y, axis=-1, keepdims=True)
        var = jnp.mean(jnp.square(y - mean), axis=-1, keepdims=True)
        yn = ((y - mean) * lax.rsqrt(var + RK_GN_EPS)).reshape(bsz, n, RK_WIDTH)
        yn = yn * ln_w.astype(jnp.float32) + ln_b.astype(jnp.float32)
        bonus = (jnp.sum(r * dirs[0][1] * r_k, axis=-1, keepdims=True)
                 + jnp.sum(r * dirs[1][1] * r_k, axis=-1, keepdims=True)) * v
        return (yn + bonus.reshape(bsz, n, RK_WIDTH)) * g

    out_lat = finish(lat, ys_lat[0] + ys_lat[1]).astype(z_lat.dtype)
    out_ctx = finish(cx, ys_ctx[0] + ys_ctx[1]).astype(z_ctx.dtype) if need_ctx else None
    return out_lat, out_ctx


def expert_choice_ffn(h, w_router, w_gate, w_up, w_down):
    bsz, n, d = h.shape
    cap = EC_CAPACITY_FACTOR * n // N_EXPERTS
    aff = jax.nn.softmax((h @ w_router).astype(jnp.float32), axis=-1)
    gate, idx = lax.top_k(jnp.swapaxes(aff, 1, 2), cap)
    xin = jax.vmap(lambda hb, ib: hb[ib])(h, idx)
    hid = jax.nn.silu(jnp.einsum('becd,edf->becf', xin, w_gate)) * jnp.einsum('becd,edf->becf', xin, w_up)
    y = jnp.einsum('becf,efd->becd', hid, w_down) * gate[..., None].astype(h.dtype)
    return jax.vmap(lambda ib, yb: jnp.zeros((n, d), yb.dtype).at[ib.reshape(-1)].add(yb.reshape(-1, d)))(idx, y)


def kernel(x, c, ctx, c_ctx, ada_w, ada_b, norm_g, w_in, w_out, s5_lam_re, s5_lam_im, s5_log_dt, s5_b_re, s5_b_im, s5_c_re, s5_c_im, s5_d, s5_glu_w, s5_glu_b, na_rpb, rk_mu, rk_w0, rk_w2, rk_a0, rk_a2, rk_g2, rk_k_k, rk_k_a, rk_r_k, rk_ln_w, rk_ln_b, ec_router, ec_w_gate, ec_w_up, ec_w_down):
    o_na = S5_WIDTH
    o_rk = S5_WIDTH + 3 * NA_WIDTH
    bsz = x.shape[0]

    def na_heads(p, i):
        t = p[..., o_na + i * NA_WIDTH:o_na + (i + 1) * NA_WIDTH]
        return t.reshape(t.shape[0], t.shape[1], NA_HEADS, HEAD_DIM)

    for l in range(DEPTH):
        need_ctx = l < DEPTH - 1
        mod = jax.nn.silu(c) @ ada_w[l] + ada_b[l]
        mod_c = jax.nn.silu(c_ctx) @ ada_w[l] + ada_b[l]
        sh1, sc1, gt1, sh2, sc2, gt2 = jnp.split(mod, 6, axis=-1)
        csh1, csc1, cgt1, csh2, csc2, cgt2 = (jnp.broadcast_to(t, (bsz, D_MODEL))
                                              for t in jnp.split(mod_c, 6, axis=-1))
        w_in_l = w_in[l].astype(jnp.bfloat16)
        w_out_l = w_out[l].astype(jnp.bfloat16)

        p_lat = _inproj(x, norm_g[l, 0], sc1, sh1, w_in_l, 512)
        p_ctx = _inproj(ctx, norm_g[l, 0], csc1, csh1, w_in_l, 256)

        y_s5, yc_s5 = s5_mixer(p_lat[..., :S5_WIDTH], p_ctx[..., :S5_WIDTH], s5_lam_re[l], s5_lam_im[l],
                               s5_log_dt[l], s5_b_re[l], s5_b_im[l], s5_c_re[l], s5_c_im[l], s5_d[l],
                               s5_glu_w[l], s5_glu_b[l], need_ctx)
        q, k, v = (na_heads(p_lat, i) for i in range(3))
        qc, kc, vc = (na_heads(p_ctx, i) for i in range(3))
        y_na, yc_na = na_mixer(axial_rope(q), axial_rope(k), v, qc, kc, vc, na_rpb[l], need_ctx)
        y_rk, yc_rk = rwkv_mixer(p_lat[..., o_rk:], p_ctx[..., o_rk:], rk_mu[l], rk_w0[l], rk_w2[l], rk_a0[l],
                                 rk_a2[l], rk_g2[l], rk_k_k[l], rk_k_a[l], rk_r_k[l], rk_ln_w[l], rk_ln_b[l],
                                 need_ctx)

        x = _outproj(jnp.concatenate([y_s5, y_na, y_rk], axis=-1), w_out_l, norm_g[l, 1], gt1, x, 512)

        h2 = rmsnorm(x, norm_g[l, 2]) * (1.0 + sc2[:, None, :]) + sh2[:, None, :]
        f = expert_choice_ffn(h2, ec_router[l], ec_w_gate[l], ec_w_up[l], ec_w_down[l])
        x = x + gt2[:, None, :] * rmsnorm(f, norm_g[l, 3])

        if need_ctx:
            ctx = _outproj(jnp.concatenate([yc_s5, yc_na, yc_rk], axis=-1), w_out_l, norm_g[l, 1], cgt1, ctx, 256)
            hc2 = rmsnorm(ctx, norm_g[l, 2]) * (1.0 + csc2[:, None, :]) + csh2[:, None, :]
            fc = expert_choice_ffn(hc2, ec_router[l], ec_w_gate[l], ec_w_up[l], ec_w_down[l])
            ctx = ctx + cgt2[:, None, :] * rmsnorm(fc, norm_g[l, 3])
    return x
```

```python
import functools
import math

import jax
import jax.numpy as jnp
import numpy as np
from jax import lax
from jax.experimental import pallas as pl
from jax.experimental.pallas import tpu as pltpu

D_MODEL = 1024
DEPTH = 2
GRID_W = 64
D_MIX = D_MODEL
HEAD_DIM = 64
S5_WIDTH = D_MIX // 4
S5_GROUP = 16
S5_GROUPS = S5_WIDTH // S5_GROUP
S5_STATE = 64
NA_WIDTH = (D_MIX - S5_WIDTH) // 2
NA_HEADS = NA_WIDTH // HEAD_DIM
NA_WIN_R = 8
NA_WIN_C = 16
ROPE_BASE = 10000.0
RK_WIDTH = D_MIX - S5_WIDTH - NA_WIDTH
RK_HEADS = RK_WIDTH // HEAD_DIM
RK_DECAY_RANK = 64
RK_A_RANK = 64
RK_GATE_RANK = 128
RK_IN_WIDTH = 3 * RK_WIDTH + 2 * RK_DECAY_RANK + 2 * RK_A_RANK + RK_GATE_RANK
RK_GN_EPS = 64e-5
N_IN = S5_WIDTH + 3 * NA_WIDTH + RK_IN_WIDTH
N_EXPERTS = 16
EC_CAPACITY_FACTOR = 2
RMS_EPS = 1e-6
NEG_INF = -1e30

VMEM_LIMIT_BYTES = 56 * 1024 * 1024
TOKEN_TILE = 256
S5_GP = S5_GROUPS * S5_STATE
S5_CHUNK = 64
S5_LANES = 512
RK_CHUNK = 64
FFN_ROWS = 512
FFN_FCHUNK = 704


def _compiler_params(semantics):
    return pltpu.CompilerParams(dimension_semantics=semantics, vmem_limit_bytes=VMEM_LIMIT_BYTES)


def _token_tile(n_ctx):
    return min(TOKEN_TILE, n_ctx)


def _mod_specs(bsz, d, n_ctx):
    cb = n_ctx // _token_tile(n_ctx)
    return [pl.BlockSpec((None, None, None, 1, d), lambda b, i, j=j: (b, jnp.minimum(i // cb, 1), j, 0, 0))
            for j in range(2)]


def _dotf(x, y):
    return jnp.dot(x, y, preferred_element_type=jnp.float32)


def _bf(x):
    return x.astype(jnp.bfloat16)


def _rope_tables(n_ctx, n):
    t = np.arange(n)
    nf = HEAD_DIM // 4
    inv_freq = ROPE_BASE ** (-np.arange(nf, dtype=np.float32) / nf)
    pos = np.stack([(t // GRID_W).astype(np.float32), (t % GRID_W).astype(np.float32)], axis=1)
    ang = pos[:, :, None] * inv_freq[None, None, :]
    cos = np.repeat(np.cos(ang), 2, axis=1).reshape(n, HEAD_DIM)
    sin = np.sin(ang)
    sin = np.stack([-sin[:, 0], sin[:, 0], -sin[:, 1], sin[:, 1]], axis=1).reshape(n, HEAD_DIM)
    cos = np.concatenate([np.ones((n_ctx, HEAD_DIM), np.float32), cos.astype(np.float32)], axis=0)
    sin = np.concatenate([np.zeros((n_ctx, HEAD_DIM), np.float32), sin.astype(np.float32)], axis=0)
    reps = 2 * NA_HEADS
    return jnp.asarray(np.tile(cos, (1, reps))), jnp.asarray(np.tile(sin, (1, reps)))


def _inproj_kernel(x_ref, g_ref, sc_ref, sh_ref, w_ref, cos_ref, sin_ref, u_ref, qkv_ref, z_ref):
    x = x_ref[...]
    y = x * lax.rsqrt(jnp.mean(x * x, axis=-1, keepdims=True) + RMS_EPS)
    h = _bf((y * g_ref[...]) * (1.0 + sc_ref[...]) + sh_ref[...])
    u_ref[...] = _dotf(h, w_ref[:, :S5_WIDTH])
    z_ref[...] = _dotf(h, w_ref[:, S5_WIDTH + 3 * NA_WIDTH:])
    qk = _dotf(h, w_ref[:, S5_WIDTH:S5_WIDTH + 2 * NA_WIDTH])
    nf = HEAD_DIM // 4
    lane = lax.broadcasted_iota(jnp.int32, qk.shape, 1)
    first = (lane & (2 * nf - 1)) < nf
    partner = jnp.where(first, pltpu.roll(qk, qk.shape[1] - nf, axis=1), pltpu.roll(qk, nf, axis=1))
    qk = _bf(qk * cos_ref[...] + partner * sin_ref[...])
    v = _bf(_dotf(h, w_ref[:, S5_WIDTH + 2 * NA_WIDTH:S5_WIDTH + 3 * NA_WIDTH]))
    for hd in range(NA_HEADS):
        lo = hd * HEAD_DIM
        qkv_ref[0, hd] = qk[:, lo:lo + HEAD_DIM]
        qkv_ref[1, hd] = qk[:, NA_WIDTH + lo:NA_WIDTH + lo + HEAD_DIM]
        qkv_ref[2, hd] = v[:, lo:lo + HEAD_DIM]


def _inproj(xc, g, mods, w_bf16, cos, sin, n_ctx):
    bsz, nt, d = xc.shape
    tn = _token_tile(n_ctx)
    return pl.pallas_call(
        _inproj_kernel,
        grid=(bsz, nt // tn),
        in_specs=[pl.BlockSpec((None, tn, d), lambda b, i: (b, i, 0)),
                  pl.BlockSpec((1, d), lambda b, i: (0, 0)),
                  *_mod_specs(bsz, d, n_ctx),
                  pl.BlockSpec((d, N_IN), lambda b, i: (0, 0)),
                  pl.BlockSpec((tn, 2 * NA_WIDTH), lambda b, i: (i, 0)),
                  pl.BlockSpec((tn, 2 * NA_WIDTH), lambda b, i: (i, 0))],
        out_specs=[pl.BlockSpec((tn, S5_WIDTH), lambda b, i: (i, b)),
                   pl.BlockSpec((None, 3, NA_HEADS, tn, HEAD_DIM), lambda b, i: (b, 0, 0, i, 0)),
                   pl.BlockSpec((None, tn, RK_IN_WIDTH), lambda b, i: (b, i, 0))],
        out_shape=[jax.ShapeDtypeStruct((nt, bsz * S5_WIDTH), jnp.float32),
                   jax.ShapeDtypeStruct((bsz, 3, NA_HEADS, nt, HEAD_DIM), jnp.bfloat16),
                   jax.ShapeDtypeStruct((bsz, nt, RK_IN_WIDTH), jnp.float32)],
        compiler_params=_compiler_params(("parallel", "parallel")),
    )(xc, g.reshape(1, d), mods, mods, w_bf16, cos, sin)


def _split_bf16(w):
    hi = w.astype(jnp.bfloat16)
    lo = (w - hi.astype(jnp.float32)).astype(jnp.bfloat16)
    return hi, lo


def _dot3(a, w_hi, w_lo):
    a_hi = a.astype(jnp.bfloat16)
    a_lo = (a - a_hi.astype(jnp.float32)).astype(jnp.bfloat16)
    return _dotf(a_hi, w_hi) + (_dotf(a_lo, w_hi) + _dotf(a_hi, w_lo))


def _s5_kernel(uf_ref, ub_ref, lam_ref, bhi_ref, blo_ref, chi_ref, clo_ref, yf_ref, yb_ref, sbuf, st, *, T, B):
    c = pl.program_id(0)

    @pl.when(c == 0)
    def _():
        st[...] = jnp.zeros_like(st)

    for d, (u_ref, y_ref) in enumerate(((uf_ref, yf_ref), (ub_ref, yb_ref))):
        sbuf[d] = _dot3(u_ref[...], bhi_ref[d], blo_ref[d])
        for h in range(S5_GP // S5_LANES):
            re_sl = pl.ds(h * S5_LANES, S5_LANES)
            im_sl = pl.ds(S5_GP + h * S5_LANES, S5_LANES)
            lr = jnp.broadcast_to(lam_ref[d, 0:1, h * S5_LANES:(h + 1) * S5_LANES], (B, S5_LANES))
            li = jnp.broadcast_to(lam_ref[d, 1:2, h * S5_LANES:(h + 1) * S5_LANES], (B, S5_LANES))

            def step(i, carry, d=d, re_sl=re_sl, im_sl=im_sl, lr=lr, li=li):
                s_re, s_im = carry
                t = i if d == 0 else T - 1 - i
                rows = pl.ds(pl.multiple_of(t * B, B), B)
                n_re = lr * s_re - li * s_im + sbuf[d, rows, re_sl]
                n_im = lr * s_im + li * s_re + sbuf[d, rows, im_sl]
                sbuf[d, rows, re_sl] = n_re
                sbuf[d, rows, im_sl] = n_im
                return n_re, n_im

            s_re, s_im = lax.fori_loop(0, T, step, (st[d, :, re_sl], st[d, :, im_sl]), unroll=4)
            st[d, :, re_sl] = s_re
            st[d, :, im_sl] = s_im
        y_ref[...] = _dot3(sbuf[d], chi_ref[d], clo_ref[d])


def _backward_chunk(c, nc, ncc):
    return jnp.where(c < ncc, ncc - 1 - c, nc + ncc - 1 - c)


def _s5_scan(uf, lam, bblk, cblk, bsz, n_ctx):
    T = S5_CHUNK
    nc = uf.shape[0] // (T * bsz)
    ncc = n_ctx // T
    bhi, blo = _split_bf16(bblk)
    chi, clo = _split_bf16(cblk)
    bmap = functools.partial(_backward_chunk, nc=nc, ncc=ncc)
    blk = (T * bsz, S5_WIDTH)

    def full(shape):
        return pl.BlockSpec(shape, lambda c: (0,) * len(shape))

    return pl.pallas_call(
        functools.partial(_s5_kernel, T=T, B=bsz),
        grid=(nc,),
        in_specs=[pl.BlockSpec(blk, lambda c: (c, 0)), pl.BlockSpec(blk, lambda c: (bmap(c), 0)),
                  full(lam.shape), full(bhi.shape), full(blo.shape), full(chi.shape), full(clo.shape)],
        out_specs=[pl.BlockSpec(blk, lambda c: (c, 0)), pl.BlockSpec(blk, lambda c: (bmap(c), 0))],
        out_shape=[jax.ShapeDtypeStruct(uf.shape, jnp.float32)] * 2,
        scratch_shapes=[pltpu.VMEM((2, T * bsz, 2 * S5_GP), jnp.float32),
                        pltpu.VMEM((2, bsz, 2 * S5_GP), jnp.float32)],
        compiler_params=_compiler_params(("arbitrary",)),
    )(uf, uf, lam, bhi, blo, chi, clo)


def _s5_params(lam_re, lam_im, log_dt, b_re, b_im, c_re, c_im):
    lams, bs, cs = [], [], []
    eye = jnp.eye(S5_GROUPS, dtype=jnp.float32)
    for d in range(2):
        dt = jnp.exp(log_dt[d])[:, None]
        mag = jnp.exp(lam_re[d] * dt)
        lb_re, lb_im = mag * jnp.cos(lam_im[d] * dt), mag * jnp.sin(lam_im[d] * dt)
        den = lam_re[d] ** 2 + lam_im[d] ** 2
        nr, ni = lb_re - 1.0, lb_im
        f_re = (nr * lam_re[d] + ni * lam_im[d]) / den
        f_im = (ni * lam_re[d] - nr * lam_im[d]) / den
        bb_re = f_re[..., None] * b_re - f_im[..., None] * b_im
        bb_im = f_re[..., None] * b_im + f_im[..., None] * b_re

        def blockdiag_in(m):
            return jnp.einsum('gph,gk->ghkp', m, eye).reshape(S5_WIDTH, S5_GP)

        def blockdiag_out(m):
            return jnp.einsum('ghp,gk->gpkh', m, eye).reshape(S5_GP, S5_WIDTH)

        bs.append(jnp.concatenate([blockdiag_in(bb_re), blockdiag_in(bb_im)], axis=1))
        cs.append(jnp.concatenate([blockdiag_out(c_re[d]), -blockdiag_out(c_im[d])], axis=0))
        lams.append(jnp.stack([lb_re.reshape(S5_GP), lb_im.reshape(S5_GP)]))
    return jnp.stack(lams), jnp.stack(bs), jnp.stack(cs)


def _s5_glu_kernel(u_ref, yf_ref, yb_ref, d_ref, w_ref, b_ref, o_ref):
    y = jax.nn.gelu(d_ref[...] * u_ref[...] + yf_ref[...] + yb_ref[...])
    o_ref[...] = y * jax.nn.sigmoid(_dotf(_bf(y), w_ref[...]) + b_ref[...])


def _s5_glu(u, yf, yb, d_skip, glu_w_bf16, glu_b, bsz, n_ctx):
    nt = u.shape[0]
    tn = _token_tile(n_ctx)
    tm = pl.BlockSpec((tn, S5_WIDTH), lambda b, i: (i, b))
    vec = pl.BlockSpec((1, S5_WIDTH), lambda b, i: (0, 0))
    return pl.pallas_call(
        _s5_glu_kernel,
        grid=(bsz, nt // tn),
        in_specs=[tm, tm, tm, vec, pl.BlockSpec((S5_WIDTH, S5_WIDTH), lambda b, i: (0, 0)), vec],
        out_specs=pl.BlockSpec((None, tn, S5_WIDTH), lambda b, i: (b, i, 0)),
        out_shape=jax.ShapeDtypeStruct((bsz, nt, S5_WIDTH), jnp.float32),
        compiler_params=_compiler_params(("parallel", "parallel")),
    )(u, yf, yb, d_skip.reshape(1, S5_WIDTH), glu_w_bf16, glu_b.reshape(1, S5_WIDTH))


def _na_offsets(rows):
    win_r = min(NA_WIN_R, rows)
    i = np.arange(rows)
    r0 = np.clip(i - win_r // 2, 0, rows - win_r)
    return r0 - i + NA_WIN_R - 1, win_r


def _na_bias_table(rpb, rows):
    off, win_r = _na_offsets(rows)
    offs = np.unique(off)
    qc = np.arange(GRID_W)[:, None]
    kc = np.arange(GRID_W)[None, :]
    c0 = np.clip(qc - NA_WIN_C // 2, 0, GRID_W - NA_WIN_C)
    valid = (kc >= c0) & (kc < c0 + NA_WIN_C)
    dc = np.clip(kc - qc, 1 - NA_WIN_C, NA_WIN_C - 1) + NA_WIN_C - 1
    tabs = []
    for o in offs:
        b = rpb[:, o + np.arange(win_r)][:, :, dc]
        b = jnp.where(valid[None, None], b, NEG_INF)
        tabs.append(b.transpose(0, 2, 1, 3).reshape(rpb.shape[0], GRID_W, win_r * GRID_W))
    return jnp.stack(tabs, axis=1), jnp.asarray(off - offs[0], jnp.int32)


_NT_DIMS = (((1,), (1,)), ((), ()))


def _softmax_pv(s_list, v_list):
    heads = range(len(s_list[0]))
    m = [functools.reduce(jnp.maximum, [jnp.max(s[h], axis=-1, keepdims=True) for s in s_list]) for h in heads]
    p = [[jnp.exp(s[h] - m[h]) for h in heads] for s in s_list]
    den = [sum(jnp.sum(pj[h], axis=-1, keepdims=True) for pj in p) for h in heads]
    o = [sum(_dotf(_bf(pj[h]), vj[h]) for pj, vj in zip(p, v_list)) for h in heads]
    return jnp.concatenate([o[h] / den[h] for h in heads], axis=-1)


def _na_kernel(off_ref, q_ref, k_ref, v_ref, bias_ref, o_ref, *, n_ctx, win_r, rows):
    j = pl.program_id(1)
    cb = n_ctx // GRID_W
    heads = range(q_ref.shape[0])
    scale = HEAD_DIM ** -0.5

    def scores(h, start, size):
        return lax.dot_general(q_ref[h], k_ref[h, pl.ds(start, size), :], _NT_DIMS,
                               preferred_element_type=jnp.float32) * scale

    @pl.when(j < cb)
    def _():
        s = [scores(h, 0, n_ctx) for h in heads]
        o_ref[...] = _softmax_pv([s], [[v_ref[h, pl.ds(0, n_ctx), :] for h in heads]])

    @pl.when(j >= cb)
    def _():
        i = j - cb
        r0 = jnp.clip(i - win_r // 2, 0, rows - win_r)
        start = pl.multiple_of(n_ctx + r0 * GRID_W, GRID_W)
        nk = win_r * GRID_W
        s_lat = [scores(h, start, nk) + bias_ref[h] for h in heads]
        s_ctx = [scores(h, 0, n_ctx) for h in heads]
        o_ref[...] = _softmax_pv([s_lat, s_ctx], [[v_ref[h, pl.ds(start, nk), :] for h in heads],
                                                  [v_ref[h, pl.ds(0, n_ctx), :] for h in heads]])


def _na_attention(qkv, bias, off_idx, n_ctx):
    bsz, _, H, nt, _ = qkv.shape
    rows = (nt - n_ctx) // GRID_W
    win_r = min(NA_WIN_R, rows)
    cb = n_ctx // GRID_W
    return pl.pallas_call(
        functools.partial(_na_kernel, n_ctx=n_ctx, win_r=win_r, rows=rows),
        grid_spec=pltpu.PrefetchScalarGridSpec(
            num_scalar_prefetch=1,
            grid=(bsz, cb + rows),
            in_specs=[pl.BlockSpec((None, None, H, GRID_W, HEAD_DIM), lambda b, j, off: (b, 0, 0, j, 0)),
                      pl.BlockSpec((None, None, H, nt, HEAD_DIM), lambda b, j, off: (b, 1, 0, 0, 0)),
                      pl.BlockSpec((None, None, H, nt, HEAD_DIM), lambda b, j, off: (b, 2, 0, 0, 0)),
                      pl.BlockSpec((H, None, GRID_W, win_r * GRID_W),
                                   lambda b, j, off: (0, off[jnp.maximum(j - cb, 0)], 0, 0))],
            out_specs=pl.BlockSpec((None, GRID_W, H * HEAD_DIM), lambda b, j, off: (b, j, 0)),
        ),
        out_shape=jax.ShapeDtypeStruct((bsz, nt, H * HEAD_DIM), jnp.float32),
        compiler_params=_compiler_params(("parallel", "arbitrary")),
    )(off_idx, qkv, qkv, qkv, bias)


def _head_sum_matrix():
    i = np.arange(RK_WIDTH)
    return jnp.asarray((i[:, None] // HEAD_DIM) == (i[None, :] // HEAD_DIM), jnp.bfloat16)


def _head_sums(x, ones):
    hi = x.astype(jnp.bfloat16)
    lo = (x - hi.astype(jnp.float32)).astype(jnp.bfloat16)
    return _dotf(hi, ones) + _dotf(lo, ones)


def _rkprep_kernel(z_ref, zp_ref, zn_ref, mu_ref, vec_ref, w2_ref, a2_ref, g2_ref, ones_ref,
                   com_ref, dir_ref, g_ref, bonus_ref, *, tn, n_ctx):
    i = pl.program_id(1)
    t0 = i * tn
    nt = pl.num_programs(1) * tn
    keep_prev = jnp.where((t0 == 0) | (t0 == n_ctx), 0.0, 1.0)
    keep_next = jnp.where((t0 + tn == n_ctx) | (t0 + tn == nt), 0.0, 1.0)
    z = z_ref[...]
    row = lax.broadcasted_iota(jnp.int32, z.shape, 0)
    prev = jnp.where(row == 0, zp_ref[7:8, :] * keep_prev, pltpu.roll(z, 1, axis=0))
    nxt = jnp.where(row == tn - 1, zn_ref[0:1, :] * keep_next, pltpu.roll(z, tn - 1, axis=0))
    zs = z + (0.5 * (prev + nxt) - z) * mu_ref[...]
    W, R = RK_WIDTH, RK_DECAY_RANK
    r, k, v = zs[:, :W], zs[:, W:2 * W], zs[:, 2 * W:3 * W]
    o = 3 * W
    zw = (zs[:, o:o + R], zs[:, o + R:o + 2 * R])
    za = (zs[:, o + 2 * R:o + 3 * R], zs[:, o + 3 * R:o + 4 * R])
    zg = zs[:, o + 4 * R:]
    k_k, k_a, r_k = vec_ref[0:1, :], vec_ref[1:2, :], vec_ref[2:3, :]
    ones = ones_ref[...]
    g_ref[...] = _dotf(_bf(jax.nn.sigmoid(zg)), g2_ref[...])
    kk = k * k_k
    kk = kk * lax.rsqrt(jnp.maximum(_head_sums(kk * kk, ones), 1e-24))
    bonus = 0.0
    fields = [r, kk, v]
    for d in range(2):
        w = -jax.nn.softplus(-(vec_ref[3 + d:4 + d, :] + _dotf(_bf(jnp.tanh(zw[d])), w2_ref[d]))) - 0.5
        a = jax.nn.sigmoid(vec_ref[5 + d:6 + d, :] + _dotf(_bf(za[d]), a2_ref[d]))
        kd = k * (1.0 + (a - 1.0) * k_a)
        bonus = bonus + _head_sums(r * kd * r_k, ones)
        fields += [-jnp.exp(w), kd, kk * a]
    bonus_ref[...] = bonus * v
    for j, t in enumerate(fields):
        for h in range(RK_HEADS):
            blk = t[:, h * HEAD_DIM:(h + 1) * HEAD_DIM]
            if j < 3:
                com_ref[j, h] = blk
            else:
                dir_ref[(j - 3) // 3, (j - 3) % 3, h] = blk


def _rkprep(z, mu, vec, w2, a2, g2, n_ctx):
    bsz, nt, zw = z.shape
    tn = _token_tile(n_ctx)
    tb = tn // 8
    nb = nt // 8
    ones = _head_sum_matrix()
    mu2 = mu.reshape(1, zw)

    def full(a):
        return pl.BlockSpec(a.shape, lambda b, i: (0,) * a.ndim)

    S = jax.ShapeDtypeStruct
    return pl.pallas_call(
        functools.partial(_rkprep_kernel, tn=tn, n_ctx=n_ctx),
        grid=(bsz, nt // tn),
        in_specs=[pl.BlockSpec((None, tn, zw), lambda b, i: (b, i, 0)),
                  pl.BlockSpec((None, 8, zw), lambda b, i: (b, jnp.maximum(i * tb - 1, 0), 0)),
                  pl.BlockSpec((None, 8, zw), lambda b, i: (b, jnp.minimum((i + 1) * tb, nb - 1), 0)),
                  full(mu2), full(vec), full(w2), full(a2), full(g2), full(ones)],
        out_specs=[pl.BlockSpec((None, 3, RK_HEADS, tn, HEAD_DIM), lambda b, i: (b, 0, 0, i, 0)),
                   pl.BlockSpec((None, 2, 3, RK_HEADS, tn, HEAD_DIM), lambda b, i: (b, 0, 0, 0, i, 0)),
                   pl.BlockSpec((None, tn, RK_WIDTH), lambda b, i: (b, i, 0)),
                   pl.BlockSpec((None, tn, RK_WIDTH), lambda b, i: (b, i, 0))],
        out_shape=[S((bsz, 3, RK_HEADS, nt, HEAD_DIM), jnp.float32),
                   S((bsz, 2, 3, RK_HEADS, nt, HEAD_DIM), jnp.float32),
                   S((bsz, nt, RK_WIDTH), jnp.float32), S((bsz, nt, RK_WIDTH), jnp.float32)],
        compiler_params=_compiler_params(("parallel", "parallel")),
    )(z, z, z, mu2, vec, w2, a2, g2, ones)


def _bdot(x, y):
    return _dotf(_bf(x), _bf(y))


def _bdot_nt(x, y):
    return lax.dot_general(_bf(x), _bf(y), _NT_DIMS, preferred_element_type=jnp.float32)


def _bdot_tn(x, y):
    return lax.dot_general(_bf(x), _bf(y), (((0,), (0,)), ((), ())), preferred_element_type=jnp.float32)


def _rwkv_units(units):
    T = units[0][0].shape[0]
    U = range(len(units))
    steps = int(math.log2(T))
    ti = lax.broadcasted_iota(jnp.int32, (T, T), 0)
    si = lax.broadcasted_iota(jnp.int32, (T, T), 1)
    ti2 = lax.broadcasted_iota(jnp.int32, (T, 2 * T), 0)
    si2 = lax.broadcasted_iota(jnp.int32, (T, 2 * T), 1) & (T - 1)
    eye = (lax.broadcasted_iota(jnp.int32, (HEAD_DIM, HEAD_DIM), 0)
           == lax.broadcasted_iota(jnp.int32, (HEAD_DIM, HEAD_DIM), 1))
    masks = {}
    for rev in (False, True):
        incl = (si >= ti) if rev else (si <= ti)
        masks[rev] = (incl.astype(jnp.bfloat16), (si2 >= ti2) if rev else (si2 <= ti2),
                      (si2 > ti2) if rev else (si2 < ti2))
    cum = []
    for (r, kk, v, lw, kd, b, h0, rev) in units:
        l1 = lw.astype(jnp.bfloat16)
        rem = lw - l1.astype(jnp.float32)
        l2 = rem.astype(jnp.bfloat16)
        l3 = (rem - l2.astype(jnp.float32)).astype(jnp.bfloat16)
        tri = masks[rev][0]
        cum.append(_dotf(tri, l1) + (_dotf(tri, l2) + _dotf(tri, l3)))
    at, rt, p, cl = [], [], [], []
    for u, (r, kk, v, lw, kd, b, h0, rev) in enumerate(units):
        c = cum[u]
        cl.append(c[0:1, :] if rev else c[T - 1:T, :])
        e_neg = jnp.exp(-c)
        at.append(-kk * jnp.exp(c - lw))
        rt.append(r * jnp.exp(c))
        p.append(_bdot_nt(jnp.concatenate([at[u], rt[u]], axis=0), jnp.concatenate([b * e_neg, kd * e_neg], axis=0)))
    top = [jnp.where(masks[units[u][7]][2], p[u][:T], 0.0) for u in U]
    l2m = [jnp.where(masks[units[u][7]][1], p[u][T:], 0.0) for u in U]
    npow = [top[u][:, :T] for u in U]
    x = [jnp.concatenate([at[u], _bdot(top[u][:, T:], units[u][2])], axis=1) for u in U]
    for i in range(steps):
        x = [x[u] + _bdot(npow[u], x[u]) for u in U]
        if i < steps - 1:
            npow = [_bdot(npow[u], npow[u]) for u in U]
    z = [jnp.concatenate([x[u], jnp.concatenate([jnp.zeros_like(units[u][2]), units[u][2]], axis=1)], axis=0)
         for u in U]
    ry = [_bdot(l2m[u], z[u]) for u in U]
    gj = []
    for u, (r, kk, v, lw, kd, b, h0, rev) in enumerate(units):
        e_end = jnp.exp(cl[u] - cum[u])
        gj.append(_bdot_tn(jnp.concatenate([b * e_end, kd * e_end], axis=0), z[u]))
    out = []
    for u in U:
        g = jnp.where(eye, jnp.exp(cl[u]), 0.0) + gj[u][:, :HEAD_DIM]
        yh = _bdot(jnp.concatenate([rt[u] + ry[u][:, :HEAD_DIM], g], axis=0), units[u][6])
        out.append((yh[:T] + ry[u][:, HEAD_DIM:], yh[T:] + gj[u][:, HEAD_DIM:]))
    return out


def _rwkv_kernel(cf_ref, cb_ref, df_ref, db_ref, yf_ref, yb_ref, hs):
    c = pl.program_id(1)
    H = cf_ref.shape[1]

    @pl.when(c == 0)
    def _():
        hs[...] = jnp.zeros_like(hs)

    units = []
    for d, (c_ref, d_ref) in enumerate(((cf_ref, df_ref), (cb_ref, db_ref))):
        for h in range(H):
            units.append((c_ref[0, h], c_ref[1, h], c_ref[2, h], d_ref[0, h], d_ref[1, h], d_ref[2, h],
                          hs[d, h], d == 1))
    res = _rwkv_units(units)
    for d, y_ref in enumerate((yf_ref, yb_ref)):
        for h in range(H):
            y, hn = res[d * H + h]
            y_ref[h] = y
            hs[d, h] = hn


def _rwkv_scan(com, dirs, n_ctx):
    bsz, _, H, nt, _ = com.shape
    T = RK_CHUNK
    bmap = functools.partial(_backward_chunk, nc=nt // T, ncc=n_ctx // T)
    cblk = (None, 3, H, T, HEAD_DIM)
    dblk = (None, None, 3, H, T, HEAD_DIM)
    oblk = (None, H, T, HEAD_DIM)
    return pl.pallas_call(
        _rwkv_kernel,
        grid=(bsz, nt // T),
        in_specs=[pl.BlockSpec(cblk, lambda b, c: (b, 0, 0, c, 0)),
                  pl.BlockSpec(cblk, lambda b, c: (b, 0, 0, bmap(c), 0)),
                  pl.BlockSpec(dblk, lambda b, c: (b, 0, 0, 0, c, 0)),
                  pl.BlockSpec(dblk, lambda b, c: (b, 1, 0, 0, bmap(c), 0))],
        out_specs=[pl.BlockSpec(oblk, lambda b, c: (b, 0, c, 0)),
                   pl.BlockSpec(oblk, lambda b, c: (b, 0, bmap(c), 0))],
        out_shape=[jax.ShapeDtypeStruct((bsz, H, nt, HEAD_DIM), jnp.float32)] * 2,
        scratch_shapes=[pltpu.VMEM((2, H, HEAD_DIM, HEAD_DIM), jnp.float32)],
        compiler_params=_compiler_params(("parallel", "arbitrary")),
    )(com, com, dirs, dirs)


def _rkfin_kernel(yf_ref, yb_ref, g_ref, bonus_ref, ln_ref, o_ref):
    outs = []
    for h in range(RK_HEADS):
        y = yf_ref[h] + yb_ref[h]
        yc = y - jnp.mean(y, axis=-1, keepdims=True)
        var = jnp.mean(yc * yc, axis=-1, keepdims=True)
        outs.append(yc * lax.rsqrt(var + RK_GN_EPS))
    yn = jnp.concatenate(outs, axis=-1)
    o_ref[...] = (yn * ln_ref[0:1, :] + ln_ref[1:2, :] + bonus_ref[...]) * g_ref[...]


def _rkfin(yf, yb, g, bonus, ln, n_ctx):
    bsz, H, nt, _ = yf.shape
    tn = _token_tile(n_ctx)
    yblk = pl.BlockSpec((None, H, tn, HEAD_DIM), lambda b, i: (b, 0, i, 0))
    tblk = pl.BlockSpec((None, tn, RK_WIDTH), lambda b, i: (b, i, 0))
    return pl.pallas_call(
        _rkfin_kernel,
        grid=(bsz, nt // tn),
        in_specs=[yblk, yblk, tblk, tblk, pl.BlockSpec(ln.shape, lambda b, i: (0, 0))],
        out_specs=tblk,
        out_shape=jax.ShapeDtypeStruct((bsz, nt, RK_WIDTH), jnp.float32),
        compiler_params=_compiler_params(("parallel", "parallel")),
    )(yf, yb, g, bonus, ln)


def _outproj_kernel(s5_ref, na_ref, rk_ref, w_ref, g_ref, gt_ref, x_ref, o_ref):
    z = (_dotf(_bf(s5_ref[...]), w_ref[:S5_WIDTH, :])
         + _dotf(_bf(na_ref[...]), w_ref[S5_WIDTH:S5_WIDTH + NA_WIDTH, :])
         + _dotf(_bf(rk_ref[...]), w_ref[S5_WIDTH + NA_WIDTH:, :]))
    zn = z * lax.rsqrt(jnp.mean(z * z, axis=-1, keepdims=True) + RMS_EPS)
    o_ref[...] = x_ref[...] + gt_ref[...] * (zn * g_ref[...])


def _outproj(y_s5, y_na, y_rk, w_bf16, g, gates, xc, n_ctx):
    bsz, nt, d = xc.shape
    tn = _token_tile(n_ctx)
    cb = n_ctx // tn

    def tok(w):
        return pl.BlockSpec((None, tn, w), lambda b, i: (b, i, 0))

    return pl.pallas_call(
        _outproj_kernel,
        grid=(bsz, nt // tn),
        in_specs=[tok(S5_WIDTH), tok(NA_WIDTH), tok(RK_WIDTH),
                  pl.BlockSpec((D_MIX, d), lambda b, i: (0, 0)),
                  pl.BlockSpec((1, d), lambda b, i: (0, 0)),
                  pl.BlockSpec((None, None, None, 1, d), lambda b, i: (b, jnp.minimum(i // cb, 1), 0, 0, 0)),
                  tok(d)],
        out_specs=tok(d),
        out_shape=jax.ShapeDtypeStruct(xc.shape, jnp.float32),
        compiler_params=_compiler_params(("parallel", "parallel")),
    )(y_s5, y_na, y_rk, w_bf16, g.reshape(1, d), gates, xc)


def _router_kernel(x_ref, g_ref, sc_ref, sh_ref, wr_ref, h_ref, aff_ref):
    x = x_ref[...]
    y = x * lax.rsqrt(jnp.mean(x * x, axis=-1, keepdims=True) + RMS_EPS)
    h = _bf((y * g_ref[...]) * (1.0 + sc_ref[...]) + sh_ref[...])
    h_ref[...] = h
    logits = _dotf(h, wr_ref[...])
    e = jnp.exp(logits - jnp.max(logits, axis=-1, keepdims=True))
    aff_ref[...] = e / jnp.sum(e, axis=-1, keepdims=True)


def _router(xc, g, mods, wr_bf16, n_ctx):
    bsz, nt, d = xc.shape
    tn = _token_tile(n_ctx)
    ne = wr_bf16.shape[1]
    return pl.pallas_call(
        _router_kernel,
        grid=(bsz, nt // tn),
        in_specs=[pl.BlockSpec((None, tn, d), lambda b, i: (b, i, 0)),
                  pl.BlockSpec((1, d), lambda b, i: (0, 0)),
                  *_mod_specs(bsz, d, n_ctx),
                  pl.BlockSpec((d, ne), lambda b, i: (0, 0))],
        out_specs=[pl.BlockSpec((None, tn, d), lambda b, i: (b, i, 0)),
                   pl.BlockSpec((None, tn, ne), lambda b, i: (b, i, 0))],
        out_shape=[jax.ShapeDtypeStruct((bsz, nt, d), jnp.bfloat16),
                   jax.ShapeDtypeStruct((bsz, nt, ne), jnp.float32)],
        compiler_params=_compiler_params(("parallel", "parallel")),
    )(xc, g.reshape(1, d), mods, mods, wr_bf16)


def _expert_kernel(idx_ref, gate_ref, h_ref, wg_ref, wu_ref, wd_ref, y_ref, *, G, cap, n, t0):
    tok = lax.broadcasted_iota(jnp.int32, (cap, n), 1)
    xin = _bf(jnp.concatenate(
        [_dotf(_bf(idx_ref[s] == tok), h_ref[s, pl.ds(t0, n), :]) for s in range(G)], axis=0))
    acc = jnp.zeros((G * cap, wd_ref.shape[1]), jnp.float32)
    for c in range(wg_ref.shape[1] // FFN_FCHUNK):
        cols = slice(c * FFN_FCHUNK, (c + 1) * FFN_FCHUNK)
        hid = jax.nn.silu(_dotf(xin, wg_ref[:, cols])) * _dotf(xin, wu_ref[:, cols])
        acc = acc + _dotf(_bf(hid), wd_ref[cols, :])
    gate = jnp.concatenate([gate_ref[s] for s in range(G)], axis=0)
    y_ref[...] = _bf(acc * gate)


def _experts(idx, gate, h2, wg, wu, wd, t0, n):
    bsz, ne, cap = idx.shape
    nt, d = h2.shape[1], h2.shape[2]
    f = wg.shape[2]
    G = max(1, min(bsz, FFN_ROWS // cap))
    if t0 % n == 0:
        hspec = pl.BlockSpec((G, n, d), lambda e, b: (b, t0 // n, 0))
        t_in = 0
    else:
        hspec = pl.BlockSpec((G, nt, d), lambda e, b: (b, 0, 0))
        t_in = t0

    def wspec(shape):
        return pl.BlockSpec((None,) + shape, lambda e, b: (e, 0, 0), pipeline_mode=pl.Buffered(1))

    return pl.pallas_call(
        functools.partial(_expert_kernel, G=G, cap=cap, n=n, t0=t_in),
        grid=(ne, bsz // G),
        in_specs=[pl.BlockSpec((G, None, cap, 1), lambda e, b: (b, e, 0, 0)),
                  pl.BlockSpec((G, None, cap, 1), lambda e, b: (b, e, 0, 0)),
                  hspec, wspec((d, f)), wspec((d, f)), wspec((f, d))],
        out_specs=pl.BlockSpec((None, None, G * cap, d), lambda e, b: (b, e, 0, 0)),
        out_shape=jax.ShapeDtypeStruct((bsz // G, ne, G * cap, d), jnp.bfloat16),
        compiler_params=_compiler_params(("arbitrary", "arbitrary")),
    )(idx.reshape(bsz, ne, cap, 1), gate.reshape(bsz, ne, cap, 1), h2, wg, wu, wd)


def _combine_kernel(idx_ref, y_ref, x_ref, g_ref, gt_ref, o_ref, *, tn, t_lo, t_hi):
    i = pl.program_id(1)
    inside = (i >= t_lo) & (i < t_hi)

    @pl.when(inside)
    def _():
        tok = lax.broadcasted_iota(jnp.int32, (tn, idx_ref.shape[1]), 0) + (i - t_lo) * tn
        f = _dotf(_bf(idx_ref[...] == tok), y_ref[...])
        fn = f * lax.rsqrt(jnp.mean(f * f, axis=-1, keepdims=True) + RMS_EPS)
        o_ref[...] = x_ref[...] + gt_ref[...] * (fn * g_ref[...])

    @pl.when(jnp.logical_not(inside))
    def _():
        o_ref[...] = x_ref[...]


def _combine(idx, y, xc, g, gt, t0, n, n_ctx):
    bsz, ne, cap = idx.shape
    nt, d = xc.shape[1], xc.shape[2]
    G = bsz // y.shape[0]
    yb = y.reshape(bsz // G, ne, G, cap, d).transpose(0, 2, 1, 3, 4).reshape(bsz, ne * cap, d)
    tn = _token_tile(n_ctx)
    S = ne * cap
    return pl.pallas_call(
        functools.partial(_combine_kernel, tn=tn, t_lo=t0 // tn, t_hi=(t0 + n) // tn),
        grid=(bsz, nt // tn),
        in_specs=[pl.BlockSpec((None, 1, S), lambda b, i: (b, 0, 0)),
                  pl.BlockSpec((None, S, d), lambda b, i: (b, 0, 0)),
                  pl.BlockSpec((None, tn, d), lambda b, i: (b, i, 0)),
                  pl.BlockSpec((1, d), lambda b, i: (0, 0)),
                  pl.BlockSpec((None, 1, d), lambda b, i: (b, 0, 0))],
        out_specs=pl.BlockSpec((None, tn, d), lambda b, i: (b, i, 0)),
        out_shape=jax.ShapeDtypeStruct(xc.shape, jnp.float32),
        compiler_params=_compiler_params(("parallel", "arbitrary")),
    )(idx.reshape(bsz, 1, S), yb, xc, g.reshape(1, d), gt.reshape(bsz, 1, d))


def _expert_choice(xc, h2, aff, wg, wu, wd, g, gt, t0, n, n_ctx):
    cap = EC_CAPACITY_FACTOR * n // N_EXPERTS
    gate, idx = lax.top_k(jnp.swapaxes(aff[:, t0:t0 + n], 1, 2), cap)
    y = _experts(idx, gate, h2, wg, wu, wd, t0, n)
    return _combine(idx, y, xc, g, gt, t0, n, n_ctx)


def kernel(x, c, ctx, c_ctx, ada_w, ada_b, norm_g, w_in, w_out, s5_lam_re, s5_lam_im, s5_log_dt, s5_b_re, s5_b_im, s5_c_re, s5_c_im, s5_d, s5_glu_w, s5_glu_b, na_rpb, rk_mu, rk_w0, rk_w2, rk_a0, rk_a2, rk_g2, rk_k_k, rk_k_a, rk_r_k, rk_ln_w, rk_ln_b, ec_router, ec_w_gate, ec_w_up, ec_w_down):
    bsz, n, d = x.shape
    n_ctx = ctx.shape[1]
    rows = n // GRID_W
    xc = jnp.concatenate([ctx, x], axis=1)
    cos, sin = _rope_tables(n_ctx, n)

    for l in range(DEPTH):
        need_ctx = l < DEPTH - 1
        mod = jax.nn.silu(c) @ ada_w[l] + ada_b[l]
        mod_c = jnp.broadcast_to(jax.nn.silu(c_ctx) @ ada_w[l] + ada_b[l], mod.shape)
        both = jnp.stack([mod_c, mod], axis=1).reshape(bsz, 2, 6, 1, d)
        mods1 = jnp.stack([both[:, :, 1], both[:, :, 0]], axis=2)
        mods2 = jnp.stack([both[:, :, 4], both[:, :, 3]], axis=2)
        gates1 = both[:, :, 2:3]

        u, qkv, z = _inproj(xc, norm_g[l, 0], mods1, _bf(w_in[l]), cos, sin, n_ctx)

        lam, bblk, cblk = _s5_params(s5_lam_re[l], s5_lam_im[l], s5_log_dt[l], s5_b_re[l], s5_b_im[l],
                                     s5_c_re[l], s5_c_im[l])
        yf, yb = _s5_scan(u.reshape(-1, S5_WIDTH), lam, bblk, cblk, bsz, n_ctx)
        y_s5 = _s5_glu(u, yf.reshape(u.shape), yb.reshape(u.shape), s5_d[l], _bf(s5_glu_w[l]), s5_glu_b[l],
                       bsz, n_ctx)

        bias, off_idx = _na_bias_table(na_rpb[l], rows)
        y_na = _na_attention(qkv, bias, off_idx, n_ctx)

        vec = jnp.stack([rk_k_k[l], rk_k_a[l], rk_r_k[l].reshape(-1), rk_w0[l, 0], rk_w0[l, 1],
                         rk_a0[l, 0], rk_a0[l, 1], jnp.zeros_like(rk_k_k[l])])
        com, dirs, g_rk, bonus = _rkprep(z, rk_mu[l], vec, _bf(rk_w2[l]), _bf(rk_a2[l]), _bf(rk_g2[l]), n_ctx)
        yf_rk, yb_rk = _rwkv_scan(com, dirs, n_ctx)
        y_rk = _rkfin(yf_rk, yb_rk, g_rk, bonus, jnp.stack([rk_ln_w[l], rk_ln_b[l]]), n_ctx)

        xc = _outproj(y_s5, y_na, y_rk, _bf(w_out[l]), norm_g[l, 1], gates1, xc, n_ctx)

        h2, aff = _router(xc, norm_g[l, 2], mods2, _bf(ec_router[l]), n_ctx)
        wg, wu, wd = _bf(ec_w_gate[l]), _bf(ec_w_up[l]), _bf(ec_w_down[l])
        x_new = _expert_choice(xc, h2, aff, wg, wu, wd, norm_g[l, 3], both[:, 1, 5, 0], n_ctx, n, n_ctx)
        if need_ctx:
            x_new = _expert_choice(x_new, h2, aff, wg, wu, wd, norm_g[l, 3], both[:, 0, 5, 0], 0, n_ctx, n_ctx)
        xc = x_new
    return xc[:, n_ctx:]
```

```python
import functools
import math

import jax
import jax.numpy as jnp
import numpy as np
from jax import lax
from jax.experimental import pallas as pl
from jax.experimental.pallas import tpu as pltpu

D_MODEL = 1024
DEPTH = 2
GRID_W = 64
D_MIX = D_MODEL
HEAD_DIM = 64
S5_WIDTH = D_MIX // 4
S5_GROUP = 16
S5_GROUPS = S5_WIDTH // S5_GROUP
S5_STATE = 64
NA_WIDTH = (D_MIX - S5_WIDTH) // 2
NA_HEADS = NA_WIDTH // HEAD_DIM
NA_WIN_R = 8
NA_WIN_C = 16
ROPE_BASE = 10000.0
RK_WIDTH = D_MIX - S5_WIDTH - NA_WIDTH
RK_HEADS = RK_WIDTH // HEAD_DIM
RK_DECAY_RANK = 64
RK_A_RANK = 64
RK_GATE_RANK = 128
RK_IN_WIDTH = 3 * RK_WIDTH + 2 * RK_DECAY_RANK + 2 * RK_A_RANK + RK_GATE_RANK
RK_GN_EPS = 64e-5
N_IN = S5_WIDTH + 3 * NA_WIDTH + RK_IN_WIDTH
N_EXPERTS = 16
EC_CAPACITY_FACTOR = 2
RMS_EPS = 1e-6
NEG_INF = -1e30

VMEM_LIMIT_BYTES = 56 * 1024 * 1024
TOKEN_TILE = 256
S5_GP = S5_GROUPS * S5_STATE
S5_CHUNK = 64
S5_LANES = 512
RK_CHUNK = 64
FFN_ROWS = 512
FFN_FCHUNK = 704


def _compiler_params(semantics):
    return pltpu.CompilerParams(dimension_semantics=semantics, vmem_limit_bytes=VMEM_LIMIT_BYTES)


def _token_tile(n_ctx):
    return min(TOKEN_TILE, n_ctx)


def _mod_specs(bsz, d, n_ctx):
    cb = n_ctx // _token_tile(n_ctx)
    return [pl.BlockSpec((None, None, None, 1, d), lambda b, i, j=j: (b, jnp.minimum(i // cb, 1), j, 0, 0))
            for j in range(2)]


def _dotf(x, y):
    return jnp.dot(x, y, preferred_element_type=jnp.float32)


def _bf(x):
    return x.astype(jnp.bfloat16)


def _rope_tables(n_ctx, n):
    t = np.arange(n)
    nf = HEAD_DIM // 4
    inv_freq = ROPE_BASE ** (-np.arange(nf, dtype=np.float32) / nf)
    pos = np.stack([(t // GRID_W).astype(np.float32), (t % GRID_W).astype(np.float32)], axis=1)
    ang = pos[:, :, None] * inv_freq[None, None, :]
    cos = np.repeat(np.cos(ang), 2, axis=1).reshape(n, HEAD_DIM)
    sin = np.sin(ang)
    sin = np.stack([-sin[:, 0], sin[:, 0], -sin[:, 1], sin[:, 1]], axis=1).reshape(n, HEAD_DIM)
    cos = np.concatenate([np.ones((n_ctx, HEAD_DIM), np.float32), cos.astype(np.float32)], axis=0)
    sin = np.concatenate([np.zeros((n_ctx, HEAD_DIM), np.float32), sin.astype(np.float32)], axis=0)
    reps = 2 * NA_HEADS
    return jnp.asarray(np.tile(cos, (1, reps))), jnp.asarray(np.tile(sin, (1, reps)))


def _inproj_kernel(x_ref, g_ref, sc_ref, sh_ref, w_ref, cos_ref, sin_ref, u_ref, qkv_ref, z_ref):
    x = x_ref[...]
    y = x * lax.rsqrt(jnp.mean(x * x, axis=-1, keepdims=True) + RMS_EPS)
    h = _bf((y * g_ref[...]) * (1.0 + sc_ref[...]) + sh_ref[...])
    u_ref[...] = _dotf(h, w_ref[:, :S5_WIDTH])
    z_ref[...] = _dotf(h, w_ref[:, S5_WIDTH + 3 * NA_WIDTH:])
    qk = _dotf(h, w_ref[:, S5_WIDTH:S5_WIDTH + 2 * NA_WIDTH])
    nf = HEAD_DIM // 4
    lane = lax.broadcasted_iota(jnp.int32, qk.shape, 1)
    first = (lane & (2 * nf - 1)) < nf
    partner = jnp.where(first, pltpu.roll(qk, qk.shape[1] - nf, axis=1), pltpu.roll(qk, nf, axis=1))
    qk = _bf(qk * cos_ref[...] + partner * sin_ref[...])
    v = _bf(_dotf(h, w_ref[:, S5_WIDTH + 2 * NA_WIDTH:S5_WIDTH + 3 * NA_WIDTH]))
    for hd in range(NA_HEADS):
        lo = hd * HEAD_DIM
        qkv_ref[0, hd] = qk[:, lo:lo + HEAD_DIM]
        qkv_ref[1, hd] = qk[:, NA_WIDTH + lo:NA_WIDTH + lo + HEAD_DIM]
        qkv_ref[2, hd] = v[:, lo:lo + HEAD_DIM]


def _inproj(xc, g, mods, w_bf16, cos, sin, n_ctx):
    bsz, nt, d = xc.shape
    tn = _token_tile(n_ctx)
    return pl.pallas_call(
        _inproj_kernel,
        grid=(bsz, nt // tn),
        in_specs=[pl.BlockSpec((None, tn, d), lambda b, i: (b, i, 0)),
                  pl.BlockSpec((1, d), lambda b, i: (0, 0)),
                  *_mod_specs(bsz, d, n_ctx),
                  pl.BlockSpec((d, N_IN), lambda b, i: (0, 0)),
                  pl.BlockSpec((tn, 2 * NA_WIDTH), lambda b, i: (i, 0)),
                  pl.BlockSpec((tn, 2 * NA_WIDTH), lambda b, i: (i, 0))],
        out_specs=[pl.BlockSpec((tn, S5_WIDTH), lambda b, i: (i, b)),
                   pl.BlockSpec((None, 3, NA_HEADS, tn, HEAD_DIM), lambda b, i: (b, 0, 0, i, 0)),
                   pl.BlockSpec((None, tn, RK_IN_WIDTH), lambda b, i: (b, i, 0))],
        out_shape=[jax.ShapeDtypeStruct((nt, bsz * S5_WIDTH), jnp.float32),
                   jax.ShapeDtypeStruct((bsz, 3, NA_HEADS, nt, HEAD_DIM), jnp.bfloat16),
                   jax.ShapeDtypeStruct((bsz, nt, RK_IN_WIDTH), jnp.float32)],
        compiler_params=_compiler_params(("parallel", "parallel")),
    )(xc, g.reshape(1, d), mods, mods, w_bf16, cos, sin)


def _s5_kernel(uf_ref, ub_ref, lam_ref, bw_ref, cw_ref, yf_ref, yb_ref, sbuf, st, *, T, B):
    c = pl.program_id(0)

    @pl.when(c == 0)
    def _():
        st[...] = jnp.zeros_like(st)

    for d, u_ref in enumerate((uf_ref, ub_ref)):
        sbuf[d] = _dotf(_bf(u_ref[...]), bw_ref[d])
    for d in range(2):
        for h in range(S5_GP // S5_LANES):
            re_sl = pl.ds(h * S5_LANES, S5_LANES)
            im_sl = pl.ds(S5_GP + h * S5_LANES, S5_LANES)
            lr = jnp.broadcast_to(lam_ref[d, 0:1, h * S5_LANES:(h + 1) * S5_LANES], (B, S5_LANES))
            li = jnp.broadcast_to(lam_ref[d, 1:2, h * S5_LANES:(h + 1) * S5_LANES], (B, S5_LANES))
            s_re, s_im = st[d, :, re_sl], st[d, :, im_sl]
            for i in range(T):
                t = i if d == 0 else T - 1 - i
                rows = pl.ds(t * B, B)
                s_re, s_im = (lr * s_re - li * s_im + sbuf[d, rows, re_sl],
                              lr * s_im + li * s_re + sbuf[d, rows, im_sl])
                sbuf[d, rows, re_sl] = s_re
                sbuf[d, rows, im_sl] = s_im
            st[d, :, re_sl] = s_re
            st[d, :, im_sl] = s_im
    for d, y_ref in enumerate((yf_ref, yb_ref)):
        y_ref[...] = _dotf(_bf(sbuf[d]), cw_ref[d])


def _backward_chunk(c, nc, ncc):
    return jnp.where(c < ncc, ncc - 1 - c, nc + ncc - 1 - c)


def _s5_scan(uf, lam, bblk, cblk, bsz, n_ctx):
    T = S5_CHUNK
    nc = uf.shape[0] // (T * bsz)
    ncc = n_ctx // T
    bw, cw = _bf(bblk), _bf(cblk)
    bmap = functools.partial(_backward_chunk, nc=nc, ncc=ncc)
    blk = (T * bsz, S5_WIDTH)

    def full(shape):
        return pl.BlockSpec(shape, lambda c: (0,) * len(shape))

    return pl.pallas_call(
        functools.partial(_s5_kernel, T=T, B=bsz),
        grid=(nc,),
        in_specs=[pl.BlockSpec(blk, lambda c: (c, 0)), pl.BlockSpec(blk, lambda c: (bmap(c), 0)),
                  full(lam.shape), full(bw.shape), full(cw.shape)],
        out_specs=[pl.BlockSpec(blk, lambda c: (c, 0)), pl.BlockSpec(blk, lambda c: (bmap(c), 0))],
        out_shape=[jax.ShapeDtypeStruct(uf.shape, jnp.float32)] * 2,
        scratch_shapes=[pltpu.VMEM((2, T * bsz, 2 * S5_GP), jnp.float32),
                        pltpu.VMEM((2, bsz, 2 * S5_GP), jnp.float32)],
        compiler_params=_compiler_params(("arbitrary",)),
    )(uf, uf, lam, bw, cw)


def _s5_params(lam_re, lam_im, log_dt, b_re, b_im, c_re, c_im):
    lams, bs, cs = [], [], []
    eye = jnp.eye(S5_GROUPS, dtype=jnp.float32)
    for d in range(2):
        dt = jnp.exp(log_dt[d])[:, None]
        mag = jnp.exp(lam_re[d] * dt)
        lb_re, lb_im = mag * jnp.cos(lam_im[d] * dt), mag * jnp.sin(lam_im[d] * dt)
        den = lam_re[d] ** 2 + lam_im[d] ** 2
        nr, ni = lb_re - 1.0, lb_im
        f_re = (nr * lam_re[d] + ni * lam_im[d]) / den
        f_im = (ni * lam_re[d] - nr * lam_im[d]) / den
        bb_re = f_re[..., None] * b_re - f_im[..., None] * b_im
        bb_im = f_re[..., None] * b_im + f_im[..., None] * b_re

        def blockdiag_in(m):
            return jnp.einsum('gph,gk->ghkp', m, eye).reshape(S5_WIDTH, S5_GP)

        def blockdiag_out(m):
            return jnp.einsum('ghp,gk->gpkh', m, eye).reshape(S5_GP, S5_WIDTH)

        bs.append(jnp.concatenate([blockdiag_in(bb_re), blockdiag_in(bb_im)], axis=1))
        cs.append(jnp.concatenate([blockdiag_out(c_re[d]), -blockdiag_out(c_im[d])], axis=0))
        lams.append(jnp.stack([lb_re.reshape(S5_GP), lb_im.reshape(S5_GP)]))
    return jnp.stack(lams), jnp.stack(bs), jnp.stack(cs)


def _na_offsets(rows):
    win_r = min(NA_WIN_R, rows)
    i = np.arange(rows)
    r0 = np.clip(i - win_r // 2, 0, rows - win_r)
    return r0 - i + NA_WIN_R - 1, win_r


def _na_bias_table(rpb, rows):
    off, win_r = _na_offsets(rows)
    offs = np.unique(off)
    qc = np.arange(GRID_W)[:, None]
    kc = np.arange(GRID_W)[None, :]
    c0 = np.clip(qc - NA_WIN_C // 2, 0, GRID_W - NA_WIN_C)
    valid = (kc >= c0) & (kc < c0 + NA_WIN_C)
    dc = np.clip(kc - qc, 1 - NA_WIN_C, NA_WIN_C - 1) + NA_WIN_C - 1
    full = jnp.where(valid[None, :, None, :], rpb[:, :, dc].transpose(0, 2, 1, 3), NEG_INF)
    full = full.reshape(rpb.shape[0], GRID_W, -1)
    tabs = [full[:, :, o * GRID_W:(o + win_r) * GRID_W] for o in offs]
    return jnp.stack(tabs, axis=1), jnp.asarray(off - offs[0], jnp.int32)


_NT_DIMS = (((1,), (1,)), ((), ()))


def _softmax_pv(s_list, v_list):
    heads = range(len(s_list[0]))
    m = [functools.reduce(jnp.maximum, [jnp.max(s[h], axis=-1, keepdims=True) for s in s_list]) for h in heads]
    p = [[jnp.exp(s[h] - m[h]) for h in heads] for s in s_list]
    den = [sum(jnp.sum(pj[h], axis=-1, keepdims=True) for pj in p) for h in heads]
    o = [sum(_dotf(_bf(pj[h]), vj[h]) for pj, vj in zip(p, v_list)) for h in heads]
    return jnp.concatenate([o[h] / den[h] for h in heads], axis=-1)


def _na_kernel(off_ref, q_ref, k_ref, v_ref, bias_ref, o_ref, *, n_ctx, win_r, rows):
    j = pl.program_id(1)
    cb = n_ctx // GRID_W
    heads = range(q_ref.shape[0])
    scale = HEAD_DIM ** -0.5

    def scores(h, start, size):
        return lax.dot_general(q_ref[h], k_ref[h, pl.ds(start, size), :], _NT_DIMS,
                               preferred_element_type=jnp.float32) * scale

    @pl.when(j < cb)
    def _():
        s = [scores(h, 0, n_ctx) for h in heads]
        o_ref[...] = _softmax_pv([s], [[v_ref[h, pl.ds(0, n_ctx), :] for h in heads]])

    @pl.when(j >= cb)
    def _():
        i = j - cb
        r0 = jnp.clip(i - win_r // 2, 0, rows - win_r)
        start = pl.multiple_of(n_ctx + r0 * GRID_W, GRID_W)
        nk = win_r * GRID_W
        s_lat = [scores(h, start, nk) + bias_ref[h] for h in heads]
        s_ctx = [scores(h, 0, n_ctx) for h in heads]
        o_ref[...] = _softmax_pv([s_lat, s_ctx], [[v_ref[h, pl.ds(start, nk), :] for h in heads],
                                                  [v_ref[h, pl.ds(0, n_ctx), :] for h in heads]])


def _na_attention(qkv, bias, off_idx, n_ctx):
    bsz, _, H, nt, _ = qkv.shape
    rows = (nt - n_ctx) // GRID_W
    win_r = min(NA_WIN_R, rows)
    cb = n_ctx // GRID_W
    return pl.pallas_call(
        functools.partial(_na_kernel, n_ctx=n_ctx, win_r=win_r, rows=rows),
        grid_spec=pltpu.PrefetchScalarGridSpec(
            num_scalar_prefetch=1,
            grid=(bsz, cb + rows),
            in_specs=[pl.BlockSpec((None, None, H, GRID_W, HEAD_DIM), lambda b, j, off: (b, 0, 0, j, 0)),
                      pl.BlockSpec((None, None, H, nt, HEAD_DIM), lambda b, j, off: (b, 1, 0, 0, 0)),
                      pl.BlockSpec((None, None, H, nt, HEAD_DIM), lambda b, j, off: (b, 2, 0, 0, 0)),
                      pl.BlockSpec((H, None, GRID_W, win_r * GRID_W),
                                   lambda b, j, off: (0, off[jnp.maximum(j - cb, 0)], 0, 0))],
            out_specs=pl.BlockSpec((None, GRID_W, H * HEAD_DIM), lambda b, j, off: (b, j, 0)),
        ),
        out_shape=jax.ShapeDtypeStruct((bsz, nt, H * HEAD_DIM), jnp.float32),
        compiler_params=_compiler_params(("parallel", "arbitrary")),
    )(off_idx, qkv, qkv, qkv, bias)


def _head_sum_matrix():
    i = np.arange(RK_WIDTH)
    return jnp.asarray((i[:, None] // HEAD_DIM) == (i[None, :] // HEAD_DIM), jnp.bfloat16)


def _head_sums(x, ones):
    hi = x.astype(jnp.bfloat16)
    lo = (x - hi.astype(jnp.float32)).astype(jnp.bfloat16)
    return _dotf(hi, ones) + _dotf(lo, ones)


def _rkprep_kernel(z_ref, zp_ref, zn_ref, mu_ref, vec_ref, w2_ref, a2_ref, g2_ref, ones_ref,
                   com_ref, dir_ref, g_ref, bonus_ref, *, tn, n_ctx):
    i = pl.program_id(1)
    t0 = i * tn
    nt = pl.num_programs(1) * tn
    keep_prev = jnp.where((t0 == 0) | (t0 == n_ctx), 0.0, 1.0)
    keep_next = jnp.where((t0 + tn == n_ctx) | (t0 + tn == nt), 0.0, 1.0)
    z = z_ref[...]
    row = lax.broadcasted_iota(jnp.int32, z.shape, 0)
    prev = jnp.where(row == 0, zp_ref[7:8, :] * keep_prev, pltpu.roll(z, 1, axis=0))
    nxt = jnp.where(row == tn - 1, zn_ref[0:1, :] * keep_next, pltpu.roll(z, tn - 1, axis=0))
    zs = z + (0.5 * (prev + nxt) - z) * mu_ref[...]
    W, R = RK_WIDTH, RK_DECAY_RANK
    r, k, v = zs[:, :W], zs[:, W:2 * W], zs[:, 2 * W:3 * W]
    o = 3 * W
    zw = (zs[:, o:o + R], zs[:, o + R:o + 2 * R])
    za = (zs[:, o + 2 * R:o + 3 * R], zs[:, o + 3 * R:o + 4 * R])
    zg = zs[:, o + 4 * R:]
    k_k, k_a, r_k = vec_ref[0:1, :], vec_ref[1:2, :], vec_ref[2:3, :]
    ones = ones_ref[...]
    g_ref[...] = _dotf(_bf(jax.nn.sigmoid(zg)), g2_ref[...])
    kk = k * k_k
    kk = kk * lax.rsqrt(jnp.maximum(_head_sums(kk * kk, ones), 1e-24))
    bonus = 0.0
    fields = [r, kk, v]
    for d in range(2):
        w = -jax.nn.softplus(-(vec_ref[3 + d:4 + d, :] + _dotf(_bf(jnp.tanh(zw[d])), w2_ref[d]))) - 0.5
        a = jax.nn.sigmoid(vec_ref[5 + d:6 + d, :] + _dotf(_bf(za[d]), a2_ref[d]))
        kd = k * (1.0 + (a - 1.0) * k_a)
        bonus = bonus + _head_sums(r * kd * r_k, ones)
        fields += [-jnp.exp(w), kd, kk * a]
    bonus_ref[...] = bonus * v
    for j, t in enumerate(fields):
        for h in range(RK_HEADS):
            blk = t[:, h * HEAD_DIM:(h + 1) * HEAD_DIM]
            if j < 3:
                com_ref[j, h] = blk
            else:
                dir_ref[(j - 3) // 3, (j - 3) % 3, h] = blk


def _rkprep(z, mu, vec, w2, a2, g2, n_ctx):
    bsz, nt, zw = z.shape
    tn = _token_tile(n_ctx)
    tb = tn // 8
    nb = nt // 8
    ones = _head_sum_matrix()
    mu2 = mu.reshape(1, zw)

    def full(a):
        return pl.BlockSpec(a.shape, lambda b, i: (0,) * a.ndim)

    S = jax.ShapeDtypeStruct
    return pl.pallas_call(
        functools.partial(_rkprep_kernel, tn=tn, n_ctx=n_ctx),
        grid=(bsz, nt // tn),
        in_specs=[pl.BlockSpec((None, tn, zw), lambda b, i: (b, i, 0)),
                  pl.BlockSpec((None, 8, zw), lambda b, i: (b, jnp.maximum(i * tb - 1, 0), 0)),
                  pl.BlockSpec((None, 8, zw), lambda b, i: (b, jnp.minimum((i + 1) * tb, nb - 1), 0)),
                  full(mu2), full(vec), full(w2), full(a2), full(g2), full(ones)],
        out_specs=[pl.BlockSpec((None, 3, RK_HEADS, tn, HEAD_DIM), lambda b, i: (b, 0, 0, i, 0)),
                   pl.BlockSpec((None, 2, 3, RK_HEADS, tn, HEAD_DIM), lambda b, i: (b, 0, 0, 0, i, 0)),
                   pl.BlockSpec((None, tn, RK_WIDTH), lambda b, i: (b, i, 0)),
                   pl.BlockSpec((None, tn, RK_WIDTH), lambda b, i: (b, i, 0))],
        out_shape=[S((bsz, 3, RK_HEADS, nt, HEAD_DIM), jnp.float32),
                   S((bsz, 2, 3, RK_HEADS, nt, HEAD_DIM), jnp.float32),
                   S((bsz, nt, RK_WIDTH), jnp.float32), S((bsz, nt, RK_WIDTH), jnp.float32)],
        compiler_params=_compiler_params(("parallel", "parallel")),
    )(z, z, z, mu2, vec, w2, a2, g2, ones)


def _bdot(x, y):
    return _dotf(_bf(x), _bf(y))


def _bdot_nt(x, y):
    return lax.dot_general(_bf(x), _bf(y), _NT_DIMS, preferred_element_type=jnp.float32)


def _bdot_tn(x, y):
    return lax.dot_general(_bf(x), _bf(y), (((0,), (0,)), ((), ())), preferred_element_type=jnp.float32)


def _rwkv_units(units):
    T = units[0][0].shape[0]
    U = range(len(units))
    steps = int(math.log2(T))
    ti = lax.broadcasted_iota(jnp.int32, (T, T), 0)
    si = lax.broadcasted_iota(jnp.int32, (T, T), 1)
    ti2 = lax.broadcasted_iota(jnp.int32, (T, 2 * T), 0)
    si2 = lax.broadcasted_iota(jnp.int32, (T, 2 * T), 1) & (T - 1)
    eye = (lax.broadcasted_iota(jnp.int32, (HEAD_DIM, HEAD_DIM), 0)
           == lax.broadcasted_iota(jnp.int32, (HEAD_DIM, HEAD_DIM), 1))
    masks = {}
    for rev in (False, True):
        incl = (si >= ti) if rev else (si <= ti)
        masks[rev] = (incl.astype(jnp.bfloat16), (si2 >= ti2) if rev else (si2 <= ti2),
                      (si2 > ti2) if rev else (si2 < ti2))
    cum = []
    for (r, kk, v, lw, kd, b, h0, rev) in units:
        l1 = lw.astype(jnp.bfloat16)
        rem = lw - l1.astype(jnp.float32)
        l2 = rem.astype(jnp.bfloat16)
        l3 = (rem - l2.astype(jnp.float32)).astype(jnp.bfloat16)
        tri = masks[rev][0]
        cum.append(_dotf(tri, l1) + (_dotf(tri, l2) + _dotf(tri, l3)))
    at, rt, p, cl = [], [], [], []
    for u, (r, kk, v, lw, kd, b, h0, rev) in enumerate(units):
        c = cum[u]
        cl.append(c[0:1, :] if rev else c[T - 1:T, :])
        e_neg = jnp.exp(-c)
        at.append(-kk * jnp.exp(c - lw))
        rt.append(r * jnp.exp(c))
        p.append(_bdot_nt(jnp.concatenate([at[u], rt[u]], axis=0), jnp.concatenate([b * e_neg, kd * e_neg], axis=0)))
    top = [jnp.where(masks[units[u][7]][2], p[u][:T], 0.0) for u in U]
    l2m = [jnp.where(masks[units[u][7]][1], p[u][T:], 0.0) for u in U]
    npow = [top[u][:, :T] for u in U]
    x = [jnp.concatenate([at[u], _bdot(top[u][:, T:], units[u][2])], axis=1) for u in U]
    for i in range(steps):
        x = [x[u] + _bdot(npow[u], x[u]) for u in U]
        if i < steps - 1:
            npow = [_bdot(npow[u], npow[u]) for u in U]
    z = [jnp.concatenate([x[u], jnp.concatenate([jnp.zeros_like(units[u][2]), units[u][2]], axis=1)], axis=0)
         for u in U]
    ry = [_bdot(l2m[u], z[u]) for u in U]
    gj = []
    for u, (r, kk, v, lw, kd, b, h0, rev) in enumerate(units):
        e_end = jnp.exp(cl[u] - cum[u])
        gj.append(_bdot_tn(jnp.concatenate([b * e_end, kd * e_end], axis=0), z[u]))
    out = []
    for u in U:
        g = jnp.where(eye, jnp.exp(cl[u]), 0.0) + gj[u][:, :HEAD_DIM]
        yh = _bdot(jnp.concatenate([rt[u] + ry[u][:, :HEAD_DIM], g], axis=0), units[u][6])
        out.append((yh[:T] + ry[u][:, HEAD_DIM:], yh[T:] + gj[u][:, HEAD_DIM:]))
    return out


def _rwkv_kernel(cf_ref, cb_ref, df_ref, db_ref, yf_ref, yb_ref, hs):
    c = pl.program_id(1)
    H = cf_ref.shape[1]

    @pl.when(c == 0)
    def _():
        hs[...] = jnp.zeros_like(hs)

    units = []
    for d, (c_ref, d_ref) in enumerate(((cf_ref, df_ref), (cb_ref, db_ref))):
        for h in range(H):
            units.append((c_ref[0, h], c_ref[1, h], c_ref[2, h], d_ref[0, h], d_ref[1, h], d_ref[2, h],
                          hs[d, h], d == 1))
    res = _rwkv_units(units)
    for d, y_ref in enumerate((yf_ref, yb_ref)):
        for h in range(H):
            y, hn = res[d * H + h]
            y_ref[h] = y
            hs[d, h] = hn


def _rwkv_scan(com, dirs, n_ctx):
    bsz, _, H, nt, _ = com.shape
    T = RK_CHUNK
    bmap = functools.partial(_backward_chunk, nc=nt // T, ncc=n_ctx // T)
    cblk = (None, 3, H, T, HEAD_DIM)
    dblk = (None, None, 3, H, T, HEAD_DIM)
    oblk = (None, H, T, HEAD_DIM)
    return pl.pallas_call(
        _rwkv_kernel,
        grid=(bsz, nt // T),
        in_specs=[pl.BlockSpec(cblk, lambda b, c: (b, 0, 0, c, 0)),
                  pl.BlockSpec(cblk, lambda b, c: (b, 0, 0, bmap(c), 0)),
                  pl.BlockSpec(dblk, lambda b, c: (b, 0, 0, 0, c, 0)),
                  pl.BlockSpec(dblk, lambda b, c: (b, 1, 0, 0, bmap(c), 0))],
        out_specs=[pl.BlockSpec(oblk, lambda b, c: (b, 0, c, 0)),
                   pl.BlockSpec(oblk, lambda b, c: (b, 0, bmap(c), 0))],
        out_shape=[jax.ShapeDtypeStruct((bsz, H, nt, HEAD_DIM), jnp.float32)] * 2,
        scratch_shapes=[pltpu.VMEM((2, H, HEAD_DIM, HEAD_DIM), jnp.float32)],
        compiler_params=_compiler_params(("parallel", "arbitrary")),
    )(com, com, dirs, dirs)


def _mixout_kernel(u_ref, sf_ref, sb_ref, s5v_ref, gw_ref, na_ref, rf_ref, rb_ref, grk_ref, bonus_ref, ln_ref,
                   w_ref, ng_ref, gt_ref, x_ref, sc_ref, sh_ref, wrt_ref, o_ref, h_ref, aff_ref):
    y5 = jax.nn.gelu(s5v_ref[0:1, :] * u_ref[...] + sf_ref[...] + sb_ref[...])
    y5 = y5 * jax.nn.sigmoid(_dotf(_bf(y5), gw_ref[...]) + s5v_ref[1:2, :])
    outs = []
    for h in range(RK_HEADS):
        y = rf_ref[h] + rb_ref[h]
        yc = y - jnp.mean(y, axis=-1, keepdims=True)
        var = jnp.mean(yc * yc, axis=-1, keepdims=True)
        outs.append(yc * lax.rsqrt(var + RK_GN_EPS))
    yrk = (jnp.concatenate(outs, axis=-1) * ln_ref[0:1, :] + ln_ref[1:2, :] + bonus_ref[...]) * grk_ref[...]
    z = (_dotf(_bf(y5), w_ref[:S5_WIDTH, :])
         + _dotf(_bf(na_ref[...]), w_ref[S5_WIDTH:S5_WIDTH + NA_WIDTH, :])
         + _dotf(_bf(yrk), w_ref[S5_WIDTH + NA_WIDTH:, :]))
    zn = z * lax.rsqrt(jnp.mean(z * z, axis=-1, keepdims=True) + RMS_EPS)
    x = x_ref[...] + gt_ref[...] * (zn * ng_ref[0:1, :])
    o_ref[...] = x
    y = x * lax.rsqrt(jnp.mean(x * x, axis=-1, keepdims=True) + RMS_EPS)
    h = _bf((y * ng_ref[1:2, :]) * (1.0 + sc_ref[...]) + sh_ref[...])
    h_ref[...] = h
    logits = lax.dot_general(wrt_ref[...], h, _NT_DIMS, preferred_element_type=jnp.float32)
    e = jnp.exp(logits - jnp.max(logits, axis=0, keepdims=True))
    aff_ref[...] = e / jnp.sum(e, axis=0, keepdims=True)


def _mixout(u, sf, sb, s5v, glu_w, y_na, rf, rb, g_rk, bonus, ln, w_out, ng, gates, xc, mods, wrt, n_ctx):
    bsz, nt, d = xc.shape
    tn = _token_tile(n_ctx)
    cb = n_ctx // tn
    ne = wrt.shape[0]

    def tok(w):
        return pl.BlockSpec((None, tn, w), lambda b, i: (b, i, 0))

    def full(a):
        return pl.BlockSpec(a.shape, lambda b, i: (0,) * a.ndim)

    tm = pl.BlockSpec((tn, S5_WIDTH), lambda b, i: (i, b))
    yblk = pl.BlockSpec((None, RK_HEADS, tn, HEAD_DIM), lambda b, i: (b, 0, i, 0))
    return pl.pallas_call(
        _mixout_kernel,
        grid=(bsz, nt // tn),
        in_specs=[tm, tm, tm, full(s5v), full(glu_w), tok(NA_WIDTH), yblk, yblk, tok(RK_WIDTH), tok(RK_WIDTH),
                  full(ln), full(w_out), full(ng),
                  pl.BlockSpec((None, None, None, 1, d), lambda b, i: (b, jnp.minimum(i // cb, 1), 0, 0, 0)),
                  tok(d), *_mod_specs(bsz, d, n_ctx), full(wrt)],
        out_specs=[tok(d), tok(d), pl.BlockSpec((None, ne, tn), lambda b, i: (b, 0, i))],
        out_shape=[jax.ShapeDtypeStruct(xc.shape, jnp.float32), jax.ShapeDtypeStruct(xc.shape, jnp.bfloat16),
                   jax.ShapeDtypeStruct((bsz, ne, nt), jnp.float32)],
        compiler_params=_compiler_params(("parallel", "parallel")),
    )(u, sf, sb, s5v, glu_w, y_na, rf, rb, g_rk, bonus, ln, w_out, ng, gates, xc, mods, mods, wrt)


def _expert_kernel(idx_ref, gate_ref, h_ref, wg_ref, wu_ref, wd_ref, y_ref, *, G, cap, n, t0):
    tok = lax.broadcasted_iota(jnp.int32, (cap, n), 1)
    xin = _bf(jnp.concatenate(
        [_dotf(_bf(idx_ref[s] == tok), h_ref[s, pl.ds(t0, n), :]) for s in range(G)], axis=0))
    acc = jnp.zeros((G * cap, wd_ref.shape[1]), jnp.float32)
    for c in range(wg_ref.shape[1] // FFN_FCHUNK):
        cols = slice(c * FFN_FCHUNK, (c + 1) * FFN_FCHUNK)
        hid = jax.nn.silu(_dotf(xin, wg_ref[:, cols])) * _dotf(xin, wu_ref[:, cols])
        acc = acc + _dotf(_bf(hid), wd_ref[cols, :])
    gate = jnp.concatenate([gate_ref[s] for s in range(G)], axis=0)
    y_ref[...] = _bf(acc * gate)


def _experts(idx, gate, h2, wg, wu, wd, t0, n):
    bsz, ne, cap = idx.shape
    nt, d = h2.shape[1], h2.shape[2]
    f = wg.shape[2]
    G = max(1, min(bsz, FFN_ROWS // cap))
    if t0 % n == 0:
        hspec = pl.BlockSpec((G, n, d), lambda e, b: (b, t0 // n, 0))
        t_in = 0
    else:
        hspec = pl.BlockSpec((G, nt, d), lambda e, b: (b, 0, 0))
        t_in = t0

    def wspec(shape):
        return pl.BlockSpec((None,) + shape, lambda e, b: (e, 0, 0), pipeline_mode=pl.Buffered(1))

    return pl.pallas_call(
        functools.partial(_expert_kernel, G=G, cap=cap, n=n, t0=t_in),
        grid=(ne, bsz // G),
        in_specs=[pl.BlockSpec((G, None, cap, 1), lambda e, b: (b, e, 0, 0)),
                  pl.BlockSpec((G, None, cap, 1), lambda e, b: (b, e, 0, 0)),
                  hspec, wspec((d, f)), wspec((d, f)), wspec((f, d))],
        out_specs=pl.BlockSpec((None, None, G * cap, d), lambda e, b: (b, e, 0, 0)),
        out_shape=jax.ShapeDtypeStruct((bsz // G, ne, G * cap, d), jnp.bfloat16),
        compiler_params=_compiler_params(("arbitrary", "arbitrary")),
    )(idx.reshape(bsz, ne, cap, 1), gate.reshape(bsz, ne, cap, 1), h2, wg, wu, wd)


def _combine_kernel(idx_ref, y_ref, x_ref, g_ref, gt_ref, o_ref, *, tn, t_lo, t_hi):
    i = pl.program_id(1)
    inside = (i >= t_lo) & (i < t_hi)

    @pl.when(inside)
    def _():
        tok = lax.broadcasted_iota(jnp.int32, (tn, idx_ref.shape[1]), 0) + (i - t_lo) * tn
        f = _dotf(_bf(idx_ref[...] == tok), y_ref[...].reshape(idx_ref.shape[1], y_ref.shape[2]))
        fn = f * lax.rsqrt(jnp.mean(f * f, axis=-1, keepdims=True) + RMS_EPS)
        o_ref[...] = x_ref[...] + gt_ref[...] * (fn * g_ref[...])

    @pl.when(jnp.logical_not(inside))
    def _():
        o_ref[...] = x_ref[...]


def _combine(idx, y, xc, g, gt, t0, n, n_ctx):
    bsz, ne, cap = idx.shape
    nt, d = xc.shape[1], xc.shape[2]
    G = bsz // y.shape[0]
    yb = y.reshape(bsz // G, ne, G, cap, d)
    tn = _token_tile(n_ctx)
    S = ne * cap
    return pl.pallas_call(
        functools.partial(_combine_kernel, tn=tn, t_lo=t0 // tn, t_hi=(t0 + n) // tn),
        grid=(bsz, nt // tn),
        in_specs=[pl.BlockSpec((None, 1, S), lambda b, i: (b, 0, 0)),
                  pl.BlockSpec((None, ne, None, cap, d), lambda b, i: (b // G, 0, b % G, 0, 0)),
                  pl.BlockSpec((None, tn, d), lambda b, i: (b, i, 0)),
                  pl.BlockSpec((1, d), lambda b, i: (0, 0)),
                  pl.BlockSpec((None, 1, d), lambda b, i: (b, 0, 0))],
        out_specs=pl.BlockSpec((None, tn, d), lambda b, i: (b, i, 0)),
        out_shape=jax.ShapeDtypeStruct(xc.shape, jnp.float32),
        compiler_params=_compiler_params(("parallel", "arbitrary")),
    )(idx.reshape(bsz, 1, S), yb, xc, g.reshape(1, d), gt.reshape(bsz, 1, d))


def _expert_choice(xc, h2, aff, wg, wu, wd, g, gt, t0, n, n_ctx):
    cap = EC_CAPACITY_FACTOR * n // N_EXPERTS
    gate, idx = lax.top_k(aff[:, :, t0:t0 + n], cap)
    y = _experts(idx, gate, h2, wg, wu, wd, t0, n)
    return _combine(idx, y, xc, g, gt, t0, n, n_ctx)


def kernel(x, c, ctx, c_ctx, ada_w, ada_b, norm_g, w_in, w_out, s5_lam_re, s5_lam_im, s5_log_dt, s5_b_re, s5_b_im, s5_c_re, s5_c_im, s5_d, s5_glu_w, s5_glu_b, na_rpb, rk_mu, rk_w0, rk_w2, rk_a0, rk_a2, rk_g2, rk_k_k, rk_k_a, rk_r_k, rk_ln_w, rk_ln_b, ec_router, ec_w_gate, ec_w_up, ec_w_down):
    bsz, n, d = x.shape
    n_ctx = ctx.shape[1]
    rows = n // GRID_W
    xc = jnp.concatenate([ctx, x], axis=1)
    cos, sin = _rope_tables(n_ctx, n)

    for l in range(DEPTH):
        need_ctx = l < DEPTH - 1
        mod = jax.nn.silu(c) @ ada_w[l] + ada_b[l]
        mod_c = jnp.broadcast_to(jax.nn.silu(c_ctx) @ ada_w[l] + ada_b[l], mod.shape)
        both = jnp.stack([mod_c, mod], axis=1).reshape(bsz, 2, 6, 1, d)
        mods1 = jnp.stack([both[:, :, 1], both[:, :, 0]], axis=2)
        mods2 = jnp.stack([both[:, :, 4], both[:, :, 3]], axis=2)
        gates1 = both[:, :, 2:3]

        u, qkv, z = _inproj(xc, norm_g[l, 0], mods1, _bf(w_in[l]), cos, sin, n_ctx)

        lam, bblk, cblk = _s5_params(s5_lam_re[l], s5_lam_im[l], s5_log_dt[l], s5_b_re[l], s5_b_im[l],
                                     s5_c_re[l], s5_c_im[l])
        sf, sb = _s5_scan(u.reshape(-1, S5_WIDTH), lam, bblk, cblk, bsz, n_ctx)

        bias, off_idx = _na_bias_table(na_rpb[l], rows)
        y_na = _na_attention(qkv, bias, off_idx, n_ctx)

        vec = jnp.stack([rk_k_k[l], rk_k_a[l], rk_r_k[l].reshape(-1), rk_w0[l, 0], rk_w0[l, 1],
                         rk_a0[l, 0], rk_a0[l, 1], jnp.zeros_like(rk_k_k[l])])
        com, dirs, g_rk, bonus = _rkprep(z, rk_mu[l], vec, _bf(rk_w2[l]), _bf(rk_a2[l]), _bf(rk_g2[l]), n_ctx)
        rf, rb = _rwkv_scan(com, dirs, n_ctx)

        xc, h2, aff = _mixout(u, sf.reshape(u.shape), sb.reshape(u.shape), jnp.stack([s5_d[l], s5_glu_b[l]]),
                              _bf(s5_glu_w[l]), y_na, rf, rb, g_rk, bonus, jnp.stack([rk_ln_w[l], rk_ln_b[l]]),
                              _bf(w_out[l]), norm_g[l, 1:4], gates1, xc, mods2, _bf(ec_router[l].T), n_ctx)
        wg, wu, wd = _bf(ec_w_gate[l]), _bf(ec_w_up[l]), _bf(ec_w_down[l])
        x_new = _expert_choice(xc, h2, aff, wg, wu, wd, norm_g[l, 3], both[:, 1, 5, 0], n_ctx, n, n_ctx)
        if need_ctx:
            x_new = _expert_choice(x_new, h2, aff, wg, wu, wd, norm_g[l, 3], both[:, 0, 5, 0], 0, n_ctx, n_ctx)
        xc = x_new
    return xc[:, n_ctx:]
```

```python
import functools
import math

import jax
import jax.numpy as jnp
import numpy as np
from jax import lax
from jax.experimental import pallas as pl
from jax.experimental.pallas import tpu as pltpu

D_MODEL = 1024
DEPTH = 2
GRID_W = 64
D_MIX = D_MODEL
HEAD_DIM = 64
S5_WIDTH = D_MIX // 4
S5_GROUP = 16
S5_GROUPS = S5_WIDTH // S5_GROUP
S5_STATE = 64
NA_WIDTH = (D_MIX - S5_WIDTH) // 2
NA_HEADS = NA_WIDTH // HEAD_DIM
NA_WIN_R = 8
NA_WIN_C = 16
ROPE_BASE = 10000.0
RK_WIDTH = D_MIX - S5_WIDTH - NA_WIDTH
RK_HEADS = RK_WIDTH // HEAD_DIM
RK_DECAY_RANK = 64
RK_A_RANK = 64
RK_GATE_RANK = 128
RK_IN_WIDTH = 3 * RK_WIDTH + 2 * RK_DECAY_RANK + 2 * RK_A_RANK + RK_GATE_RANK
RK_GN_EPS = 64e-5
N_IN = S5_WIDTH + 3 * NA_WIDTH + RK_IN_WIDTH
N_EXPERTS = 16
EC_CAPACITY_FACTOR = 2
RMS_EPS = 1e-6
NEG_INF = -1e30

VMEM_LIMIT_BYTES = 56 * 1024 * 1024
TOKEN_TILE = 256
S5_GP = S5_GROUPS * S5_STATE
S5_CHUNK = 64
S5_LANES = 512
RK_CHUNK = 64
FFN_ROWS = 512
FFN_FCHUNK = 704
CAST_ROWS = 256


def _compiler_params(semantics):
    return pltpu.CompilerParams(dimension_semantics=semantics, vmem_limit_bytes=VMEM_LIMIT_BYTES)


def _token_tile(n_ctx):
    return min(TOKEN_TILE, n_ctx)


def _mod_specs(bsz, d, n_ctx):
    cb = n_ctx // _token_tile(n_ctx)
    return [pl.BlockSpec((None, None, None, 1, d), lambda b, i, j=j: (b, jnp.minimum(i // cb, 1), j, 0, 0))
            for j in range(2)]


def _dotf(x, y):
    return jnp.dot(x, y, preferred_element_type=jnp.float32)


def _bf(x):
    return x.astype(jnp.bfloat16)


def _rope_tables(n_ctx, n):
    t = np.arange(n)
    nf = HEAD_DIM // 4
    inv_freq = ROPE_BASE ** (-np.arange(nf, dtype=np.float32) / nf)
    pos = np.stack([(t // GRID_W).astype(np.float32), (t % GRID_W).astype(np.float32)], axis=1)
    ang = pos[:, :, None] * inv_freq[None, None, :]
    cos = np.repeat(np.cos(ang), 2, axis=1).reshape(n, HEAD_DIM)
    sin = np.sin(ang)
    sin = np.stack([-sin[:, 0], sin[:, 0], -sin[:, 1], sin[:, 1]], axis=1).reshape(n, HEAD_DIM)
    cos = np.concatenate([np.ones((n_ctx, HEAD_DIM), np.float32), cos.astype(np.float32)], axis=0)
    sin = np.concatenate([np.zeros((n_ctx, HEAD_DIM), np.float32), sin.astype(np.float32)], axis=0)
    reps = 2 * NA_HEADS
    return jnp.asarray(np.tile(cos, (1, reps))), jnp.asarray(np.tile(sin, (1, reps)))


def _inproj_kernel(x_ref, g_ref, sc_ref, sh_ref, w_ref, cos_ref, sin_ref, u_ref, qkv_ref, z_ref):
    x = x_ref[...]
    y = x * lax.rsqrt(jnp.mean(x * x, axis=-1, keepdims=True) + RMS_EPS)
    h = _bf((y * g_ref[...]) * (1.0 + sc_ref[...]) + sh_ref[...])
    u_ref[...] = _dotf(h, w_ref[:, :S5_WIDTH])
    z_ref[...] = _dotf(h, w_ref[:, S5_WIDTH + 3 * NA_WIDTH:])
    qk = _dotf(h, w_ref[:, S5_WIDTH:S5_WIDTH + 2 * NA_WIDTH])
    nf = HEAD_DIM // 4
    lane = lax.broadcasted_iota(jnp.int32, qk.shape, 1)
    first = (lane & (2 * nf - 1)) < nf
    partner = jnp.where(first, pltpu.roll(qk, qk.shape[1] - nf, axis=1), pltpu.roll(qk, nf, axis=1))
    qk = _bf(qk * cos_ref[...] + partner * sin_ref[...])
    v = _bf(_dotf(h, w_ref[:, S5_WIDTH + 2 * NA_WIDTH:S5_WIDTH + 3 * NA_WIDTH]))
    for hd in range(NA_HEADS):
        lo = hd * HEAD_DIM
        qkv_ref[0, hd] = qk[:, lo:lo + HEAD_DIM]
        qkv_ref[1, hd] = qk[:, NA_WIDTH + lo:NA_WIDTH + lo + HEAD_DIM]
        qkv_ref[2, hd] = v[:, lo:lo + HEAD_DIM]


def _inproj(xc, g, mods, w_bf16, cos, sin, n_ctx):
    bsz, nt, d = xc.shape
    tn = _token_tile(n_ctx)
    return pl.pallas_call(
        _inproj_kernel,
        grid=(bsz, nt // tn),
        in_specs=[pl.BlockSpec((None, tn, d), lambda b, i: (b, i, 0)),
                  pl.BlockSpec((1, d), lambda b, i: (0, 0)),
                  *_mod_specs(bsz, d, n_ctx),
                  pl.BlockSpec((d, N_IN), lambda b, i: (0, 0)),
                  pl.BlockSpec((tn, 2 * NA_WIDTH), lambda b, i: (i, 0)),
                  pl.BlockSpec((tn, 2 * NA_WIDTH), lambda b, i: (i, 0))],
        out_specs=[pl.BlockSpec((tn, S5_WIDTH), lambda b, i: (i, b)),
                   pl.BlockSpec((None, 3, NA_HEADS, tn, HEAD_DIM), lambda b, i: (b, 0, 0, i, 0)),
                   pl.BlockSpec((None, tn, RK_IN_WIDTH), lambda b, i: (b, i, 0))],
        out_shape=[jax.ShapeDtypeStruct((nt, bsz * S5_WIDTH), jnp.float32),
                   jax.ShapeDtypeStruct((bsz, 3, NA_HEADS, nt, HEAD_DIM), jnp.bfloat16),
                   jax.ShapeDtypeStruct((bsz, nt, RK_IN_WIDTH), jnp.float32)],
        compiler_params=_compiler_params(("parallel", "parallel")),
    )(xc, g.reshape(1, d), mods, mods, w_bf16, cos, sin)


def _s5_kernel(uf_ref, ub_ref, lam_ref, bw_ref, cw_ref, yf_ref, yb_ref, sbuf, st, *, T, B):
    c = pl.program_id(0)

    @pl.when(c == 0)
    def _():
        st[...] = jnp.zeros_like(st)

    for d, u_ref in enumerate((uf_ref, ub_ref)):
        sbuf[d] = _dotf(_bf(u_ref[...]), bw_ref[d])
    for d in range(2):
        for h in range(S5_GP // S5_LANES):
            re_sl = pl.ds(h * S5_LANES, S5_LANES)
            im_sl = pl.ds(S5_GP + h * S5_LANES, S5_LANES)
            lr = jnp.broadcast_to(lam_ref[d, 0:1, h * S5_LANES:(h + 1) * S5_LANES], (B, S5_LANES))
            li = jnp.broadcast_to(lam_ref[d, 1:2, h * S5_LANES:(h + 1) * S5_LANES], (B, S5_LANES))
            s_re, s_im = st[d, :, re_sl], st[d, :, im_sl]
            for i in range(T):
                t = i if d == 0 else T - 1 - i
                rows = pl.ds(t * B, B)
                s_re, s_im = (lr * s_re - li * s_im + sbuf[d, rows, re_sl],
                              lr * s_im + li * s_re + sbuf[d, rows, im_sl])
                sbuf[d, rows, re_sl] = s_re
                sbuf[d, rows, im_sl] = s_im
            st[d, :, re_sl] = s_re
            st[d, :, im_sl] = s_im
    for d, y_ref in enumerate((yf_ref, yb_ref)):
        y_ref[...] = _dotf(_bf(sbuf[d]), cw_ref[d])


def _backward_chunk(c, nc, ncc):
    return jnp.where(c < ncc, ncc - 1 - c, nc + ncc - 1 - c)


def _s5_scan(uf, lam, bblk, cblk, bsz, n_ctx):
    T = S5_CHUNK
    nc = uf.shape[0] // (T * bsz)
    ncc = n_ctx // T
    bw, cw = _bf(bblk), _bf(cblk)
    bmap = functools.partial(_backward_chunk, nc=nc, ncc=ncc)
    blk = (T * bsz, S5_WIDTH)

    def full(shape):
        return pl.BlockSpec(shape, lambda c: (0,) * len(shape))

    return pl.pallas_call(
        functools.partial(_s5_kernel, T=T, B=bsz),
        grid=(nc,),
        in_specs=[pl.BlockSpec(blk, lambda c: (c, 0)), pl.BlockSpec(blk, lambda c: (bmap(c), 0)),
                  full(lam.shape), full(bw.shape), full(cw.shape)],
        out_specs=[pl.BlockSpec(blk, lambda c: (c, 0)), pl.BlockSpec(blk, lambda c: (bmap(c), 0))],
        out_shape=[jax.ShapeDtypeStruct(uf.shape, jnp.float32)] * 2,
        scratch_shapes=[pltpu.VMEM((2, T * bsz, 2 * S5_GP), jnp.float32),
                        pltpu.VMEM((2, bsz, 2 * S5_GP), jnp.float32)],
        compiler_params=_compiler_params(("arbitrary",)),
    )(uf, uf, lam, bw, cw)


def _s5_params(lam_re, lam_im, log_dt, b_re, b_im, c_re, c_im):
    lams, bs, cs = [], [], []
    eye = jnp.eye(S5_GROUPS, dtype=jnp.float32)
    for d in range(2):
        dt = jnp.exp(log_dt[d])[:, None]
        mag = jnp.exp(lam_re[d] * dt)
        lb_re, lb_im = mag * jnp.cos(lam_im[d] * dt), mag * jnp.sin(lam_im[d] * dt)
        den = lam_re[d] ** 2 + lam_im[d] ** 2
        nr, ni = lb_re - 1.0, lb_im
        f_re = (nr * lam_re[d] + ni * lam_im[d]) / den
        f_im = (ni * lam_re[d] - nr * lam_im[d]) / den
        bb_re = f_re[..., None] * b_re - f_im[..., None] * b_im
        bb_im = f_re[..., None] * b_im + f_im[..., None] * b_re

        def blockdiag_in(m):
            return jnp.einsum('gph,gk->ghkp', m, eye).reshape(S5_WIDTH, S5_GP)

        def blockdiag_out(m):
            return jnp.einsum('ghp,gk->gpkh', m, eye).reshape(S5_GP, S5_WIDTH)

        bs.append(jnp.concatenate([blockdiag_in(bb_re), blockdiag_in(bb_im)], axis=1))
        cs.append(jnp.concatenate([blockdiag_out(c_re[d]), -blockdiag_out(c_im[d])], axis=0))
        lams.append(jnp.stack([lb_re.reshape(S5_GP), lb_im.reshape(S5_GP)]))
    return jnp.stack(lams), jnp.stack(bs), jnp.stack(cs)


def _na_offsets(rows):
    win_r = min(NA_WIN_R, rows)
    i = np.arange(rows)
    r0 = np.clip(i - win_r // 2, 0, rows - win_r)
    return r0 - i + NA_WIN_R - 1, win_r


def _na_bias_table(rpb, rows):
    off, win_r = _na_offsets(rows)
    offs = np.unique(off)
    qc = np.arange(GRID_W)[:, None]
    kc = np.arange(GRID_W)[None, :]
    c0 = np.clip(qc - NA_WIN_C // 2, 0, GRID_W - NA_WIN_C)
    valid = (kc >= c0) & (kc < c0 + NA_WIN_C)
    dc = np.clip(kc - qc, 1 - NA_WIN_C, NA_WIN_C - 1) + NA_WIN_C - 1
    full = jnp.where(valid[None, :, None, :], rpb[:, :, dc].transpose(0, 2, 1, 3), NEG_INF)
    full = full.reshape(rpb.shape[0], GRID_W, -1)
    tabs = [full[:, :, o * GRID_W:(o + win_r) * GRID_W] for o in offs]
    return jnp.stack(tabs, axis=1), jnp.asarray(off - offs[0], jnp.int32)


_NT_DIMS = (((1,), (1,)), ((), ()))


def _softmax_pv(s_list, v_list):
    heads = range(len(s_list[0]))
    m = [functools.reduce(jnp.maximum, [jnp.max(s[h], axis=-1, keepdims=True) for s in s_list]) for h in heads]
    p = [[jnp.exp(s[h] - m[h]) for h in heads] for s in s_list]
    den = [sum(jnp.sum(pj[h], axis=-1, keepdims=True) for pj in p) for h in heads]
    o = [sum(_dotf(_bf(pj[h]), vj[h]) for pj, vj in zip(p, v_list)) for h in heads]
    return jnp.concatenate([o[h] / den[h] for h in heads], axis=-1)


def _na_kernel(off_ref, q_ref, k_ref, v_ref, bias_ref, o_ref, *, n_ctx, win_r, rows):
    j = pl.program_id(1)
    cb = n_ctx // GRID_W
    heads = range(q_ref.shape[0])
    scale = HEAD_DIM ** -0.5

    def scores(h, start, size):
        return lax.dot_general(q_ref[h], k_ref[h, pl.ds(start, size), :], _NT_DIMS,
                               preferred_element_type=jnp.float32) * scale

    @pl.when(j < cb)
    def _():
        s = [scores(h, 0, n_ctx) for h in heads]
        o_ref[...] = _softmax_pv([s], [[v_ref[h, pl.ds(0, n_ctx), :] for h in heads]])

    @pl.when(j >= cb)
    def _():
        i = j - cb
        r0 = jnp.clip(i - win_r // 2, 0, rows - win_r)
        start = pl.multiple_of(n_ctx + r0 * GRID_W, GRID_W)
        nk = win_r * GRID_W
        s_lat = [scores(h, start, nk) + bias_ref[h] for h in heads]
        s_ctx = [scores(h, 0, n_ctx) for h in heads]
        o_ref[...] = _softmax_pv([s_lat, s_ctx], [[v_ref[h, pl.ds(start, nk), :] for h in heads],
                                                  [v_ref[h, pl.ds(0, n_ctx), :] for h in heads]])


def _na_attention(qkv, bias, off_idx, n_ctx):
    bsz, _, H, nt, _ = qkv.shape
    rows = (nt - n_ctx) // GRID_W
    win_r = min(NA_WIN_R, rows)
    cb = n_ctx // GRID_W
    return pl.pallas_call(
        functools.partial(_na_kernel, n_ctx=n_ctx, win_r=win_r, rows=rows),
        grid_spec=pltpu.PrefetchScalarGridSpec(
            num_scalar_prefetch=1,
            grid=(bsz, cb + rows),
            in_specs=[pl.BlockSpec((None, None, H, GRID_W, HEAD_DIM), lambda b, j, off: (b, 0, 0, j, 0)),
                      pl.BlockSpec((None, None, H, nt, HEAD_DIM), lambda b, j, off: (b, 1, 0, 0, 0)),
                      pl.BlockSpec((None, None, H, nt, HEAD_DIM), lambda b, j, off: (b, 2, 0, 0, 0)),
                      pl.BlockSpec((H, None, GRID_W, win_r * GRID_W),
                                   lambda b, j, off: (0, off[jnp.maximum(j - cb, 0)], 0, 0))],
            out_specs=pl.BlockSpec((None, GRID_W, H * HEAD_DIM), lambda b, j, off: (b, j, 0)),
        ),
        out_shape=jax.ShapeDtypeStruct((bsz, nt, H * HEAD_DIM), jnp.float32),
        compiler_params=_compiler_params(("parallel", "arbitrary")),
    )(off_idx, qkv, qkv, qkv, bias)


def _head_sum_matrix():
    i = np.arange(RK_WIDTH)
    return jnp.asarray((i[:, None] // HEAD_DIM) == (i[None, :] // HEAD_DIM), jnp.bfloat16)


def _head_sums(x, ones):
    hi = x.astype(jnp.bfloat16)
    lo = (x - hi.astype(jnp.float32)).astype(jnp.bfloat16)
    return _dotf(hi, ones) + _dotf(lo, ones)


def _rkprep_kernel(z_ref, zp_ref, zn_ref, mu_ref, vec_ref, w2_ref, a2_ref, g2_ref, ones_ref,
                   com_ref, dir_ref, g_ref, bonus_ref, *, tn, n_ctx):
    i = pl.program_id(1)
    t0 = i * tn
    nt = pl.num_programs(1) * tn
    keep_prev = jnp.where((t0 == 0) | (t0 == n_ctx), 0.0, 1.0)
    keep_next = jnp.where((t0 + tn == n_ctx) | (t0 + tn == nt), 0.0, 1.0)
    z = z_ref[...]
    row = lax.broadcasted_iota(jnp.int32, z.shape, 0)
    prev = jnp.where(row == 0, zp_ref[7:8, :] * keep_prev, pltpu.roll(z, 1, axis=0))
    nxt = jnp.where(row == tn - 1, zn_ref[0:1, :] * keep_next, pltpu.roll(z, tn - 1, axis=0))
    zs = z + (0.5 * (prev + nxt) - z) * mu_ref[...]
    W, R = RK_WIDTH, RK_DECAY_RANK
    r, k, v = zs[:, :W], zs[:, W:2 * W], zs[:, 2 * W:3 * W]
    o = 3 * W
    zw = (zs[:, o:o + R], zs[:, o + R:o + 2 * R])
    za = (zs[:, o + 2 * R:o + 3 * R], zs[:, o + 3 * R:o + 4 * R])
    zg = zs[:, o + 4 * R:]
    k_k, k_a, r_k = vec_ref[0:1, :], vec_ref[1:2, :], vec_ref[2:3, :]
    ones = ones_ref[...]
    g_ref[...] = _dotf(_bf(jax.nn.sigmoid(zg)), g2_ref[...])
    kk = k * k_k
    kk = kk * lax.rsqrt(jnp.maximum(_head_sums(kk * kk, ones), 1e-24))
    bonus = 0.0
    fields = [r, kk, v]
    for d in range(2):
        w = -jax.nn.softplus(-(vec_ref[3 + d:4 + d, :] + _dotf(_bf(jnp.tanh(zw[d])), w2_ref[d]))) - 0.5
        a = jax.nn.sigmoid(vec_ref[5 + d:6 + d, :] + _dotf(_bf(za[d]), a2_ref[d]))
        kd = k * (1.0 + (a - 1.0) * k_a)
        bonus = bonus + _head_sums(r * kd * r_k, ones)
        fields += [-jnp.exp(w), kd, kk * a]
    bonus_ref[...] = bonus * v
    for j, t in enumerate(fields):
        for h in range(RK_HEADS):
            blk = t[:, h * HEAD_DIM:(h + 1) * HEAD_DIM]
            if j < 3:
                com_ref[j, h] = blk
            else:
                dir_ref[(j - 3) // 3, (j - 3) % 3, h] = blk


def _rkprep(z, mu, vec, w2, a2, g2, n_ctx):
    bsz, nt, zw = z.shape
    tn = _token_tile(n_ctx)
    tb = tn // 8
    nb = nt // 8
    ones = _head_sum_matrix()
    mu2 = mu.reshape(1, zw)

    def full(a):
        return pl.BlockSpec(a.shape, lambda b, i: (0,) * a.ndim)

    S = jax.ShapeDtypeStruct
    return pl.pallas_call(
        functools.partial(_rkprep_kernel, tn=tn, n_ctx=n_ctx),
        grid=(bsz, nt // tn),
        in_specs=[pl.BlockSpec((None, tn, zw), lambda b, i: (b, i, 0)),
                  pl.BlockSpec((None, 8, zw), lambda b, i: (b, jnp.maximum(i * tb - 1, 0), 0)),
                  pl.BlockSpec((None, 8, zw), lambda b, i: (b, jnp.minimum((i + 1) * tb, nb - 1), 0)),
                  full(mu2), full(vec), full(w2), full(a2), full(g2), full(ones)],
        out_specs=[pl.BlockSpec((None, 3, RK_HEADS, tn, HEAD_DIM), lambda b, i: (b, 0, 0, i, 0)),
                   pl.BlockSpec((None, 2, 3, RK_HEADS, tn, HEAD_DIM), lambda b, i: (b, 0, 0, 0, i, 0)),
                   pl.BlockSpec((None, tn, RK_WIDTH), lambda b, i: (b, i, 0)),
                   pl.BlockSpec((None, tn, RK_WIDTH), lambda b, i: (b, i, 0))],
        out_shape=[S((bsz, 3, RK_HEADS, nt, HEAD_DIM), jnp.float32),
                   S((bsz, 2, 3, RK_HEADS, nt, HEAD_DIM), jnp.float32),
                   S((bsz, nt, RK_WIDTH), jnp.float32), S((bsz, nt, RK_WIDTH), jnp.float32)],
        compiler_params=_compiler_params(("parallel", "parallel")),
    )(z, z, z, mu2, vec, w2, a2, g2, ones)


def _bdot(x, y):
    return _dotf(_bf(x), _bf(y))


def _bdot_nt(x, y):
    return lax.dot_general(_bf(x), _bf(y), _NT_DIMS, preferred_element_type=jnp.float32)


def _bdot_tn(x, y):
    return lax.dot_general(_bf(x), _bf(y), (((0,), (0,)), ((), ())), preferred_element_type=jnp.float32)


def _rwkv_units(units):
    T = units[0][0].shape[0]
    U = range(len(units))
    steps = int(math.log2(T))
    ti = lax.broadcasted_iota(jnp.int32, (T, T), 0)
    si = lax.broadcasted_iota(jnp.int32, (T, T), 1)
    ti2 = lax.broadcasted_iota(jnp.int32, (T, 2 * T), 0)
    si2 = lax.broadcasted_iota(jnp.int32, (T, 2 * T), 1) & (T - 1)
    eye = (lax.broadcasted_iota(jnp.int32, (HEAD_DIM, HEAD_DIM), 0)
           == lax.broadcasted_iota(jnp.int32, (HEAD_DIM, HEAD_DIM), 1))
    masks = {}
    for rev in (False, True):
        incl = (si >= ti) if rev else (si <= ti)
        masks[rev] = (incl.astype(jnp.bfloat16), (si2 >= ti2) if rev else (si2 <= ti2),
                      (si2 > ti2) if rev else (si2 < ti2))
    cum = []
    for (r, kk, v, lw, kd, b, h0, rev) in units:
        l1 = lw.astype(jnp.bfloat16)
        rem = lw - l1.astype(jnp.float32)
        l2 = rem.astype(jnp.bfloat16)
        l3 = (rem - l2.astype(jnp.float32)).astype(jnp.bfloat16)
        tri = masks[rev][0]
        cum.append(_dotf(tri, l1) + (_dotf(tri, l2) + _dotf(tri, l3)))
    at, rt, p, cl = [], [], [], []
    for u, (r, kk, v, lw, kd, b, h0, rev) in enumerate(units):
        c = cum[u]
        cl.append(c[0:1, :] if rev else c[T - 1:T, :])
        e_neg = jnp.exp(-c)
        at.append(-kk * jnp.exp(c - lw))
        rt.append(r * jnp.exp(c))
        p.append(_bdot_nt(jnp.concatenate([at[u], rt[u]], axis=0), jnp.concatenate([b * e_neg, kd * e_neg], axis=0)))
    top = [jnp.where(masks[units[u][7]][2], p[u][:T], 0.0) for u in U]
    l2m = [jnp.where(masks[units[u][7]][1], p[u][T:], 0.0) for u in U]
    npow = [top[u][:, :T] for u in U]
    x = [jnp.concatenate([at[u], _bdot(top[u][:, T:], units[u][2])], axis=1) for u in U]
    for i in range(steps):
        x = [x[u] + _bdot(npow[u], x[u]) for u in U]
        if i < steps - 1:
            npow = [_bdot(npow[u], npow[u]) for u in U]
    z = [jnp.concatenate([x[u], jnp.concatenate([jnp.zeros_like(units[u][2]), units[u][2]], axis=1)], axis=0)
         for u in U]
    ry = [_bdot(l2m[u], z[u]) for u in U]
    gj = []
    for u, (r, kk, v, lw, kd, b, h0, rev) in enumerate(units):
        e_end = jnp.exp(cl[u] - cum[u])
        gj.append(_bdot_tn(jnp.concatenate([b * e_end, kd * e_end], axis=0), z[u]))
    out = []
    for u in U:
        g = jnp.where(eye, jnp.exp(cl[u]), 0.0) + gj[u][:, :HEAD_DIM]
        yh = _bdot(jnp.concatenate([rt[u] + ry[u][:, :HEAD_DIM], g], axis=0), units[u][6])
        out.append((yh[:T] + ry[u][:, HEAD_DIM:], yh[T:] + gj[u][:, HEAD_DIM:]))
    return out


def _rwkv_kernel(cf_ref, cb_ref, df_ref, db_ref, yf_ref, yb_ref, hs):
    c = pl.program_id(1)
    H = cf_ref.shape[1]

    @pl.when(c == 0)
    def _():
        hs[...] = jnp.zeros_like(hs)

    units = []
    for d, (c_ref, d_ref) in enumerate(((cf_ref, df_ref), (cb_ref, db_ref))):
        for h in range(H):
            units.append((c_ref[0, h], c_ref[1, h], c_ref[2, h], d_ref[0, h], d_ref[1, h], d_ref[2, h],
                          hs[d, h], d == 1))
    res = _rwkv_units(units)
    for d, y_ref in enumerate((yf_ref, yb_ref)):
        for h in range(H):
            y, hn = res[d * H + h]
            y_ref[h] = y
            hs[d, h] = hn


def _rwkv_scan(com, dirs, n_ctx):
    bsz, _, H, nt, _ = com.shape
    T = RK_CHUNK
    bmap = functools.partial(_backward_chunk, nc=nt // T, ncc=n_ctx // T)
    cblk = (None, 3, H, T, HEAD_DIM)
    dblk = (None, None, 3, H, T, HEAD_DIM)
    oblk = (None, H, T, HEAD_DIM)
    return pl.pallas_call(
        _rwkv_kernel,
        grid=(bsz, nt // T),
        in_specs=[pl.BlockSpec(cblk, lambda b, c: (b, 0, 0, c, 0)),
                  pl.BlockSpec(cblk, lambda b, c: (b, 0, 0, bmap(c), 0)),
                  pl.BlockSpec(dblk, lambda b, c: (b, 0, 0, 0, c, 0)),
                  pl.BlockSpec(dblk, lambda b, c: (b, 1, 0, 0, bmap(c), 0))],
        out_specs=[pl.BlockSpec(oblk, lambda b, c: (b, 0, c, 0)),
                   pl.BlockSpec(oblk, lambda b, c: (b, 0, bmap(c), 0))],
        out_shape=[jax.ShapeDtypeStruct((bsz, H, nt, HEAD_DIM), jnp.float32)] * 2,
        scratch_shapes=[pltpu.VMEM((2, H, HEAD_DIM, HEAD_DIM), jnp.float32)],
        compiler_params=_compiler_params(("parallel", "arbitrary")),
    )(com, com, dirs, dirs)


def _mixout_kernel(u_ref, sf_ref, sb_ref, s5v_ref, gw_ref, na_ref, rf_ref, rb_ref, grk_ref, bonus_ref, ln_ref,
                   w_ref, ng_ref, gt_ref, x_ref, sc_ref, sh_ref, wrt_ref, o_ref, h_ref, aff_ref):
    y5 = jax.nn.gelu(s5v_ref[0:1, :] * u_ref[...] + sf_ref[...] + sb_ref[...])
    y5 = y5 * jax.nn.sigmoid(_dotf(_bf(y5), gw_ref[...]) + s5v_ref[1:2, :])
    outs = []
    for h in range(RK_HEADS):
        y = rf_ref[h] + rb_ref[h]
        yc = y - jnp.mean(y, axis=-1, keepdims=True)
        var = jnp.mean(yc * yc, axis=-1, keepdims=True)
        outs.append(yc * lax.rsqrt(var + RK_GN_EPS))
    yrk = (jnp.concatenate(outs, axis=-1) * ln_ref[0:1, :] + ln_ref[1:2, :] + bonus_ref[...]) * grk_ref[...]
    z = (_dotf(_bf(y5), w_ref[:S5_WIDTH, :])
         + _dotf(_bf(na_ref[...]), w_ref[S5_WIDTH:S5_WIDTH + NA_WIDTH, :])
         + _dotf(_bf(yrk), w_ref[S5_WIDTH + NA_WIDTH:, :]))
    zn = z * lax.rsqrt(jnp.mean(z * z, axis=-1, keepdims=True) + RMS_EPS)
    x = x_ref[...] + gt_ref[...] * (zn * ng_ref[0:1, :])
    o_ref[...] = x
    y = x * lax.rsqrt(jnp.mean(x * x, axis=-1, keepdims=True) + RMS_EPS)
    h = _bf((y * ng_ref[1:2, :]) * (1.0 + sc_ref[...]) + sh_ref[...])
    h_ref[...] = h
    logits = lax.dot_general(wrt_ref[...], h, _NT_DIMS, preferred_element_type=jnp.float32)
    e = jnp.exp(logits - jnp.max(logits, axis=0, keepdims=True))
    aff_ref[...] = e / jnp.sum(e, axis=0, keepdims=True)


def _mixout(u, sf, sb, s5v, glu_w, y_na, rf, rb, g_rk, bonus, ln, w_out, ng, gates, xc, mods, wrt, n_ctx):
    bsz, nt, d = xc.shape
    tn = _token_tile(n_ctx)
    cb = n_ctx // tn
    ne = wrt.shape[0]

    def tok(w):
        return pl.BlockSpec((None, tn, w), lambda b, i: (b, i, 0))

    def full(a):
        return pl.BlockSpec(a.shape, lambda b, i: (0,) * a.ndim)

    tm = pl.BlockSpec((tn, S5_WIDTH), lambda b, i: (i, b))
    yblk = pl.BlockSpec((None, RK_HEADS, tn, HEAD_DIM), lambda b, i: (b, 0, i, 0))
    return pl.pallas_call(
        _mixout_kernel,
        grid=(bsz, nt // tn),
        in_specs=[tm, tm, tm, full(s5v), full(glu_w), tok(NA_WIDTH), yblk, yblk, tok(RK_WIDTH), tok(RK_WIDTH),
                  full(ln), full(w_out), full(ng),
                  pl.BlockSpec((None, None, None, 1, d), lambda b, i: (b, jnp.minimum(i // cb, 1), 0, 0, 0)),
                  tok(d), *_mod_specs(bsz, d, n_ctx), full(wrt)],
        out_specs=[tok(d), tok(d), pl.BlockSpec((None, ne, tn), lambda b, i: (b, 0, i))],
        out_shape=[jax.ShapeDtypeStruct(xc.shape, jnp.float32), jax.ShapeDtypeStruct(xc.shape, jnp.bfloat16),
                   jax.ShapeDtypeStruct((bsz, ne, nt), jnp.float32)],
        compiler_params=_compiler_params(("parallel", "parallel")),
    )(u, sf, sb, s5v, glu_w, y_na, rf, rb, g_rk, bonus, ln, w_out, ng, gates, xc, mods, mods, wrt)


def _cast_kernel(w_ref, o_ref):
    o_ref[...] = _bf(w_ref[...])


def _expert_weights_bf16(w, l):
    _, ne, r, c = w.shape
    tr = CAST_ROWS
    return pl.pallas_call(
        _cast_kernel,
        grid=(ne, r // tr),
        in_specs=[pl.BlockSpec((None, None, tr, c), lambda e, i: (l, e, i, 0))],
        out_specs=pl.BlockSpec((None, tr, c), lambda e, i: (e, i, 0)),
        out_shape=jax.ShapeDtypeStruct((ne, r, c), jnp.bfloat16),
        compiler_params=_compiler_params(("parallel", "parallel")),
    )(w)


def _row_to_col(row):
    c = row.shape[1]
    eye = lax.broadcasted_iota(jnp.int32, (c, c), 0) == lax.broadcasted_iota(jnp.int32, (c, c), 1)
    return jnp.sum(jnp.where(eye, row, 0.0), axis=1, keepdims=True)


def _expert_kernel(idx_ref, gate_ref, h_ref, wg_ref, wu_ref, wd_ref, y_ref, *, G, cap, n, t0):
    e = pl.program_id(0)
    tok = lax.broadcasted_iota(jnp.int32, (cap, n), 1)
    idx = [_row_to_col(idx_ref[s, pl.ds(e, 1), :].astype(jnp.float32)).astype(jnp.int32) for s in range(G)]
    xin = _bf(jnp.concatenate(
        [_dotf(_bf(idx[s] == tok), h_ref[s, pl.ds(t0, n), :]) for s in range(G)], axis=0))
    acc = jnp.zeros((G * cap, wd_ref.shape[1]), jnp.float32)
    for c in range(wg_ref.shape[1] // FFN_FCHUNK):
        cols = slice(c * FFN_FCHUNK, (c + 1) * FFN_FCHUNK)
        hid = jax.nn.silu(_dotf(xin, wg_ref[:, cols])) * _dotf(xin, wu_ref[:, cols])
        acc = acc + _dotf(_bf(hid), wd_ref[cols, :])
    gate = jnp.concatenate([_row_to_col(gate_ref[s, pl.ds(e, 1), :]) for s in range(G)], axis=0)
    y_ref[...] = _bf(acc * gate)


def _experts(idx, gate, h2, wg, wu, wd, t0, n):
    bsz, ne, cap = idx.shape
    nt, d = h2.shape[1], h2.shape[2]
    f = wg.shape[2]
    G = max(1, min(bsz, FFN_ROWS // cap))
    if t0 % n == 0:
        hspec = pl.BlockSpec((G, n, d), lambda e, b: (b, t0 // n, 0))
        t_in = 0
    else:
        hspec = pl.BlockSpec((G, nt, d), lambda e, b: (b, 0, 0))
        t_in = t0

    def wspec(shape):
        return pl.BlockSpec((None,) + shape, lambda e, b: (e, 0, 0), pipeline_mode=pl.Buffered(1))

    return pl.pallas_call(
        functools.partial(_expert_kernel, G=G, cap=cap, n=n, t0=t_in),
        grid=(ne, bsz // G),
        in_specs=[pl.BlockSpec((G, ne, cap), lambda e, b: (b, 0, 0)),
                  pl.BlockSpec((G, ne, cap), lambda e, b: (b, 0, 0)),
                  hspec, wspec((d, f)), wspec((d, f)), wspec((f, d))],
        out_specs=pl.BlockSpec((None, None, G * cap, d), lambda e, b: (b, e, 0, 0)),
        out_shape=jax.ShapeDtypeStruct((bsz // G, ne, G * cap, d), jnp.bfloat16),
        compiler_params=_compiler_params(("arbitrary", "arbitrary")),
    )(idx, gate, h2, wg, wu, wd)


def _combine_kernel(idx_ref, y_ref, x_ref, g_ref, gt_ref, o_ref, *, tn, t_lo, t_hi):
    i = pl.program_id(1)
    inside = (i >= t_lo) & (i < t_hi)

    @pl.when(inside)
    def _():
        tok = lax.broadcasted_iota(jnp.int32, (tn, idx_ref.shape[1]), 0) + (i - t_lo) * tn
        f = _dotf(_bf(idx_ref[...] == tok), y_ref[...].reshape(idx_ref.shape[1], y_ref.shape[2]))
        fn = f * lax.rsqrt(jnp.mean(f * f, axis=-1, keepdims=True) + RMS_EPS)
        o_ref[...] = x_ref[...] + gt_ref[...] * (fn * g_ref[...])

    @pl.when(jnp.logical_not(inside))
    def _():
        o_ref[...] = x_ref[...]


def _combine(idx, y, xc, g, gt, t0, n, n_ctx):
    bsz, ne, cap = idx.shape
    nt, d = xc.shape[1], xc.shape[2]
    G = bsz // y.shape[0]
    yb = y.reshape(bsz // G, ne, G, cap, d)
    tn = _token_tile(n_ctx)
    S = ne * cap
    return pl.pallas_call(
        functools.partial(_combine_kernel, tn=tn, t_lo=t0 // tn, t_hi=(t0 + n) // tn),
        grid=(bsz, nt // tn),
        in_specs=[pl.BlockSpec((None, 1, S), lambda b, i: (b, 0, 0)),
                  pl.BlockSpec((None, ne, None, cap, d), lambda b, i: (b // G, 0, b % G, 0, 0)),
                  pl.BlockSpec((None, tn, d), lambda b, i: (b, i, 0)),
                  pl.BlockSpec((1, d), lambda b, i: (0, 0)),
                  pl.BlockSpec((None, 1, d), lambda b, i: (b, 0, 0))],
        out_specs=pl.BlockSpec((None, tn, d), lambda b, i: (b, i, 0)),
        out_shape=jax.ShapeDtypeStruct(xc.shape, jnp.float32),
        compiler_params=_compiler_params(("parallel", "arbitrary")),
    )(idx.reshape(bsz, 1, S), yb, xc, g.reshape(1, d), gt.reshape(bsz, 1, d))


def _expert_choice(xc, h2, aff, wg, wu, wd, g, gt, t0, n, n_ctx):
    cap = EC_CAPACITY_FACTOR * n // N_EXPERTS
    gate, idx = lax.top_k(aff[:, :, t0:t0 + n], cap)
    y = _experts(idx, gate, h2, wg, wu, wd, t0, n)
    return _combine(idx, y, xc, g, gt, t0, n, n_ctx)


def kernel(x, c, ctx, c_ctx, ada_w, ada_b, norm_g, w_in, w_out, s5_lam_re, s5_lam_im, s5_log_dt, s5_b_re, s5_b_im, s5_c_re, s5_c_im, s5_d, s5_glu_w, s5_glu_b, na_rpb, rk_mu, rk_w0, rk_w2, rk_a0, rk_a2, rk_g2, rk_k_k, rk_k_a, rk_r_k, rk_ln_w, rk_ln_b, ec_router, ec_w_gate, ec_w_up, ec_w_down):
    bsz, n, d = x.shape
    n_ctx = ctx.shape[1]
    rows = n // GRID_W
    xc = jnp.concatenate([ctx, x], axis=1)
    cos, sin = _rope_tables(n_ctx, n)

    for l in range(DEPTH):
        need_ctx = l < DEPTH - 1
        mod = jax.nn.silu(c) @ ada_w[l] + ada_b[l]
        mod_c = jnp.broadcast_to(jax.nn.silu(c_ctx) @ ada_w[l] + ada_b[l], mod.shape)
        both = jnp.stack([mod_c, mod], axis=1).reshape(bsz, 2, 6, 1, d)
        mods1 = jnp.stack([both[:, :, 1], both[:, :, 0]], axis=2)
        mods2 = jnp.stack([both[:, :, 4], both[:, :, 3]], axis=2)
        gates1 = both[:, :, 2:3]

        u, qkv, z = _inproj(xc, norm_g[l, 0], mods1, _bf(w_in[l]), cos, sin, n_ctx)

        lam, bblk, cblk = _s5_params(s5_lam_re[l], s5_lam_im[l], s5_log_dt[l], s5_b_re[l], s5_b_im[l],
                                     s5_c_re[l], s5_c_im[l])
        sf, sb = _s5_scan(u.reshape(-1, S5_WIDTH), lam, bblk, cblk, bsz, n_ctx)

        bias, off_idx = _na_bias_table(na_rpb[l], rows)
        y_na = _na_attention(qkv, bias, off_idx, n_ctx)

        vec = jnp.stack([rk_k_k[l], rk_k_a[l], rk_r_k[l].reshape(-1), rk_w0[l, 0], rk_w0[l, 1],
                         rk_a0[l, 0], rk_a0[l, 1], jnp.zeros_like(rk_k_k[l])])
        com, dirs, g_rk, bonus = _rkprep(z, rk_mu[l], vec, _bf(rk_w2[l]), _bf(rk_a2[l]), _bf(rk_g2[l]), n_ctx)
        rf, rb = _rwkv_scan(com, dirs, n_ctx)

        xc, h2, aff = _mixout(u, sf.reshape(u.shape), sb.reshape(u.shape), jnp.stack([s5_d[l], s5_glu_b[l]]),
                              _bf(s5_glu_w[l]), y_na, rf, rb, g_rk, bonus, jnp.stack([rk_ln_w[l], rk_ln_b[l]]),
                              _bf(w_out[l]), norm_g[l, 1:4], gates1, xc, mods2, _bf(ec_router[l].T), n_ctx)
        wg, wu, wd = (_expert_weights_bf16(w, l) for w in (ec_w_gate, ec_w_up, ec_w_down))
        x_new = _expert_choice(xc, h2, aff, wg, wu, wd, norm_g[l, 3], both[:, 1, 5, 0], n_ctx, n, n_ctx)
        if need_ctx:
            x_new = _expert_choice(x_new, h2, aff, wg, wu, wd, norm_g[l, 3], both[:, 0, 5, 0], 0, n_ctx, n_ctx)
        xc = x_new
    return xc[:, n_ctx:]
```

```python
import functools
import math

import jax
import jax.numpy as jnp
import numpy as np
from jax import lax
from jax.experimental import pallas as pl
from jax.experimental.pallas import tpu as pltpu

D_MODEL = 1024
DEPTH = 2
GRID_W = 64
D_MIX = D_MODEL
HEAD_DIM = 64
S5_WIDTH = D_MIX // 4
S5_GROUP = 16
S5_GROUPS = S5_WIDTH // S5_GROUP
S5_STATE = 64
NA_WIDTH = (D_MIX - S5_WIDTH) // 2
NA_HEADS = NA_WIDTH // HEAD_DIM
NA_WIN_R = 8
NA_WIN_C = 16
ROPE_BASE = 10000.0
RK_WIDTH = D_MIX - S5_WIDTH - NA_WIDTH
RK_HEADS = RK_WIDTH // HEAD_DIM
RK_DECAY_RANK = 64
RK_A_RANK = 64
RK_GATE_RANK = 128
RK_IN_WIDTH = 3 * RK_WIDTH + 2 * RK_DECAY_RANK + 2 * RK_A_RANK + RK_GATE_RANK
RK_GN_EPS = 64e-5
N_IN = S5_WIDTH + 3 * NA_WIDTH + RK_IN_WIDTH
N_EXPERTS = 16
EC_CAPACITY_FACTOR = 2
RMS_EPS = 1e-6
NEG_INF = -1e30

VMEM_LIMIT_BYTES = 56 * 1024 * 1024
TOKEN_TILE = 256
S5_GP = S5_GROUPS * S5_STATE
S5_CHUNK = 64
S5_LANES = 512
RK_CHUNK = 64
RK_SAMPLES = 4
NA_QROWS = 4
FFN_ROWS = 512
FFN_FCHUNK = 704
CAST_ROWS = 256


def _compiler_params(semantics):
    return pltpu.CompilerParams(dimension_semantics=semantics, vmem_limit_bytes=VMEM_LIMIT_BYTES)


def _token_tile(n_ctx):
    return min(TOKEN_TILE, n_ctx)


def _mod_specs(bsz, d, n_ctx):
    cb = n_ctx // _token_tile(n_ctx)
    return [pl.BlockSpec((None, None, None, 1, d), lambda b, i, j=j: (b, jnp.minimum(i // cb, 1), j, 0, 0))
            for j in range(2)]


def _dotf(x, y):
    return jnp.dot(x, y, preferred_element_type=jnp.float32)


def _bf(x):
    return x.astype(jnp.bfloat16)


def _rope_tables(n_ctx, n):
    t = np.arange(n)
    nf = HEAD_DIM // 4
    inv_freq = ROPE_BASE ** (-np.arange(nf, dtype=np.float32) / nf)
    pos = np.stack([(t // GRID_W).astype(np.float32), (t % GRID_W).astype(np.float32)], axis=1)
    ang = pos[:, :, None] * inv_freq[None, None, :]
    cos = np.repeat(np.cos(ang), 2, axis=1).reshape(n, HEAD_DIM)
    sin = np.sin(ang)
    sin = np.stack([-sin[:, 0], sin[:, 0], -sin[:, 1], sin[:, 1]], axis=1).reshape(n, HEAD_DIM)
    cos = np.concatenate([np.ones((n_ctx, HEAD_DIM), np.float32), cos.astype(np.float32)], axis=0)
    sin = np.concatenate([np.zeros((n_ctx, HEAD_DIM), np.float32), sin.astype(np.float32)], axis=0)
    reps = 2 * NA_HEADS
    return jnp.asarray(np.tile(cos, (1, reps))), jnp.asarray(np.tile(sin, (1, reps)))


def _inproj_kernel(x_ref, g_ref, sc_ref, sh_ref, w_ref, cos_ref, sin_ref, u_ref, qkv_ref, z_ref):
    x = x_ref[...]
    y = x * lax.rsqrt(jnp.mean(x * x, axis=-1, keepdims=True) + RMS_EPS)
    h = _bf((y * g_ref[...]) * (1.0 + sc_ref[...]) + sh_ref[...])
    u_ref[...] = _dotf(h, w_ref[:, :S5_WIDTH])
    z_ref[...] = _dotf(h, w_ref[:, S5_WIDTH + 3 * NA_WIDTH:])
    qk = _dotf(h, w_ref[:, S5_WIDTH:S5_WIDTH + 2 * NA_WIDTH])
    nf = HEAD_DIM // 4
    lane = lax.broadcasted_iota(jnp.int32, qk.shape, 1)
    first = (lane & (2 * nf - 1)) < nf
    partner = jnp.where(first, pltpu.roll(qk, qk.shape[1] - nf, axis=1), pltpu.roll(qk, nf, axis=1))
    qk = _bf(qk * cos_ref[...] + partner * sin_ref[...])
    v = _bf(_dotf(h, w_ref[:, S5_WIDTH + 2 * NA_WIDTH:S5_WIDTH + 3 * NA_WIDTH]))
    for hd in range(NA_HEADS):
        lo = hd * HEAD_DIM
        qkv_ref[0, hd] = qk[:, lo:lo + HEAD_DIM]
        qkv_ref[1, hd] = qk[:, NA_WIDTH + lo:NA_WIDTH + lo + HEAD_DIM]
        qkv_ref[2, hd] = v[:, lo:lo + HEAD_DIM]


def _inproj(xc, g, mods, w_bf16, cos, sin, n_ctx):
    bsz, nt, d = xc.shape
    tn = _token_tile(n_ctx)
    return pl.pallas_call(
        _inproj_kernel,
        grid=(bsz, nt // tn),
        in_specs=[pl.BlockSpec((None, tn, d), lambda b, i: (b, i, 0)),
                  pl.BlockSpec((1, d), lambda b, i: (0, 0)),
                  *_mod_specs(bsz, d, n_ctx),
                  pl.BlockSpec((d, N_IN), lambda b, i: (0, 0)),
                  pl.BlockSpec((tn, 2 * NA_WIDTH), lambda b, i: (i, 0)),
                  pl.BlockSpec((tn, 2 * NA_WIDTH), lambda b, i: (i, 0))],
        out_specs=[pl.BlockSpec((tn, S5_WIDTH), lambda b, i: (i, b)),
                   pl.BlockSpec((None, 3, NA_HEADS, tn, HEAD_DIM), lambda b, i: (b, 0, 0, i, 0)),
                   pl.BlockSpec((None, tn, RK_IN_WIDTH), lambda b, i: (b, i, 0))],
        out_shape=[jax.ShapeDtypeStruct((nt, bsz * S5_WIDTH), jnp.float32),
                   jax.ShapeDtypeStruct((bsz, 3, NA_HEADS, nt, HEAD_DIM), jnp.bfloat16),
                   jax.ShapeDtypeStruct((bsz, nt, RK_IN_WIDTH), jnp.float32)],
        compiler_params=_compiler_params(("parallel", "parallel")),
    )(xc, g.reshape(1, d), mods, mods, w_bf16, cos, sin)


def _s5_kernel(uf_ref, ub_ref, lam_ref, bw_ref, cw_ref, yf_ref, yb_ref, sbuf, st, *, T, B):
    c = pl.program_id(0)

    @pl.when(c == 0)
    def _():
        st[...] = jnp.zeros_like(st)

    for d, u_ref in enumerate((uf_ref, ub_ref)):
        sbuf[d] = _dotf(_bf(u_ref[...]), bw_ref[d])
    for d in range(2):
        for h in range(S5_GP // S5_LANES):
            re_sl = pl.ds(h * S5_LANES, S5_LANES)
            im_sl = pl.ds(S5_GP + h * S5_LANES, S5_LANES)
            lr = jnp.broadcast_to(lam_ref[d, 0:1, h * S5_LANES:(h + 1) * S5_LANES], (B, S5_LANES))
            li = jnp.broadcast_to(lam_ref[d, 1:2, h * S5_LANES:(h + 1) * S5_LANES], (B, S5_LANES))
            s_re, s_im = st[d, :, re_sl], st[d, :, im_sl]
            for i in range(T):
                t = i if d == 0 else T - 1 - i
                rows = pl.ds(t * B, B)
                s_re, s_im = (lr * s_re - li * s_im + sbuf[d, rows, re_sl],
                              lr * s_im + li * s_re + sbuf[d, rows, im_sl])
                sbuf[d, rows, re_sl] = s_re
                sbuf[d, rows, im_sl] = s_im
            st[d, :, re_sl] = s_re
            st[d, :, im_sl] = s_im
    for d, y_ref in enumerate((yf_ref, yb_ref)):
        y_ref[...] = _dotf(_bf(sbuf[d]), cw_ref[d])


def _backward_chunk(c, nc, ncc):
    return jnp.where(c < ncc, ncc - 1 - c, nc + ncc - 1 - c)


def _s5_scan(uf, lam, bblk, cblk, bsz, n_ctx):
    T = S5_CHUNK
    nc = uf.shape[0] // (T * bsz)
    ncc = n_ctx // T
    bw, cw = _bf(bblk), _bf(cblk)
    bmap = functools.partial(_backward_chunk, nc=nc, ncc=ncc)
    blk = (T * bsz, S5_WIDTH)

    def full(shape):
        return pl.BlockSpec(shape, lambda c: (0,) * len(shape))

    return pl.pallas_call(
        functools.partial(_s5_kernel, T=T, B=bsz),
        grid=(nc,),
        in_specs=[pl.BlockSpec(blk, lambda c: (c, 0)), pl.BlockSpec(blk, lambda c: (bmap(c), 0)),
                  full(lam.shape), full(bw.shape), full(cw.shape)],
        out_specs=[pl.BlockSpec(blk, lambda c: (c, 0)), pl.BlockSpec(blk, lambda c: (bmap(c), 0))],
        out_shape=[jax.ShapeDtypeStruct(uf.shape, jnp.float32)] * 2,
        scratch_shapes=[pltpu.VMEM((2, T * bsz, 2 * S5_GP), jnp.float32),
                        pltpu.VMEM((2, bsz, 2 * S5_GP), jnp.float32)],
        compiler_params=_compiler_params(("arbitrary",)),
    )(uf, uf, lam, bw, cw)


def _s5_params(lam_re, lam_im, log_dt, b_re, b_im, c_re, c_im):
    lams, bs, cs = [], [], []
    eye = jnp.eye(S5_GROUPS, dtype=jnp.float32)
    for d in range(2):
        dt = jnp.exp(log_dt[d])[:, None]
        mag = jnp.exp(lam_re[d] * dt)
        lb_re, lb_im = mag * jnp.cos(lam_im[d] * dt), mag * jnp.sin(lam_im[d] * dt)
        den = lam_re[d] ** 2 + lam_im[d] ** 2
        nr, ni = lb_re - 1.0, lb_im
        f_re = (nr * lam_re[d] + ni * lam_im[d]) / den
        f_im = (ni * lam_re[d] - nr * lam_im[d]) / den
        bb_re = f_re[..., None] * b_re - f_im[..., None] * b_im
        bb_im = f_re[..., None] * b_im + f_im[..., None] * b_re

        def blockdiag_in(m):
            return jnp.einsum('gph,gk->ghkp', m, eye).reshape(S5_WIDTH, S5_GP)

        def blockdiag_out(m):
            return jnp.einsum('ghp,gk->gpkh', m, eye).reshape(S5_GP, S5_WIDTH)

        bs.append(jnp.concatenate([blockdiag_in(bb_re), blockdiag_in(bb_im)], axis=1))
        cs.append(jnp.concatenate([blockdiag_out(c_re[d]), -blockdiag_out(c_im[d])], axis=0))
        lams.append(jnp.stack([lb_re.reshape(S5_GP), lb_im.reshape(S5_GP)]))
    return jnp.stack(lams), jnp.stack(bs), jnp.stack(cs)


def _na_offsets(rows):
    win_r = min(NA_WIN_R, rows)
    i = np.arange(rows)
    r0 = np.clip(i - win_r // 2, 0, rows - win_r)
    return r0 - i + NA_WIN_R - 1, win_r


def _na_bias_table(rpb, rows):
    off, win_r = _na_offsets(rows)
    offs = np.unique(off)
    qc = np.arange(GRID_W)[:, None]
    kc = np.arange(GRID_W)[None, :]
    c0 = np.clip(qc - NA_WIN_C // 2, 0, GRID_W - NA_WIN_C)
    valid = (kc >= c0) & (kc < c0 + NA_WIN_C)
    dc = np.clip(kc - qc, 1 - NA_WIN_C, NA_WIN_C - 1) + NA_WIN_C - 1
    full = jnp.where(valid[None, :, None, :], rpb[:, :, dc].transpose(0, 2, 1, 3), NEG_INF)
    full = full.reshape(rpb.shape[0], GRID_W, -1)
    tabs = [full[:, :, o * GRID_W:(o + win_r) * GRID_W] for o in offs]
    return jnp.stack(tabs, axis=1), jnp.asarray(off - offs[0], jnp.int32)


_NT_DIMS = (((1,), (1,)), ((), ()))


def _softmax_pv(s_list, v_list):
    units = range(len(s_list[0]))
    m = [functools.reduce(jnp.maximum, [jnp.max(s[u], axis=-1, keepdims=True) for s in s_list]) for u in units]
    p = [[jnp.exp(s[u] - m[u]) for u in units] for s in s_list]
    den = [sum(jnp.sum(pj[u], axis=-1, keepdims=True) for pj in p) for u in units]
    o = [sum(_dotf(_bf(pj[u]), vj[u]) for pj, vj in zip(p, v_list)) for u in units]
    return [o[u] / den[u] for u in units]


def _na_kernel(off_ref, q_ref, k_ref, v_ref, *rest, n_ctx, win_r, rows):
    bias_refs, o_ref = rest[:-1], rest[-1]
    Q = len(bias_refs)
    j = pl.program_id(1) * Q
    cb = n_ctx // GRID_W
    H = q_ref.shape[0]
    units = [(qi, h) for qi in range(Q) for h in range(H)]
    scale = HEAD_DIM ** -0.5

    def scores(qi, h, start, size):
        return lax.dot_general(q_ref[h, qi * GRID_W:(qi + 1) * GRID_W, :], k_ref[h, pl.ds(start, size), :], _NT_DIMS,
                               preferred_element_type=jnp.float32) * scale

    def store(o):
        for qi in range(Q):
            o_ref[qi * GRID_W:(qi + 1) * GRID_W, :] = jnp.concatenate(o[qi * H:(qi + 1) * H], axis=-1)

    @pl.when(j < cb)
    def _():
        s = [scores(qi, h, 0, n_ctx) for qi, h in units]
        store(_softmax_pv([s], [[v_ref[h, pl.ds(0, n_ctx), :] for _, h in units]]))

    @pl.when(j >= cb)
    def _():
        nk = win_r * GRID_W
        start = [pl.multiple_of(n_ctx + jnp.clip(j - cb + qi - win_r // 2, 0, rows - win_r) * GRID_W, GRID_W)
                 for qi in range(Q)]
        s_lat = [scores(qi, h, start[qi], nk) + bias_refs[qi][h] for qi, h in units]
        s_ctx = [scores(qi, h, 0, n_ctx) for qi, h in units]
        store(_softmax_pv([s_lat, s_ctx], [[v_ref[h, pl.ds(start[qi], nk), :] for qi, h in units],
                                           [v_ref[h, pl.ds(0, n_ctx), :] for _, h in units]]))


def _na_attention(qkv, bias, off_idx, n_ctx):
    bsz, _, H, nt, _ = qkv.shape
    rows = (nt - n_ctx) // GRID_W
    win_r = min(NA_WIN_R, rows)
    cb = n_ctx // GRID_W
    Q = NA_QROWS if (cb % NA_QROWS == 0 and rows % NA_QROWS == 0) else 1

    def bias_spec(qi):
        return pl.BlockSpec((H, None, GRID_W, win_r * GRID_W),
                            lambda b, j, off: (0, off[jnp.maximum(j * Q + qi - cb, 0)], 0, 0))

    return pl.pallas_call(
        functools.partial(_na_kernel, n_ctx=n_ctx, win_r=win_r, rows=rows),
        grid_spec=pltpu.PrefetchScalarGridSpec(
            num_scalar_prefetch=1,
            grid=(bsz, (cb + rows) // Q),
            in_specs=[pl.BlockSpec((None, None, H, Q * GRID_W, HEAD_DIM), lambda b, j, off: (b, 0, 0, j, 0)),
                      pl.BlockSpec((None, None, H, nt, HEAD_DIM), lambda b, j, off: (b, 1, 0, 0, 0)),
                      pl.BlockSpec((None, None, H, nt, HEAD_DIM), lambda b, j, off: (b, 2, 0, 0, 0)),
                      *[bias_spec(qi) for qi in range(Q)]],
            out_specs=pl.BlockSpec((None, Q * GRID_W, H * HEAD_DIM), lambda b, j, off: (b, j, 0)),
        ),
        out_shape=jax.ShapeDtypeStruct((bsz, nt, H * HEAD_DIM), jnp.float32),
        compiler_params=_compiler_params(("parallel", "arbitrary")),
    )(off_idx, qkv, qkv, qkv, *([bias] * Q))


def _head_sum_matrix():
    i = np.arange(RK_WIDTH)
    return jnp.asarray((i[:, None] // HEAD_DIM) == (i[None, :] // HEAD_DIM), jnp.bfloat16)


def _head_sums(x, ones):
    hi = x.astype(jnp.bfloat16)
    lo = (x - hi.astype(jnp.float32)).astype(jnp.bfloat16)
    return _dotf(hi, ones) + _dotf(lo, ones)


def _rkprep_kernel(z_ref, zp_ref, zn_ref, mu_ref, vec_ref, w2_ref, a2_ref, g2_ref, ones_ref,
                   com_ref, dir_ref, g_ref, bonus_ref, *, tn, n_ctx):
    i = pl.program_id(1)
    t0 = i * tn
    nt = pl.num_programs(1) * tn
    keep_prev = jnp.where((t0 == 0) | (t0 == n_ctx), 0.0, 1.0)
    keep_next = jnp.where((t0 + tn == n_ctx) | (t0 + tn == nt), 0.0, 1.0)
    z = z_ref[...]
    row = lax.broadcasted_iota(jnp.int32, z.shape, 0)
    prev = jnp.where(row == 0, zp_ref[7:8, :] * keep_prev, pltpu.roll(z, 1, axis=0))
    nxt = jnp.where(row == tn - 1, zn_ref[0:1, :] * keep_next, pltpu.roll(z, tn - 1, axis=0))
    zs = z + (0.5 * (prev + nxt) - z) * mu_ref[...]
    W, R = RK_WIDTH, RK_DECAY_RANK
    r, k, v = zs[:, :W], zs[:, W:2 * W], zs[:, 2 * W:3 * W]
    o = 3 * W
    zw = (zs[:, o:o + R], zs[:, o + R:o + 2 * R])
    za = (zs[:, o + 2 * R:o + 3 * R], zs[:, o + 3 * R:o + 4 * R])
    zg = zs[:, o + 4 * R:]
    k_k, k_a, r_k = vec_ref[0:1, :], vec_ref[1:2, :], vec_ref[2:3, :]
    ones = ones_ref[...]
    g_ref[...] = _dotf(_bf(jax.nn.sigmoid(zg)), g2_ref[...])
    kk = k * k_k
    kk = kk * lax.rsqrt(jnp.maximum(_head_sums(kk * kk, ones), 1e-24))
    bonus = 0.0
    fields = [r, kk, v]
    for d in range(2):
        w = -jax.nn.softplus(-(vec_ref[3 + d:4 + d, :] + _dotf(_bf(jnp.tanh(zw[d])), w2_ref[d]))) - 0.5
        a = jax.nn.sigmoid(vec_ref[5 + d:6 + d, :] + _dotf(_bf(za[d]), a2_ref[d]))
        kd = k * (1.0 + (a - 1.0) * k_a)
        bonus = bonus + _head_sums(r * kd * r_k, ones)
        fields += [-jnp.exp(w), kd, kk * a]
    bonus_ref[...] = bonus * v
    for j, t in enumerate(fields):
        for h in range(RK_HEADS):
            blk = t[:, h * HEAD_DIM:(h + 1) * HEAD_DIM]
            if j < 3:
                com_ref[j, h] = blk
            else:
                dir_ref[(j - 3) // 3, (j - 3) % 3, h] = blk


def _rkprep(z, mu, vec, w2, a2, g2, n_ctx):
    bsz, nt, zw = z.shape
    tn = _token_tile(n_ctx)
    tb = tn // 8
    nb = nt // 8
    ones = _head_sum_matrix()
    mu2 = mu.reshape(1, zw)

    def full(a):
        return pl.BlockSpec(a.shape, lambda b, i: (0,) * a.ndim)

    S = jax.ShapeDtypeStruct
    return pl.pallas_call(
        functools.partial(_rkprep_kernel, tn=tn, n_ctx=n_ctx),
        grid=(bsz, nt // tn),
        in_specs=[pl.BlockSpec((None, tn, zw), lambda b, i: (b, i, 0)),
                  pl.BlockSpec((None, 8, zw), lambda b, i: (b, jnp.maximum(i * tb - 1, 0), 0)),
                  pl.BlockSpec((None, 8, zw), lambda b, i: (b, jnp.minimum((i + 1) * tb, nb - 1), 0)),
                  full(mu2), full(vec), full(w2), full(a2), full(g2), full(ones)],
        out_specs=[pl.BlockSpec((None, 3, RK_HEADS, tn, HEAD_DIM), lambda b, i: (b, 0, 0, i, 0)),
                   pl.BlockSpec((None, 2, 3, RK_HEADS, tn, HEAD_DIM), lambda b, i: (b, 0, 0, 0, i, 0)),
                   pl.BlockSpec((None, tn, RK_WIDTH), lambda b, i: (b, i, 0)),
                   pl.BlockSpec((None, tn, RK_WIDTH), lambda b, i: (b, i, 0))],
        out_shape=[S((bsz, 3, RK_HEADS, nt, HEAD_DIM), jnp.float32),
                   S((bsz, 2, 3, RK_HEADS, nt, HEAD_DIM), jnp.float32),
                   S((bsz, nt, RK_WIDTH), jnp.float32), S((bsz, nt, RK_WIDTH), jnp.float32)],
        compiler_params=_compiler_params(("parallel", "parallel")),
    )(z, z, z, mu2, vec, w2, a2, g2, ones)


def _bdot(x, y):
    return _dotf(_bf(x), _bf(y))


def _bdot_nt(x, y):
    return lax.dot_general(_bf(x), _bf(y), _NT_DIMS, preferred_element_type=jnp.float32)


def _bdot_tn(x, y):
    return lax.dot_general(_bf(x), _bf(y), (((0,), (0,)), ((), ())), preferred_element_type=jnp.float32)


def _rwkv_units(units):
    T = units[0][0].shape[0]
    U = range(len(units))
    steps = int(math.log2(T))
    ti = lax.broadcasted_iota(jnp.int32, (T, T), 0)
    si = lax.broadcasted_iota(jnp.int32, (T, T), 1)
    ti2 = lax.broadcasted_iota(jnp.int32, (T, 2 * T), 0)
    si2 = lax.broadcasted_iota(jnp.int32, (T, 2 * T), 1) & (T - 1)
    eye = (lax.broadcasted_iota(jnp.int32, (HEAD_DIM, HEAD_DIM), 0)
           == lax.broadcasted_iota(jnp.int32, (HEAD_DIM, HEAD_DIM), 1))
    masks = {}
    for rev in (False, True):
        incl = (si >= ti) if rev else (si <= ti)
        masks[rev] = (incl.astype(jnp.bfloat16), (si2 >= ti2) if rev else (si2 <= ti2),
                      (si2 > ti2) if rev else (si2 < ti2))
    cum = []
    for (r, kk, v, lw, kd, b, h0, rev) in units:
        l1 = lw.astype(jnp.bfloat16)
        rem = lw - l1.astype(jnp.float32)
        l2 = rem.astype(jnp.bfloat16)
        l3 = (rem - l2.astype(jnp.float32)).astype(jnp.bfloat16)
        tri = masks[rev][0]
        cum.append(_dotf(tri, l1) + (_dotf(tri, l2) + _dotf(tri, l3)))
    at, rt, p, cl = [], [], [], []
    for u, (r, kk, v, lw, kd, b, h0, rev) in enumerate(units):
        c = cum[u]
        cl.append(c[0:1, :] if rev else c[T - 1:T, :])
        e_neg = jnp.exp(-c)
        at.append(-kk * jnp.exp(c - lw))
        rt.append(r * jnp.exp(c))
        p.append(_bdot_nt(jnp.concatenate([at[u], rt[u]], axis=0), jnp.concatenate([b * e_neg, kd * e_neg], axis=0)))
    top = [jnp.where(masks[units[u][7]][2], p[u][:T], 0.0) for u in U]
    l2m = [jnp.where(masks[units[u][7]][1], p[u][T:], 0.0) for u in U]
    npow = [top[u][:, :T] for u in U]
    x = [jnp.concatenate([at[u], _bdot(top[u][:, T:], units[u][2])], axis=1) for u in U]
    for i in range(steps):
        x = [x[u] + _bdot(npow[u], x[u]) for u in U]
        if i < steps - 1:
            npow = [_bdot(npow[u], npow[u]) for u in U]
    z = [jnp.concatenate([x[u], jnp.concatenate([jnp.zeros_like(units[u][2]), units[u][2]], axis=1)], axis=0)
         for u in U]
    ry = [_bdot(l2m[u], z[u]) for u in U]
    gj = []
    for u, (r, kk, v, lw, kd, b, h0, rev) in enumerate(units):
        e_end = jnp.exp(cl[u] - cum[u])
        gj.append(_bdot_tn(jnp.concatenate([b * e_end, kd * e_end], axis=0), z[u]))
    out = []
    for u in U:
        g = jnp.where(eye, jnp.exp(cl[u]), 0.0) + gj[u][:, :HEAD_DIM]
        yh = _bdot(jnp.concatenate([rt[u] + ry[u][:, :HEAD_DIM], g], axis=0), units[u][6])
        out.append((yh[:T] + ry[u][:, HEAD_DIM:], yh[T:] + gj[u][:, HEAD_DIM:]))
    return out


def _rwkv_kernel(cf_ref, cb_ref, df_ref, db_ref, yf_ref, yb_ref, hs):
    c = pl.program_id(1)
    S, H = cf_ref.shape[0], cf_ref.shape[2]

    @pl.when(c == 0)
    def _():
        hs[...] = jnp.zeros_like(hs)

    units, where = [], []
    for s in range(S):
        for d, (c_ref, d_ref) in enumerate(((cf_ref, df_ref), (cb_ref, db_ref))):
            for h in range(H):
                units.append((c_ref[s, 0, h], c_ref[s, 1, h], c_ref[s, 2, h],
                              d_ref[s, 0, h], d_ref[s, 1, h], d_ref[s, 2, h], hs[s, d, h], d == 1))
                where.append((s, d, h))
    for (s, d, h), (y, hn) in zip(where, _rwkv_units(units)):
        (yf_ref, yb_ref)[d][s, h] = y
        hs[s, d, h] = hn


def _rwkv_scan(com, dirs, n_ctx):
    bsz, _, H, nt, _ = com.shape
    T = RK_CHUNK
    S = RK_SAMPLES if bsz % RK_SAMPLES == 0 else 1
    bmap = functools.partial(_backward_chunk, nc=nt // T, ncc=n_ctx // T)
    cblk = (S, 3, H, T, HEAD_DIM)
    dblk = (S, None, 3, H, T, HEAD_DIM)
    oblk = (S, H, T, HEAD_DIM)
    return pl.pallas_call(
        _rwkv_kernel,
        grid=(bsz // S, nt // T),
        in_specs=[pl.BlockSpec(cblk, lambda b, c: (b, 0, 0, c, 0)),
                  pl.BlockSpec(cblk, lambda b, c: (b, 0, 0, bmap(c), 0)),
                  pl.BlockSpec(dblk, lambda b, c: (b, 0, 0, 0, c, 0)),
                  pl.BlockSpec(dblk, lambda b, c: (b, 1, 0, 0, bmap(c), 0))],
        out_specs=[pl.BlockSpec(oblk, lambda b, c: (b, 0, c, 0)),
                   pl.BlockSpec(oblk, lambda b, c: (b, 0, bmap(c), 0))],
        out_shape=[jax.ShapeDtypeStruct((bsz, H, nt, HEAD_DIM), jnp.float32)] * 2,
        scratch_shapes=[pltpu.VMEM((S, 2, H, HEAD_DIM, HEAD_DIM), jnp.float32)],
        compiler_params=_compiler_params(("parallel", "arbitrary")),
    )(com, com, dirs, dirs)


def _mixout_kernel(u_ref, sf_ref, sb_ref, s5v_ref, gw_ref, na_ref, rf_ref, rb_ref, grk_ref, bonus_ref, ln_ref,
                   w_ref, ng_ref, gt_ref, x_ref, sc_ref, sh_ref, wrt_ref, o_ref, h_ref, aff_ref):
    y5 = jax.nn.gelu(s5v_ref[0:1, :] * u_ref[...] + sf_ref[...] + sb_ref[...])
    y5 = y5 * jax.nn.sigmoid(_dotf(_bf(y5), gw_ref[...]) + s5v_ref[1:2, :])
    outs = []
    for h in range(RK_HEADS):
        y = rf_ref[h] + rb_ref[h]
        yc = y - jnp.mean(y, axis=-1, keepdims=True)
        var = jnp.mean(yc * yc, axis=-1, keepdims=True)
        outs.append(yc * lax.rsqrt(var + RK_GN_EPS))
    yrk = (jnp.concatenate(outs, axis=-1) * ln_ref[0:1, :] + ln_ref[1:2, :] + bonus_ref[...]) * grk_ref[...]
    z = (_dotf(_bf(y5), w_ref[:S5_WIDTH, :])
         + _dotf(_bf(na_ref[...]), w_ref[S5_WIDTH:S5_WIDTH + NA_WIDTH, :])
         + _dotf(_bf(yrk), w_ref[S5_WIDTH + NA_WIDTH:, :]))
    zn = z * lax.rsqrt(jnp.mean(z * z, axis=-1, keepdims=True) + RMS_EPS)
    x = x_ref[...] + gt_ref[...] * (zn * ng_ref[0:1, :])
    o_ref[...] = x
    y = x * lax.rsqrt(jnp.mean(x * x, axis=-1, keepdims=True) + RMS_EPS)
    h = _bf((y * ng_ref[1:2, :]) * (1.0 + sc_ref[...]) + sh_ref[...])
    h_ref[...] = h
    logits = lax.dot_general(wrt_ref[...], h, _NT_DIMS, preferred_element_type=jnp.float32)
    e = jnp.exp(logits - jnp.max(logits, axis=0, keepdims=True))
    aff_ref[...] = e / jnp.sum(e, axis=0, keepdims=True)


def _mixout(u, sf, sb, s5v, glu_w, y_na, rf, rb, g_rk, bonus, ln, w_out, ng, gates, xc, mods, wrt, n_ctx):
    bsz, nt, d = xc.shape
    tn = _token_tile(n_ctx)
    cb = n_ctx // tn
    ne = wrt.shape[0]

    def tok(w):
        return pl.BlockSpec((None, tn, w), lambda b, i: (b, i, 0))

    def full(a):
        return pl.BlockSpec(a.shape, lambda b, i: (0,) * a.ndim)

    tm = pl.BlockSpec((tn, S5_WIDTH), lambda b, i: (i, b))
    yblk = pl.BlockSpec((None, RK_HEADS, tn, HEAD_DIM), lambda b, i: (b, 0, i, 0))
    return pl.pallas_call(
        _mixout_kernel,
        grid=(bsz, nt // tn),
        in_specs=[tm, tm, tm, full(s5v), full(glu_w), tok(NA_WIDTH), yblk, yblk, tok(RK_WIDTH), tok(RK_WIDTH),
                  full(ln), full(w_out), full(ng),
                  pl.BlockSpec((None, None, None, 1, d), lambda b, i: (b, jnp.minimum(i // cb, 1), 0, 0, 0)),
                  tok(d), *_mod_specs(bsz, d, n_ctx), full(wrt)],
        out_specs=[tok(d), tok(d), pl.BlockSpec((None, ne, tn), lambda b, i: (b, 0, i))],
        out_shape=[jax.ShapeDtypeStruct(xc.shape, jnp.float32), jax.ShapeDtypeStruct(xc.shape, jnp.bfloat16),
                   jax.ShapeDtypeStruct((bsz, ne, nt), jnp.float32)],
        compiler_params=_compiler_params(("parallel", "parallel")),
    )(u, sf, sb, s5v, glu_w, y_na, rf, rb, g_rk, bonus, ln, w_out, ng, gates, xc, mods, mods, wrt)


def _cast_kernel(w_ref, o_ref):
    o_ref[...] = _bf(w_ref[...])


def _expert_weights_bf16(w, l):
    _, ne, r, c = w.shape
    tr = CAST_ROWS
    return pl.pallas_call(
        _cast_kernel,
        grid=(ne, r // tr),
        in_specs=[pl.BlockSpec((None, None, tr, c), lambda e, i: (l, e, i, 0))],
        out_specs=pl.BlockSpec((None, tr, c), lambda e, i: (e, i, 0)),
        out_shape=jax.ShapeDtypeStruct((ne, r, c), jnp.bfloat16),
        compiler_params=_compiler_params(("parallel", "parallel")),
    )(w)


def _row_to_col(row):
    c = row.shape[1]
    eye = lax.broadcasted_iota(jnp.int32, (c, c), 0) == lax.broadcasted_iota(jnp.int32, (c, c), 1)
    return jnp.sum(jnp.where(eye, row, 0.0), axis=1, keepdims=True)


def _expert_kernel(idx_ref, gate_ref, h_ref, wg_ref, wu_ref, wd_ref, y_ref, *, G, cap, n, t0):
    e = pl.program_id(0)
    tok = lax.broadcasted_iota(jnp.int32, (cap, n), 1)
    idx = [_row_to_col(idx_ref[s, pl.ds(e, 1), :].astype(jnp.float32)).astype(jnp.int32) for s in range(G)]
    xin = _bf(jnp.concatenate(
        [_dotf(_bf(idx[s] == tok), h_ref[s, pl.ds(t0, n), :]) for s in range(G)], axis=0))
    acc = jnp.zeros((G * cap, wd_ref.shape[1]), jnp.float32)
    for c in range(wg_ref.shape[1] // FFN_FCHUNK):
        cols = slice(c * FFN_FCHUNK, (c + 1) * FFN_FCHUNK)
        hid = jax.nn.silu(_dotf(xin, wg_ref[:, cols])) * _dotf(xin, wu_ref[:, cols])
        acc = acc + _dotf(_bf(hid), wd_ref[cols, :])
    gate = jnp.concatenate([_row_to_col(gate_ref[s, pl.ds(e, 1), :]) for s in range(G)], axis=0)
    y_ref[...] = _bf(acc * gate)


def _experts(idx, gate, h2, wg, wu, wd, t0, n):
    bsz, ne, cap = idx.shape
    nt, d = h2.shape[1], h2.shape[2]
    f = wg.shape[2]
    G = max(1, min(bsz, FFN_ROWS // cap))
    if t0 % n == 0:
        hspec = pl.BlockSpec((G, n, d), lambda e, b: (b, t0 // n, 0))
        t_in = 0
    else:
        hspec = pl.BlockSpec((G, nt, d), lambda e, b: (b, 0, 0))
        t_in = t0

    def wspec(shape):
        return pl.BlockSpec((None,) + shape, lambda e, b: (e, 0, 0), pipeline_mode=pl.Buffered(1))

    return pl.pallas_call(
        functools.partial(_expert_kernel, G=G, cap=cap, n=n, t0=t_in),
        grid=(ne, bsz // G),
        in_specs=[pl.BlockSpec((G, ne, cap), lambda e, b: (b, 0, 0)),
                  pl.BlockSpec((G, ne, cap), lambda e, b: (b, 0, 0)),
                  hspec, wspec((d, f)), wspec((d, f)), wspec((f, d))],
        out_specs=pl.BlockSpec((None, None, G * cap, d), lambda e, b: (b, e, 0, 0)),
        out_shape=jax.ShapeDtypeStruct((bsz // G, ne, G * cap, d), jnp.bfloat16),
        compiler_params=_compiler_params(("arbitrary", "arbitrary")),
    )(idx, gate, h2, wg, wu, wd)


def _combine_kernel(idx_ref, y_ref, x_ref, g_ref, gt_ref, o_ref, *, tn, t_lo, t_hi):
    i = pl.program_id(1)
    inside = (i >= t_lo) & (i < t_hi)

    @pl.when(inside)
    def _():
        tok = lax.broadcasted_iota(jnp.int32, (tn, idx_ref.shape[1]), 0) + (i - t_lo) * tn
        f = _dotf(_bf(idx_ref[...] == tok), y_ref[...].reshape(idx_ref.shape[1], y_ref.shape[2]))
        fn = f * lax.rsqrt(jnp.mean(f * f, axis=-1, keepdims=True) + RMS_EPS)
        o_ref[...] = x_ref[...] + gt_ref[...] * (fn * g_ref[...])

    @pl.when(jnp.logical_not(inside))
    def _():
        o_ref[...] = x_ref[...]


def _combine(idx, y, xc, g, gt, t0, n, n_ctx):
    bsz, ne, cap = idx.shape
    nt, d = xc.shape[1], xc.shape[2]
    G = bsz // y.shape[0]
    yb = y.reshape(bsz // G, ne, G, cap, d)
    tn = _token_tile(n_ctx)
    S = ne * cap
    return pl.pallas_call(
        functools.partial(_combine_kernel, tn=tn, t_lo=t0 // tn, t_hi=(t0 + n) // tn),
        grid=(bsz, nt // tn),
        in_specs=[pl.BlockSpec((None, 1, S), lambda b, i: (b, 0, 0)),
                  pl.BlockSpec((None, ne, None, cap, d), lambda b, i: (b // G, 0, b % G, 0, 0)),
                  pl.BlockSpec((None, tn, d), lambda b, i: (b, i, 0)),
                  pl.BlockSpec((1, d), lambda b, i: (0, 0)),
                  pl.BlockSpec((None, 1, d), lambda b, i: (b, 0, 0))],
        out_specs=pl.BlockSpec((None, tn, d), lambda b, i: (b, i, 0)),
        out_shape=jax.ShapeDtypeStruct(xc.shape, jnp.float32),
        compiler_params=_compiler_params(("parallel", "arbitrary")),
    )(idx.reshape(bsz, 1, S), yb, xc, g.reshape(1, d), gt.reshape(bsz, 1, d))


def _expert_choice(xc, h2, aff, wg, wu, wd, g, gt, t0, n, n_ctx):
    cap = EC_CAPACITY_FACTOR * n // N_EXPERTS
    gate, idx = lax.top_k(aff[:, :, t0:t0 + n], cap)
    y = _experts(idx, gate, h2, wg, wu, wd, t0, n)
    return _combine(idx, y, xc, g, gt, t0, n, n_ctx)


def kernel(x, c, ctx, c_ctx, ada_w, ada_b, norm_g, w_in, w_out, s5_lam_re, s5_lam_im, s5_log_dt, s5_b_re, s5_b_im, s5_c_re, s5_c_im, s5_d, s5_glu_w, s5_glu_b, na_rpb, rk_mu, rk_w0, rk_w2, rk_a0, rk_a2, rk_g2, rk_k_k, rk_k_a, rk_r_k, rk_ln_w, rk_ln_b, ec_router, ec_w_gate, ec_w_up, ec_w_down):
    bsz, n, d = x.shape
    n_ctx = ctx.shape[1]
    rows = n // GRID_W
    xc = jnp.concatenate([ctx, x], axis=1)
    cos, sin = _rope_tables(n_ctx, n)

    for l in range(DEPTH):
        need_ctx = l < DEPTH - 1
        mod = jax.nn.silu(c) @ ada_w[l] + ada_b[l]
        mod_c = jnp.broadcast_to(jax.nn.silu(c_ctx) @ ada_w[l] + ada_b[l], mod.shape)
        both = jnp.stack([mod_c, mod], axis=1).reshape(bsz, 2, 6, 1, d)
        mods1 = jnp.stack([both[:, :, 1], both[:, :, 0]], axis=2)
        mods2 = jnp.stack([both[:, :, 4], both[:, :, 3]], axis=2)
        gates1 = both[:, :, 2:3]

        u, qkv, z = _inproj(xc, norm_g[l, 0], mods1, _bf(w_in[l]), cos, sin, n_ctx)

        lam, bblk, cblk = _s5_params(s5_lam_re[l], s5_lam_im[l], s5_log_dt[l], s5_b_re[l], s5_b_im[l],
                                     s5_c_re[l], s5_c_im[l])
        sf, sb = _s5_scan(u.reshape(-1, S5_WIDTH), lam, bblk, cblk, bsz, n_ctx)

        bias, off_idx = _na_bias_table(na_rpb[l], rows)
        y_na = _na_attention(qkv, bias, off_idx, n_ctx)

        vec = jnp.stack([rk_k_k[l], rk_k_a[l], rk_r_k[l].reshape(-1), rk_w0[l, 0], rk_w0[l, 1],
                         rk_a0[l, 0], rk_a0[l, 1], jnp.zeros_like(rk_k_k[l])])
        com, dirs, g_rk, bonus = _rkprep(z, rk_mu[l], vec, _bf(rk_w2[l]), _bf(rk_a2[l]), _bf(rk_g2[l]), n_ctx)
        rf, rb = _rwkv_scan(com, dirs, n_ctx)

        xc, h2, aff = _mixout(u, sf.reshape(u.shape), sb.reshape(u.shape), jnp.stack([s5_d[l], s5_glu_b[l]]),
                              _bf(s5_glu_w[l]), y_na, rf, rb, g_rk, bonus, jnp.stack([rk_ln_w[l], rk_ln_b[l]]),
                              _bf(w_out[l]), norm_g[l, 1:4], gates1, xc, mods2, _bf(ec_router[l].T), n_ctx)
        wg, wu, wd = (_expert_weights_bf16(w, l) for w in (ec_w_gate, ec_w_up, ec_w_down))
        x_new = _expert_choice(xc, h2, aff, wg, wu, wd, norm_g[l, 3], both[:, 1, 5, 0], n_ctx, n, n_ctx)
        if need_ctx:
            x_new = _expert_choice(x_new, h2, aff, wg, wu, wd, norm_g[l, 3], both[:, 0, 5, 0], 0, n_ctx, n_ctx)
        xc = x_new
    return xc[:, n_ctx:]
```

```python
import functools
import math

import jax
import jax.numpy as jnp
import numpy as np
from jax import lax
from jax.experimental import pallas as pl
from jax.experimental.pallas import tpu as pltpu

D_MODEL = 1024
DEPTH = 2
GRID_W = 64
D_MIX = D_MODEL
HEAD_DIM = 64
S5_WIDTH = D_MIX // 4
S5_GROUP = 16
S5_GROUPS = S5_WIDTH // S5_GROUP
S5_STATE = 64
NA_WIDTH = (D_MIX - S5_WIDTH) // 2
NA_HEADS = NA_WIDTH // HEAD_DIM
NA_WIN_R = 8
NA_WIN_C = 16
ROPE_BASE = 10000.0
RK_WIDTH = D_MIX - S5_WIDTH - NA_WIDTH
RK_HEADS = RK_WIDTH // HEAD_DIM
RK_DECAY_RANK = 64
RK_A_RANK = 64
RK_GATE_RANK = 128
RK_IN_WIDTH = 3 * RK_WIDTH + 2 * RK_DECAY_RANK + 2 * RK_A_RANK + RK_GATE_RANK
RK_GN_EPS = 64e-5
N_IN = S5_WIDTH + 3 * NA_WIDTH + RK_IN_WIDTH
N_EXPERTS = 16
EC_CAPACITY_FACTOR = 2
RMS_EPS = 1e-6
NEG_INF = -1e30

VMEM_LIMIT_BYTES = 56 * 1024 * 1024
TOKEN_TILE = 256
S5_GP = S5_GROUPS * S5_STATE
S5_CHUNK = 64
S5_LANES = 512
RK_CHUNK = 64
RK_SAMPLES = 4
NA_QROWS = 4
FFN_ROWS = 512
FFN_FCHUNK = 768
CAST_ROWS = 256


def _compiler_params(semantics):
    return pltpu.CompilerParams(dimension_semantics=semantics, vmem_limit_bytes=VMEM_LIMIT_BYTES)


def _token_tile(n_ctx):
    return min(TOKEN_TILE, n_ctx)


def _mod_specs(bsz, d, n_ctx):
    cb = n_ctx // _token_tile(n_ctx)
    return [pl.BlockSpec((None, None, None, 1, d), lambda b, i, j=j: (b, jnp.minimum(i // cb, 1), j, 0, 0))
            for j in range(2)]


def _dotf(x, y):
    return jnp.dot(x, y, preferred_element_type=jnp.float32)


def _bf(x):
    return x.astype(jnp.bfloat16)


def _rope_tables(n_ctx, n):
    t = np.arange(n)
    nf = HEAD_DIM // 4
    inv_freq = ROPE_BASE ** (-np.arange(nf, dtype=np.float32) / nf)
    pos = np.stack([(t // GRID_W).astype(np.float32), (t % GRID_W).astype(np.float32)], axis=1)
    ang = pos[:, :, None] * inv_freq[None, None, :]
    cos = np.repeat(np.cos(ang), 2, axis=1).reshape(n, HEAD_DIM)
    sin = np.sin(ang)
    sin = np.stack([-sin[:, 0], sin[:, 0], -sin[:, 1], sin[:, 1]], axis=1).reshape(n, HEAD_DIM)
    cos = np.concatenate([np.ones((n_ctx, HEAD_DIM), np.float32), cos.astype(np.float32)], axis=0)
    sin = np.concatenate([np.zeros((n_ctx, HEAD_DIM), np.float32), sin.astype(np.float32)], axis=0)
    reps = 2 * NA_HEADS
    return jnp.asarray(np.tile(cos, (1, reps))), jnp.asarray(np.tile(sin, (1, reps)))


def _inproj_kernel(x_ref, xp_ref, xn_ref, g_ref, sc_ref, sh_ref, w_ref, cos_ref, sin_ref,
                   mu_ref, vec_ref, w2_ref, a2_ref, g2_ref, ones_ref,
                   u_ref, qkv_ref, com_ref, dir_ref, grk_ref, bonus_ref, *, tn, n_ctx):
    x = jnp.concatenate([xp_ref[...], x_ref[...], xn_ref[...]], axis=0)
    y = x * lax.rsqrt(jnp.mean(x * x, axis=-1, keepdims=True) + RMS_EPS)
    h_ext = _bf((y * g_ref[...]) * (1.0 + sc_ref[...]) + sh_ref[...])
    z_ext = _dotf(h_ext, w_ref[:, S5_WIDTH + 3 * NA_WIDTH:])
    t0 = pl.program_id(1) * tn
    nt = pl.num_programs(1) * tn
    keep_prev = jnp.where((t0 == 0) | (t0 == n_ctx), 0.0, 1.0)
    keep_next = jnp.where((t0 + tn == n_ctx) | (t0 + tn == nt), 0.0, 1.0)
    _rk_prepare(z_ext[8:8 + tn], z_ext[7:8] * keep_prev, z_ext[8 + tn:9 + tn] * keep_next, mu_ref, vec_ref,
                w2_ref, a2_ref, g2_ref, ones_ref, com_ref, dir_ref, grk_ref, bonus_ref)
    h = h_ext[8:8 + tn]
    u_ref[...] = _dotf(h, w_ref[:, :S5_WIDTH])
    qk = _dotf(h, w_ref[:, S5_WIDTH:S5_WIDTH + 2 * NA_WIDTH])
    nf = HEAD_DIM // 4
    lane = lax.broadcasted_iota(jnp.int32, qk.shape, 1)
    first = (lane & (2 * nf - 1)) < nf
    partner = jnp.where(first, pltpu.roll(qk, qk.shape[1] - nf, axis=1), pltpu.roll(qk, nf, axis=1))
    qk = _bf(qk * cos_ref[...] + partner * sin_ref[...])
    v = _bf(_dotf(h, w_ref[:, S5_WIDTH + 2 * NA_WIDTH:S5_WIDTH + 3 * NA_WIDTH]))
    for hd in range(NA_HEADS):
        lo = hd * HEAD_DIM
        qkv_ref[0, hd] = qk[:, lo:lo + HEAD_DIM]
        qkv_ref[1, hd] = qk[:, NA_WIDTH + lo:NA_WIDTH + lo + HEAD_DIM]
        qkv_ref[2, hd] = v[:, lo:lo + HEAD_DIM]


def _inproj(xc, g, mods, w_bf16, cos, sin, mu, vec, w2, a2, g2, n_ctx):
    bsz, nt, d = xc.shape
    tn = _token_tile(n_ctx)
    tb = tn // 8
    nb = nt // 8
    ones = _head_sum_matrix()
    mu2 = mu.reshape(1, RK_IN_WIDTH)

    def full(a):
        return pl.BlockSpec(a.shape, lambda b, i: (0,) * a.ndim)

    S = jax.ShapeDtypeStruct
    return pl.pallas_call(
        functools.partial(_inproj_kernel, tn=tn, n_ctx=n_ctx),
        grid=(bsz, nt // tn),
        in_specs=[pl.BlockSpec((None, tn, d), lambda b, i: (b, i, 0)),
                  pl.BlockSpec((None, 8, d), lambda b, i: (b, jnp.maximum(i * tb - 1, 0), 0)),
                  pl.BlockSpec((None, 8, d), lambda b, i: (b, jnp.minimum((i + 1) * tb, nb - 1), 0)),
                  pl.BlockSpec((1, d), lambda b, i: (0, 0)),
                  *_mod_specs(bsz, d, n_ctx),
                  pl.BlockSpec((d, N_IN), lambda b, i: (0, 0)),
                  pl.BlockSpec((tn, 2 * NA_WIDTH), lambda b, i: (i, 0)),
                  pl.BlockSpec((tn, 2 * NA_WIDTH), lambda b, i: (i, 0)),
                  full(mu2), full(vec), full(w2), full(a2), full(g2), full(ones)],
        out_specs=[pl.BlockSpec((tn, S5_WIDTH), lambda b, i: (i, b)),
                   pl.BlockSpec((None, 3, NA_HEADS, tn, HEAD_DIM), lambda b, i: (b, 0, 0, i, 0)),
                   pl.BlockSpec((None, 3, RK_HEADS, tn, HEAD_DIM), lambda b, i: (b, 0, 0, i, 0)),
                   pl.BlockSpec((None, 2, 3, RK_HEADS, tn, HEAD_DIM), lambda b, i: (b, 0, 0, 0, i, 0)),
                   pl.BlockSpec((None, tn, RK_WIDTH), lambda b, i: (b, i, 0)),
                   pl.BlockSpec((None, tn, RK_WIDTH), lambda b, i: (b, i, 0))],
        out_shape=[S((nt, bsz * S5_WIDTH), jnp.float32),
                   S((bsz, 3, NA_HEADS, nt, HEAD_DIM), jnp.bfloat16),
                   S((bsz, 3, RK_HEADS, nt, HEAD_DIM), jnp.float32),
                   S((bsz, 2, 3, RK_HEADS, nt, HEAD_DIM), jnp.float32),
                   S((bsz, nt, RK_WIDTH), jnp.float32), S((bsz, nt, RK_WIDTH), jnp.float32)],
        compiler_params=_compiler_params(("parallel", "parallel")),
    )(xc, xc, xc, g.reshape(1, d), mods, mods, w_bf16, cos, sin, mu2, vec, w2, a2, g2, ones)


def _s5_kernel(uf_ref, ub_ref, lam_ref, bw_ref, cw_ref, yf_ref, yb_ref, sbuf, st, *, T, B):
    c = pl.program_id(0)

    @pl.when(c == 0)
    def _():
        st[...] = jnp.zeros_like(st)

    for d, u_ref in enumerate((uf_ref, ub_ref)):
        sbuf[d] = _dotf(_bf(u_ref[...]), bw_ref[d])
    for d in range(2):
        for h in range(S5_GP // S5_LANES):
            re_sl = pl.ds(h * S5_LANES, S5_LANES)
            im_sl = pl.ds(S5_GP + h * S5_LANES, S5_LANES)
            lr = jnp.broadcast_to(lam_ref[d, 0:1, h * S5_LANES:(h + 1) * S5_LANES], (B, S5_LANES))
            li = jnp.broadcast_to(lam_ref[d, 1:2, h * S5_LANES:(h + 1) * S5_LANES], (B, S5_LANES))
            s_re, s_im = st[d, :, re_sl], st[d, :, im_sl]
            for i in range(T):
                t = i if d == 0 else T - 1 - i
                rows = pl.ds(t * B, B)
                s_re, s_im = (lr * s_re - li * s_im + sbuf[d, rows, re_sl],
                              lr * s_im + li * s_re + sbuf[d, rows, im_sl])
                sbuf[d, rows, re_sl] = s_re
                sbuf[d, rows, im_sl] = s_im
            st[d, :, re_sl] = s_re
            st[d, :, im_sl] = s_im
    for d, y_ref in enumerate((yf_ref, yb_ref)):
        y_ref[...] = _dotf(_bf(sbuf[d]), cw_ref[d])


def _backward_chunk(c, nc, ncc):
    return jnp.where(c < ncc, ncc - 1 - c, nc + ncc - 1 - c)


def _s5_scan(uf, lam, bblk, cblk, bsz, n_ctx):
    T = S5_CHUNK
    nc = uf.shape[0] // (T * bsz)
    ncc = n_ctx // T
    bw, cw = _bf(bblk), _bf(cblk)
    bmap = functools.partial(_backward_chunk, nc=nc, ncc=ncc)
    blk = (T * bsz, S5_WIDTH)

    def full(shape):
        return pl.BlockSpec(shape, lambda c: (0,) * len(shape))

    return pl.pallas_call(
        functools.partial(_s5_kernel, T=T, B=bsz),
        grid=(nc,),
        in_specs=[pl.BlockSpec(blk, lambda c: (c, 0)), pl.BlockSpec(blk, lambda c: (bmap(c), 0)),
                  full(lam.shape), full(bw.shape), full(cw.shape)],
        out_specs=[pl.BlockSpec(blk, lambda c: (c, 0)), pl.BlockSpec(blk, lambda c: (bmap(c), 0))],
        out_shape=[jax.ShapeDtypeStruct(uf.shape, jnp.float32)] * 2,
        scratch_shapes=[pltpu.VMEM((2, T * bsz, 2 * S5_GP), jnp.float32),
                        pltpu.VMEM((2, bsz, 2 * S5_GP), jnp.float32)],
        compiler_params=_compiler_params(("arbitrary",)),
    )(uf, uf, lam, bw, cw)


def _s5_params(lam_re, lam_im, log_dt, b_re, b_im, c_re, c_im):
    lams, bs, cs = [], [], []
    eye = jnp.eye(S5_GROUPS, dtype=jnp.float32)
    for d in range(2):
        dt = jnp.exp(log_dt[d])[:, None]
        mag = jnp.exp(lam_re[d] * dt)
        lb_re, lb_im = mag * jnp.cos(lam_im[d] * dt), mag * jnp.sin(lam_im[d] * dt)
        den = lam_re[d] ** 2 + lam_im[d] ** 2
        nr, ni = lb_re - 1.0, lb_im
        f_re = (nr * lam_re[d] + ni * lam_im[d]) / den
        f_im = (ni * lam_re[d] - nr * lam_im[d]) / den
        bb_re = f_re[..., None] * b_re - f_im[..., None] * b_im
        bb_im = f_re[..., None] * b_im + f_im[..., None] * b_re

        def blockdiag_in(m):
            return jnp.einsum('gph,gk->ghkp', m, eye).reshape(S5_WIDTH, S5_GP)

        def blockdiag_out(m):
            return jnp.einsum('ghp,gk->gpkh', m, eye).reshape(S5_GP, S5_WIDTH)

        bs.append(jnp.concatenate([blockdiag_in(bb_re), blockdiag_in(bb_im)], axis=1))
        cs.append(jnp.concatenate([blockdiag_out(c_re[d]), -blockdiag_out(c_im[d])], axis=0))
        lams.append(jnp.stack([lb_re.reshape(S5_GP), lb_im.reshape(S5_GP)]))
    return jnp.stack(lams), jnp.stack(bs), jnp.stack(cs)


def _na_offsets(rows):
    win_r = min(NA_WIN_R, rows)
    i = np.arange(rows)
    r0 = np.clip(i - win_r // 2, 0, rows - win_r)
    return r0 - i + NA_WIN_R - 1, win_r


def _na_bias_table(rpb, rows):
    off, win_r = _na_offsets(rows)
    offs = np.unique(off)
    qc = np.arange(GRID_W)[:, None]
    kc = np.arange(GRID_W)[None, :]
    c0 = np.clip(qc - NA_WIN_C // 2, 0, GRID_W - NA_WIN_C)
    valid = (kc >= c0) & (kc < c0 + NA_WIN_C)
    dc = np.clip(kc - qc, 1 - NA_WIN_C, NA_WIN_C - 1) + NA_WIN_C - 1
    full = jnp.where(valid[None, :, None, :], rpb[:, :, dc].transpose(0, 2, 1, 3), NEG_INF)
    full = full.reshape(rpb.shape[0], GRID_W, -1)
    tabs = [full[:, :, o * GRID_W:(o + win_r) * GRID_W] for o in offs]
    return jnp.stack(tabs, axis=1), jnp.asarray(off - offs[0], jnp.int32)


_NT_DIMS = (((1,), (1,)), ((), ()))


def _softmax_pv(s_list, v_list):
    units = range(len(s_list[0]))
    m = [functools.reduce(jnp.maximum, [jnp.max(s[u], axis=-1, keepdims=True) for s in s_list]) for u in units]
    p = [[jnp.exp(s[u] - m[u]) for u in units] for s in s_list]
    den = [sum(jnp.sum(pj[u], axis=-1, keepdims=True) for pj in p) for u in units]
    o = [sum(_dotf(_bf(pj[u]), vj[u]) for pj, vj in zip(p, v_list)) for u in units]
    return [o[u] / den[u] for u in units]


def _na_kernel(off_ref, q_ref, k_ref, v_ref, *rest, n_ctx, win_r, rows):
    bias_refs, o_ref = rest[:-1], rest[-1]
    Q = len(bias_refs)
    j = pl.program_id(1) * Q
    cb = n_ctx // GRID_W
    H = q_ref.shape[0]
    units = [(qi, h) for qi in range(Q) for h in range(H)]
    scale = HEAD_DIM ** -0.5

    def scores(qi, h, start, size):
        return lax.dot_general(q_ref[h, qi * GRID_W:(qi + 1) * GRID_W, :], k_ref[h, pl.ds(start, size), :], _NT_DIMS,
                               preferred_element_type=jnp.float32) * scale

    def store(o):
        for qi in range(Q):
            o_ref[qi * GRID_W:(qi + 1) * GRID_W, :] = jnp.concatenate(o[qi * H:(qi + 1) * H], axis=-1)

    @pl.when(j < cb)
    def _():
        s = [scores(qi, h, 0, n_ctx) for qi, h in units]
        store(_softmax_pv([s], [[v_ref[h, pl.ds(0, n_ctx), :] for _, h in units]]))

    @pl.when(j >= cb)
    def _():
        nk = win_r * GRID_W
        start = [pl.multiple_of(n_ctx + jnp.clip(j - cb + qi - win_r // 2, 0, rows - win_r) * GRID_W, GRID_W)
                 for qi in range(Q)]
        s_lat = [scores(qi, h, start[qi], nk) + bias_refs[qi][h] for qi, h in units]
        s_ctx = [scores(qi, h, 0, n_ctx) for qi, h in units]
        store(_softmax_pv([s_lat, s_ctx], [[v_ref[h, pl.ds(start[qi], nk), :] for qi, h in units],
                                           [v_ref[h, pl.ds(0, n_ctx), :] for _, h in units]]))


def _na_attention(qkv, bias, off_idx, n_ctx):
    bsz, _, H, nt, _ = qkv.shape
    rows = (nt - n_ctx) // GRID_W
    win_r = min(NA_WIN_R, rows)
    cb = n_ctx // GRID_W
    Q = NA_QROWS if (cb % NA_QROWS == 0 and rows % NA_QROWS == 0) else 1

    def bias_spec(qi):
        return pl.BlockSpec((H, None, GRID_W, win_r * GRID_W),
                            lambda b, j, off: (0, off[jnp.maximum(j * Q + qi - cb, 0)], 0, 0))

    return pl.pallas_call(
        functools.partial(_na_kernel, n_ctx=n_ctx, win_r=win_r, rows=rows),
        grid_spec=pltpu.PrefetchScalarGridSpec(
            num_scalar_prefetch=1,
            grid=(bsz, (cb + rows) // Q),
            in_specs=[pl.BlockSpec((None, None, H, Q * GRID_W, HEAD_DIM), lambda b, j, off: (b, 0, 0, j, 0)),
                      pl.BlockSpec((None, None, H, nt, HEAD_DIM), lambda b, j, off: (b, 1, 0, 0, 0)),
                      pl.BlockSpec((None, None, H, nt, HEAD_DIM), lambda b, j, off: (b, 2, 0, 0, 0)),
                      *[bias_spec(qi) for qi in range(Q)]],
            out_specs=pl.BlockSpec((None, Q * GRID_W, H * HEAD_DIM), lambda b, j, off: (b, j, 0)),
        ),
        out_shape=jax.ShapeDtypeStruct((bsz, nt, H * HEAD_DIM), jnp.float32),
        compiler_params=_compiler_params(("parallel", "arbitrary")),
    )(off_idx, qkv, qkv, qkv, *([bias] * Q))


def _head_sum_matrix():
    i = np.arange(RK_WIDTH)
    return jnp.asarray((i[:, None] // HEAD_DIM) == (i[None, :] // HEAD_DIM), jnp.bfloat16)


def _head_sums(x, ones):
    hi = x.astype(jnp.bfloat16)
    lo = (x - hi.astype(jnp.float32)).astype(jnp.bfloat16)
    return _dotf(hi, ones) + _dotf(lo, ones)


def _rk_prepare(z, z_prev, z_next, mu_ref, vec_ref, w2_ref, a2_ref, g2_ref, ones_ref,
                com_ref, dir_ref, g_ref, bonus_ref):
    tn = z.shape[0]
    row = lax.broadcasted_iota(jnp.int32, z.shape, 0)
    prev = jnp.where(row == 0, z_prev, pltpu.roll(z, 1, axis=0))
    nxt = jnp.where(row == tn - 1, z_next, pltpu.roll(z, tn - 1, axis=0))
    zs = z + (0.5 * (prev + nxt) - z) * mu_ref[...]
    W, R = RK_WIDTH, RK_DECAY_RANK
    r, k, v = zs[:, :W], zs[:, W:2 * W], zs[:, 2 * W:3 * W]
    o = 3 * W
    zw = (zs[:, o:o + R], zs[:, o + R:o + 2 * R])
    za = (zs[:, o + 2 * R:o + 3 * R], zs[:, o + 3 * R:o + 4 * R])
    zg = zs[:, o + 4 * R:]
    k_k, k_a, r_k = vec_ref[0:1, :], vec_ref[1:2, :], vec_ref[2:3, :]
    ones = ones_ref[...]
    g_ref[...] = _dotf(_bf(jax.nn.sigmoid(zg)), g2_ref[...])
    kk = k * k_k
    kk = kk * lax.rsqrt(jnp.maximum(_head_sums(kk * kk, ones), 1e-24))
    bonus = 0.0
    fields = [r, kk, v]
    for d in range(2):
        w = -jax.nn.softplus(-(vec_ref[3 + d:4 + d, :] + _dotf(_bf(jnp.tanh(zw[d])), w2_ref[d]))) - 0.5
        a = jax.nn.sigmoid(vec_ref[5 + d:6 + d, :] + _dotf(_bf(za[d]), a2_ref[d]))
        kd = k * (1.0 + (a - 1.0) * k_a)
        bonus = bonus + _head_sums(r * kd * r_k, ones)
        fields += [-jnp.exp(w), kd, kk * a]
    bonus_ref[...] = bonus * v
    for j, t in enumerate(fields):
        for h in range(RK_HEADS):
            blk = t[:, h * HEAD_DIM:(h + 1) * HEAD_DIM]
            if j < 3:
                com_ref[j, h] = blk
            else:
                dir_ref[(j - 3) // 3, (j - 3) % 3, h] = blk


def _bdot(x, y):
    return _dotf(_bf(x), _bf(y))


def _bdot_nt(x, y):
    return lax.dot_general(_bf(x), _bf(y), _NT_DIMS, preferred_element_type=jnp.float32)


def _bdot_tn(x, y):
    return lax.dot_general(_bf(x), _bf(y), (((0,), (0,)), ((), ())), preferred_element_type=jnp.float32)


def _rwkv_units(units):
    T = units[0][0].shape[0]
    U = range(len(units))
    steps = int(math.log2(T))
    ti = lax.broadcasted_iota(jnp.int32, (T, T), 0)
    si = lax.broadcasted_iota(jnp.int32, (T, T), 1)
    ti2 = lax.broadcasted_iota(jnp.int32, (T, 2 * T), 0)
    si2 = lax.broadcasted_iota(jnp.int32, (T, 2 * T), 1) & (T - 1)
    eye = (lax.broadcasted_iota(jnp.int32, (HEAD_DIM, HEAD_DIM), 0)
           == lax.broadcasted_iota(jnp.int32, (HEAD_DIM, HEAD_DIM), 1))
    masks = {}
    for rev in (False, True):
        incl = (si >= ti) if rev else (si <= ti)
        masks[rev] = (incl.astype(jnp.bfloat16), (si2 >= ti2) if rev else (si2 <= ti2),
                      (si2 > ti2) if rev else (si2 < ti2))
    cum = []
    for (r, kk, v, lw, kd, b, h0, rev) in units:
        l1 = lw.astype(jnp.bfloat16)
        rem = lw - l1.astype(jnp.float32)
        l2 = rem.astype(jnp.bfloat16)
        l3 = (rem - l2.astype(jnp.float32)).astype(jnp.bfloat16)
        tri = masks[rev][0]
        cum.append(_dotf(tri, l1) + (_dotf(tri, l2) + _dotf(tri, l3)))
    at, rt, p, cl = [], [], [], []
    for u, (r, kk, v, lw, kd, b, h0, rev) in enumerate(units):
        c = cum[u]
        cl.append(c[0:1, :] if rev else c[T - 1:T, :])
        e_neg = jnp.exp(-c)
        at.append(-kk * jnp.exp(c - lw))
        rt.append(r * jnp.exp(c))
        p.append(_bdot_nt(jnp.concatenate([at[u], rt[u]], axis=0), jnp.concatenate([b * e_neg, kd * e_neg], axis=0)))
    top = [jnp.where(masks[units[u][7]][2], p[u][:T], 0.0) for u in U]
    l2m = [jnp.where(masks[units[u][7]][1], p[u][T:], 0.0) for u in U]
    npow = [top[u][:, :T] for u in U]
    x = [jnp.concatenate([at[u], _bdot(top[u][:, T:], units[u][2])], axis=1) for u in U]
    for i in range(steps):
        x = [x[u] + _bdot(npow[u], x[u]) for u in U]
        if i < steps - 1:
            npow = [_bdot(npow[u], npow[u]) for u in U]
    z = [jnp.concatenate([x[u], jnp.concatenate([jnp.zeros_like(units[u][2]), units[u][2]], axis=1)], axis=0)
         for u in U]
    ry = [_bdot(l2m[u], z[u]) for u in U]
    gj = []
    for u, (r, kk, v, lw, kd, b, h0, rev) in enumerate(units):
        e_end = jnp.exp(cl[u] - cum[u])
        gj.append(_bdot_tn(jnp.concatenate([b * e_end, kd * e_end], axis=0), z[u]))
    out = []
    for u in U:
        g = jnp.where(eye, jnp.exp(cl[u]), 0.0) + gj[u][:, :HEAD_DIM]
        yh = _bdot(jnp.concatenate([rt[u] + ry[u][:, :HEAD_DIM], g], axis=0), units[u][6])
        out.append((yh[:T] + ry[u][:, HEAD_DIM:], yh[T:] + gj[u][:, HEAD_DIM:]))
    return out


def _rwkv_kernel(cf_ref, cb_ref, df_ref, db_ref, yf_ref, yb_ref, hs):
    c = pl.program_id(1)
    S, H = cf_ref.shape[0], cf_ref.shape[2]

    @pl.when(c == 0)
    def _():
        hs[...] = jnp.zeros_like(hs)

    units, where = [], []
    for s in range(S):
        for d, (c_ref, d_ref) in enumerate(((cf_ref, df_ref), (cb_ref, db_ref))):
            for h in range(H):
                units.append((c_ref[s, 0, h], c_ref[s, 1, h], c_ref[s, 2, h],
                              d_ref[s, 0, h], d_ref[s, 1, h], d_ref[s, 2, h], hs[s, d, h], d == 1))
                where.append((s, d, h))
    for (s, d, h), (y, hn) in zip(where, _rwkv_units(units)):
        (yf_ref, yb_ref)[d][s, h] = y
        hs[s, d, h] = hn


def _rwkv_scan(com, dirs, n_ctx):
    bsz, _, H, nt, _ = com.shape
    T = RK_CHUNK
    S = RK_SAMPLES if bsz % RK_SAMPLES == 0 else 1
    bmap = functools.partial(_backward_chunk, nc=nt // T, ncc=n_ctx // T)
    cblk = (S, 3, H, T, HEAD_DIM)
    dblk = (S, None, 3, H, T, HEAD_DIM)
    oblk = (S, H, T, HEAD_DIM)
    return pl.pallas_call(
        _rwkv_kernel,
        grid=(bsz // S, nt // T),
        in_specs=[pl.BlockSpec(cblk, lambda b, c: (b, 0, 0, c, 0)),
                  pl.BlockSpec(cblk, lambda b, c: (b, 0, 0, bmap(c), 0)),
                  pl.BlockSpec(dblk, lambda b, c: (b, 0, 0, 0, c, 0)),
                  pl.BlockSpec(dblk, lambda b, c: (b, 1, 0, 0, bmap(c), 0))],
        out_specs=[pl.BlockSpec(oblk, lambda b, c: (b, 0, c, 0)),
                   pl.BlockSpec(oblk, lambda b, c: (b, 0, bmap(c), 0))],
        out_shape=[jax.ShapeDtypeStruct((bsz, H, nt, HEAD_DIM), jnp.float32)] * 2,
        scratch_shapes=[pltpu.VMEM((S, 2, H, HEAD_DIM, HEAD_DIM), jnp.float32)],
        compiler_params=_compiler_params(("parallel", "arbitrary")),
    )(com, com, dirs, dirs)


def _mixout_kernel(u_ref, sf_ref, sb_ref, s5v_ref, gw_ref, na_ref, rf_ref, rb_ref, grk_ref, bonus_ref, ln_ref,
                   w_ref, ng_ref, gt_ref, x_ref, sc_ref, sh_ref, wrt_ref, o_ref, h_ref, aff_ref):
    y5 = jax.nn.gelu(s5v_ref[0:1, :] * u_ref[...] + sf_ref[...] + sb_ref[...])
    y5 = y5 * jax.nn.sigmoid(_dotf(_bf(y5), gw_ref[...]) + s5v_ref[1:2, :])
    outs = []
    for h in range(RK_HEADS):
        y = rf_ref[h] + rb_ref[h]
        yc = y - jnp.mean(y, axis=-1, keepdims=True)
        var = jnp.mean(yc * yc, axis=-1, keepdims=True)
        outs.append(yc * lax.rsqrt(var + RK_GN_EPS))
    yrk = (jnp.concatenate(outs, axis=-1) * ln_ref[0:1, :] + ln_ref[1:2, :] + bonus_ref[...]) * grk_ref[...]
    z = (_dotf(_bf(y5), w_ref[:S5_WIDTH, :])
         + _dotf(_bf(na_ref[...]), w_ref[S5_WIDTH:S5_WIDTH + NA_WIDTH, :])
         + _dotf(_bf(yrk), w_ref[S5_WIDTH + NA_WIDTH:, :]))
    zn = z * lax.rsqrt(jnp.mean(z * z, axis=-1, keepdims=True) + RMS_EPS)
    x = x_ref[...] + gt_ref[...] * (zn * ng_ref[0:1, :])
    o_ref[...] = x
    y = x * lax.rsqrt(jnp.mean(x * x, axis=-1, keepdims=True) + RMS_EPS)
    h = _bf((y * ng_ref[1:2, :]) * (1.0 + sc_ref[...]) + sh_ref[...])
    h_ref[...] = h
    logits = lax.dot_general(wrt_ref[...], h, _NT_DIMS, preferred_element_type=jnp.float32)
    e = jnp.exp(logits - jnp.max(logits, axis=0, keepdims=True))
    aff_ref[...] = e / jnp.sum(e, axis=0, keepdims=True)


def _mixout(u, sf, sb, s5v, glu_w, y_na, rf, rb, g_rk, bonus, ln, w_out, ng, gates, xc, mods, wrt, n_ctx):
    bsz, nt, d = xc.shape
    tn = _token_tile(n_ctx)
    cb = n_ctx // tn
    ne = wrt.shape[0]

    def tok(w):
        return pl.BlockSpec((None, tn, w), lambda b, i: (b, i, 0))

    def full(a):
        return pl.BlockSpec(a.shape, lambda b, i: (0,) * a.ndim)

    tm = pl.BlockSpec((tn, S5_WIDTH), lambda b, i: (i, b))
    yblk = pl.BlockSpec((None, RK_HEADS, tn, HEAD_DIM), lambda b, i: (b, 0, i, 0))
    return pl.pallas_call(
        _mixout_kernel,
        grid=(bsz, nt // tn),
        in_specs=[tm, tm, tm, full(s5v), full(glu_w), tok(NA_WIDTH), yblk, yblk, tok(RK_WIDTH), tok(RK_WIDTH),
                  full(ln), full(w_out), full(ng),
                  pl.BlockSpec((None, None, None, 1, d), lambda b, i: (b, jnp.minimum(i // cb, 1), 0, 0, 0)),
                  tok(d), *_mod_specs(bsz, d, n_ctx), full(wrt)],
        out_specs=[tok(d), tok(d), pl.BlockSpec((None, ne, tn), lambda b, i: (b, 0, i))],
        out_shape=[jax.ShapeDtypeStruct(xc.shape, jnp.float32), jax.ShapeDtypeStruct(xc.shape, jnp.bfloat16),
                   jax.ShapeDtypeStruct((bsz, ne, nt), jnp.float32)],
        compiler_params=_compiler_params(("parallel", "parallel")),
    )(u, sf, sb, s5v, glu_w, y_na, rf, rb, g_rk, bonus, ln, w_out, ng, gates, xc, mods, mods, wrt)


def _cast_kernel(w_ref, o_ref):
    o_ref[...] = _bf(w_ref[...])


def _expert_weights_bf16(w, l):
    _, ne, r, c = w.shape
    tr = CAST_ROWS
    return pl.pallas_call(
        _cast_kernel,
        grid=(ne, r // tr),
        in_specs=[pl.BlockSpec((None, None, tr, c), lambda e, i: (l, e, i, 0))],
        out_specs=pl.BlockSpec((None, tr, c), lambda e, i: (e, i, 0)),
        out_shape=jax.ShapeDtypeStruct((ne, r, c), jnp.bfloat16),
        compiler_params=_compiler_params(("parallel", "parallel")),
    )(w)


def _row_to_col(row):
    c = row.shape[1]
    eye = lax.broadcasted_iota(jnp.int32, (c, c), 0) == lax.broadcasted_iota(jnp.int32, (c, c), 1)
    return jnp.sum(jnp.where(eye, row, 0.0), axis=1, keepdims=True)


def _expert_kernel(idx_ref, gate_ref, h_ref, wg_ref, wu_ref, wd_ref, y_ref, *, G, cap, n, t0):
    e = pl.program_id(0)
    tok = lax.broadcasted_iota(jnp.int32, (cap, n), 1)
    idx = [_row_to_col(idx_ref[s, pl.ds(e, 1), :].astype(jnp.float32)).astype(jnp.int32) for s in range(G)]
    xin = _bf(jnp.concatenate(
        [_dotf(_bf(idx[s] == tok), h_ref[s, pl.ds(t0, n), :]) for s in range(G)], axis=0))
    acc = jnp.zeros((G * cap, wd_ref.shape[1]), jnp.float32)
    for c0 in range(0, wg_ref.shape[1], FFN_FCHUNK):
        cols = slice(c0, min(c0 + FFN_FCHUNK, wg_ref.shape[1]))
        hid = jax.nn.silu(_dotf(xin, wg_ref[:, cols])) * _dotf(xin, wu_ref[:, cols])
        acc = acc + _dotf(_bf(hid), wd_ref[cols, :])
    gate = jnp.concatenate([_row_to_col(gate_ref[s, pl.ds(e, 1), :]) for s in range(G)], axis=0)
    y_ref[...] = _bf(acc * gate)


def _experts(idx, gate, h2, wg, wu, wd, t0, n):
    bsz, ne, cap = idx.shape
    nt, d = h2.shape[1], h2.shape[2]
    f = wg.shape[2]
    G = max(1, min(bsz, FFN_ROWS // cap))
    if t0 % n == 0:
        hspec = pl.BlockSpec((G, n, d), lambda e, b: (b, t0 // n, 0))
        t_in = 0
    else:
        hspec = pl.BlockSpec((G, nt, d), lambda e, b: (b, 0, 0))
        t_in = t0

    def wspec(shape):
        return pl.BlockSpec((None,) + shape, lambda e, b: (e, 0, 0), pipeline_mode=pl.Buffered(1))

    return pl.pallas_call(
        functools.partial(_expert_kernel, G=G, cap=cap, n=n, t0=t_in),
        grid=(ne, bsz // G),
        in_specs=[pl.BlockSpec((G, ne, cap), lambda e, b: (b, 0, 0)),
                  pl.BlockSpec((G, ne, cap), lambda e, b: (b, 0, 0)),
                  hspec, wspec((d, f)), wspec((d, f)), wspec((f, d))],
        out_specs=pl.BlockSpec((None, None, G * cap, d), lambda e, b: (b, e, 0, 0)),
        out_shape=jax.ShapeDtypeStruct((bsz // G, ne, G * cap, d), jnp.bfloat16),
        compiler_params=_compiler_params(("arbitrary", "arbitrary")),
    )(idx, gate, h2, wg, wu, wd)


def _combine_kernel(idx_ref, y_ref, x_ref, g_ref, gt_ref, o_ref, *, tn, t_lo, t_hi):
    i = pl.program_id(1)
    inside = (i >= t_lo) & (i < t_hi)

    @pl.when(inside)
    def _():
        tok = lax.broadcasted_iota(jnp.int32, (tn, idx_ref.shape[1]), 0) + (i - t_lo) * tn
        f = _dotf(_bf(idx_ref[...] == tok), y_ref[...].reshape(idx_ref.shape[1], y_ref.shape[2]))
        fn = f * lax.rsqrt(jnp.mean(f * f, axis=-1, keepdims=True) + RMS_EPS)
        o_ref[...] = x_ref[...] + gt_ref[...] * (fn * g_ref[...])

    @pl.when(jnp.logical_not(inside))
    def _():
        o_ref[...] = x_ref[...]


def _combine(idx, y, xc, g, gt, t0, n, n_ctx, keep_rest):
    bsz, ne, cap = idx.shape
    nt, d = xc.shape[1], xc.shape[2]
    G = bsz // y.shape[0]
    yb = y.reshape(bsz // G, ne, G, cap, d)
    tn = _token_tile(n_ctx)
    S = ne * cap
    t_lo, t_hi = t0 // tn, (t0 + n) // tn
    first, tiles = (0, nt // tn) if keep_rest else (t_lo, t_hi - t_lo)
    return pl.pallas_call(
        functools.partial(_combine_kernel, tn=tn, t_lo=t_lo - first, t_hi=t_hi - first),
        grid=(bsz, tiles),
        in_specs=[pl.BlockSpec((None, 1, S), lambda b, i: (b, 0, 0)),
                  pl.BlockSpec((None, ne, None, cap, d), lambda b, i: (b // G, 0, b % G, 0, 0)),
                  pl.BlockSpec((None, tn, d), lambda b, i: (b, i + first, 0)),
                  pl.BlockSpec((1, d), lambda b, i: (0, 0)),
                  pl.BlockSpec((None, 1, d), lambda b, i: (b, 0, 0))],
        out_specs=pl.BlockSpec((None, tn, d), lambda b, i: (b, i, 0)),
        out_shape=jax.ShapeDtypeStruct((bsz, tiles * tn, d), jnp.float32),
        compiler_params=_compiler_params(("parallel", "arbitrary")),
    )(idx.reshape(bsz, 1, S), yb, xc, g.reshape(1, d), gt.reshape(bsz, 1, d))


def _expert_choice(xc, h2, aff, wg, wu, wd, g, gt, t0, n, n_ctx, keep_rest=True):
    cap = EC_CAPACITY_FACTOR * n // N_EXPERTS
    gate, idx = lax.top_k(aff[:, :, t0:t0 + n], cap)
    y = _experts(idx, gate, h2, wg, wu, wd, t0, n)
    return _combine(idx, y, xc, g, gt, t0, n, n_ctx, keep_rest)


def kernel(x, c, ctx, c_ctx, ada_w, ada_b, norm_g, w_in, w_out, s5_lam_re, s5_lam_im, s5_log_dt, s5_b_re, s5_b_im, s5_c_re, s5_c_im, s5_d, s5_glu_w, s5_glu_b, na_rpb, rk_mu, rk_w0, rk_w2, rk_a0, rk_a2, rk_g2, rk_k_k, rk_k_a, rk_r_k, rk_ln_w, rk_ln_b, ec_router, ec_w_gate, ec_w_up, ec_w_down):
    bsz, n, d = x.shape
    n_ctx = ctx.shape[1]
    rows = n // GRID_W
    xc = jnp.concatenate([ctx, x], axis=1)
    cos, sin = _rope_tables(n_ctx, n)

    for l in range(DEPTH):
        need_ctx = l < DEPTH - 1
        mod = jax.nn.silu(c) @ ada_w[l] + ada_b[l]
        mod_c = jnp.broadcast_to(jax.nn.silu(c_ctx) @ ada_w[l] + ada_b[l], mod.shape)
        both = jnp.stack([mod_c, mod], axis=1).reshape(bsz, 2, 6, 1, d)
        mods1 = jnp.stack([both[:, :, 1], both[:, :, 0]], axis=2)
        mods2 = jnp.stack([both[:, :, 4], both[:, :, 3]], axis=2)
        gates1 = both[:, :, 2:3]

        vec = jnp.stack([rk_k_k[l], rk_k_a[l], rk_r_k[l].reshape(-1), rk_w0[l, 0], rk_w0[l, 1],
                         rk_a0[l, 0], rk_a0[l, 1], jnp.zeros_like(rk_k_k[l])])
        u, qkv, com, dirs, g_rk, bonus = _inproj(xc, norm_g[l, 0], mods1, _bf(w_in[l]), cos, sin, rk_mu[l], vec,
                                                 _bf(rk_w2[l]), _bf(rk_a2[l]), _bf(rk_g2[l]), n_ctx)

        lam, bblk, cblk = _s5_params(s5_lam_re[l], s5_lam_im[l], s5_log_dt[l], s5_b_re[l], s5_b_im[l],
                                     s5_c_re[l], s5_c_im[l])
        sf, sb = _s5_scan(u.reshape(-1, S5_WIDTH), lam, bblk, cblk, bsz, n_ctx)

        bias, off_idx = _na_bias_table(na_rpb[l], rows)
        y_na = _na_attention(qkv, bias, off_idx, n_ctx)

        rf, rb = _rwkv_scan(com, dirs, n_ctx)

        xc, h2, aff = _mixout(u, sf.reshape(u.shape), sb.reshape(u.shape), jnp.stack([s5_d[l], s5_glu_b[l]]),
                              _bf(s5_glu_w[l]), y_na, rf, rb, g_rk, bonus, jnp.stack([rk_ln_w[l], rk_ln_b[l]]),
                              _bf(w_out[l]), norm_g[l, 1:4], gates1, xc, mods2, _bf(ec_router[l].T), n_ctx)
        wg, wu, wd = (_expert_weights_bf16(w, l) for w in (ec_w_gate, ec_w_up, ec_w_down))
        xc = _expert_choice(xc, h2, aff, wg, wu, wd, norm_g[l, 3], both[:, 1, 5, 0], n_ctx, n, n_ctx,
                            keep_rest=need_ctx)
        if need_ctx:
            xc = _expert_choice(xc, h2, aff, wg, wu, wd, norm_g[l, 3], both[:, 0, 5, 0], 0, n_ctx, n_ctx)
    return xc
```

```python
import functools
import math

import jax
import jax.numpy as jnp
import numpy as np
from jax import lax
from jax.experimental import pallas as pl
from jax.experimental.pallas import tpu as pltpu

D_MODEL = 1024
DEPTH = 2
GRID_W = 64
D_MIX = D_MODEL
HEAD_DIM = 64
S5_WIDTH = D_MIX // 4
S5_GROUP = 16
S5_GROUPS = S5_WIDTH // S5_GROUP
S5_STATE = 64
NA_WIDTH = (D_MIX - S5_WIDTH) // 2
NA_HEADS = NA_WIDTH // HEAD_DIM
NA_WIN_R = 8
NA_WIN_C = 16
ROPE_BASE = 10000.0
RK_WIDTH = D_MIX - S5_WIDTH - NA_WIDTH
RK_HEADS = RK_WIDTH // HEAD_DIM
RK_DECAY_RANK = 64
RK_A_RANK = 64
RK_GATE_RANK = 128
RK_IN_WIDTH = 3 * RK_WIDTH + 2 * RK_DECAY_RANK + 2 * RK_A_RANK + RK_GATE_RANK
RK_GN_EPS = 64e-5
N_IN = S5_WIDTH + 3 * NA_WIDTH + RK_IN_WIDTH
N_EXPERTS = 16
EC_CAPACITY_FACTOR = 2
RMS_EPS = 1e-6
NEG_INF = -1e30

VMEM_LIMIT_BYTES = 56 * 1024 * 1024
EXPERT_VMEM_LIMIT_BYTES = 62 * 1024 * 1024
TOKEN_TILE = 256
S5_GP = S5_GROUPS * S5_STATE
S5_CHUNK = 64
S5_LANES = 512
RK_CHUNK = 64
RK_SAMPLES = 4
NA_QROWS = 4
FFN_ROWS = 512
FFN_FCHUNK = 768
CAST_ROWS = 256
SEL_LANES = 256


def _compiler_params(semantics, vmem_limit_bytes=VMEM_LIMIT_BYTES):
    return pltpu.CompilerParams(dimension_semantics=semantics, vmem_limit_bytes=vmem_limit_bytes)


def _token_tile(n_ctx):
    return min(TOKEN_TILE, n_ctx)


def _mod_specs(bsz, d, n_ctx):
    cb = n_ctx // _token_tile(n_ctx)
    return [pl.BlockSpec((None, None, None, 1, d), lambda b, i, j=j: (b, jnp.minimum(i // cb, 1), j, 0, 0))
            for j in range(2)]


def _dotf(x, y):
    return jnp.dot(x, y, preferred_element_type=jnp.float32)


def _bf(x):
    return x.astype(jnp.bfloat16)


def _rope_tables(n_ctx, n):
    t = np.arange(n)
    nf = HEAD_DIM // 4
    inv_freq = ROPE_BASE ** (-np.arange(nf, dtype=np.float32) / nf)
    pos = np.stack([(t // GRID_W).astype(np.float32), (t % GRID_W).astype(np.float32)], axis=1)
    ang = pos[:, :, None] * inv_freq[None, None, :]
    cos = np.repeat(np.cos(ang), 2, axis=1).reshape(n, HEAD_DIM)
    sin = np.sin(ang)
    sin = np.stack([-sin[:, 0], sin[:, 0], -sin[:, 1], sin[:, 1]], axis=1).reshape(n, HEAD_DIM)
    cos = np.concatenate([np.ones((n_ctx, HEAD_DIM), np.float32), cos.astype(np.float32)], axis=0)
    sin = np.concatenate([np.zeros((n_ctx, HEAD_DIM), np.float32), sin.astype(np.float32)], axis=0)
    reps = 2 * NA_HEADS
    return jnp.asarray(np.tile(cos, (1, reps))), jnp.asarray(np.tile(sin, (1, reps)))


def _inproj_kernel(x_ref, xp_ref, xn_ref, g_ref, sc_ref, sh_ref, w_ref, cos_ref, sin_ref,
                   mu_ref, vec_ref, w2_ref, a2_ref, g2_ref, ones_ref,
                   u_ref, qkv_ref, com_ref, dir_ref, grk_ref, bonus_ref, *, tn, n_ctx):
    x = jnp.concatenate([xp_ref[...], x_ref[...], xn_ref[...]], axis=0)
    y = x * lax.rsqrt(jnp.mean(x * x, axis=-1, keepdims=True) + RMS_EPS)
    h_ext = _bf((y * g_ref[...]) * (1.0 + sc_ref[...]) + sh_ref[...])
    z_ext = _dotf(h_ext, w_ref[:, S5_WIDTH + 3 * NA_WIDTH:])
    t0 = pl.program_id(1) * tn
    nt = pl.num_programs(1) * tn
    keep_prev = jnp.where((t0 == 0) | (t0 == n_ctx), 0.0, 1.0)
    keep_next = jnp.where((t0 + tn == n_ctx) | (t0 + tn == nt), 0.0, 1.0)
    _rk_prepare(z_ext[8:8 + tn], z_ext[7:8] * keep_prev, z_ext[8 + tn:9 + tn] * keep_next, mu_ref, vec_ref,
                w2_ref, a2_ref, g2_ref, ones_ref, com_ref, dir_ref, grk_ref, bonus_ref)
    h = h_ext[8:8 + tn]
    u_ref[...] = _dotf(h, w_ref[:, :S5_WIDTH])
    qk = _dotf(h, w_ref[:, S5_WIDTH:S5_WIDTH + 2 * NA_WIDTH])
    nf = HEAD_DIM // 4
    lane = lax.broadcasted_iota(jnp.int32, qk.shape, 1)
    first = (lane & (2 * nf - 1)) < nf
    partner = jnp.where(first, pltpu.roll(qk, qk.shape[1] - nf, axis=1), pltpu.roll(qk, nf, axis=1))
    qk = _bf(qk * cos_ref[...] + partner * sin_ref[...])
    v = _bf(_dotf(h, w_ref[:, S5_WIDTH + 2 * NA_WIDTH:S5_WIDTH + 3 * NA_WIDTH]))
    for hd in range(NA_HEADS):
        lo = hd * HEAD_DIM
        qkv_ref[0, hd] = qk[:, lo:lo + HEAD_DIM]
        qkv_ref[1, hd] = qk[:, NA_WIDTH + lo:NA_WIDTH + lo + HEAD_DIM]
        qkv_ref[2, hd] = v[:, lo:lo + HEAD_DIM]


def _inproj(xc, g, mods, w_bf16, cos, sin, mu, vec, w2, a2, g2, n_ctx):
    bsz, nt, d = xc.shape
    tn = _token_tile(n_ctx)
    tb = tn // 8
    nb = nt // 8
    ones = _head_sum_matrix()
    mu2 = mu.reshape(1, RK_IN_WIDTH)

    def full(a):
        return pl.BlockSpec(a.shape, lambda b, i: (0,) * a.ndim)

    S = jax.ShapeDtypeStruct
    return pl.pallas_call(
        functools.partial(_inproj_kernel, tn=tn, n_ctx=n_ctx),
        grid=(bsz, nt // tn),
        in_specs=[pl.BlockSpec((None, tn, d), lambda b, i: (b, i, 0)),
                  pl.BlockSpec((None, 8, d), lambda b, i: (b, jnp.maximum(i * tb - 1, 0), 0)),
                  pl.BlockSpec((None, 8, d), lambda b, i: (b, jnp.minimum((i + 1) * tb, nb - 1), 0)),
                  pl.BlockSpec((1, d), lambda b, i: (0, 0)),
                  *_mod_specs(bsz, d, n_ctx),
                  pl.BlockSpec((d, N_IN), lambda b, i: (0, 0)),
                  pl.BlockSpec((tn, 2 * NA_WIDTH), lambda b, i: (i, 0)),
                  pl.BlockSpec((tn, 2 * NA_WIDTH), lambda b, i: (i, 0)),
                  full(mu2), full(vec), full(w2), full(a2), full(g2), full(ones)],
        out_specs=[pl.BlockSpec((tn, S5_WIDTH), lambda b, i: (i, b)),
                   pl.BlockSpec((None, 3, NA_HEADS, tn, HEAD_DIM), lambda b, i: (b, 0, 0, i, 0)),
                   pl.BlockSpec((None, 3, RK_HEADS, tn, HEAD_DIM), lambda b, i: (b, 0, 0, i, 0)),
                   pl.BlockSpec((None, 2, 3, RK_HEADS, tn, HEAD_DIM), lambda b, i: (b, 0, 0, 0, i, 0)),
                   pl.BlockSpec((None, tn, RK_WIDTH), lambda b, i: (b, i, 0)),
                   pl.BlockSpec((None, tn, RK_WIDTH), lambda b, i: (b, i, 0))],
        out_shape=[S((nt, bsz * S5_WIDTH), jnp.float32),
                   S((bsz, 3, NA_HEADS, nt, HEAD_DIM), jnp.bfloat16),
                   S((bsz, 3, RK_HEADS, nt, HEAD_DIM), jnp.float32),
                   S((bsz, 2, 3, RK_HEADS, nt, HEAD_DIM), jnp.float32),
                   S((bsz, nt, RK_WIDTH), jnp.float32), S((bsz, nt, RK_WIDTH), jnp.float32)],
        compiler_params=_compiler_params(("parallel", "parallel")),
    )(xc, xc, xc, g.reshape(1, d), mods, mods, w_bf16, cos, sin, mu2, vec, w2, a2, g2, ones)


def _s5_kernel(uf_ref, ub_ref, lam_ref, bw_ref, cw_ref, yf_ref, yb_ref, sbuf, st, *, T, B):
    c = pl.program_id(0)

    @pl.when(c == 0)
    def _():
        st[...] = jnp.zeros_like(st)

    for d, u_ref in enumerate((uf_ref, ub_ref)):
        sbuf[d] = _dotf(_bf(u_ref[...]), bw_ref[d])
    for d in range(2):
        for h in range(S5_GP // S5_LANES):
            re_sl = pl.ds(h * S5_LANES, S5_LANES)
            im_sl = pl.ds(S5_GP + h * S5_LANES, S5_LANES)
            lr = jnp.broadcast_to(lam_ref[d, 0:1, h * S5_LANES:(h + 1) * S5_LANES], (B, S5_LANES))
            li = jnp.broadcast_to(lam_ref[d, 1:2, h * S5_LANES:(h + 1) * S5_LANES], (B, S5_LANES))
            s_re, s_im = st[d, :, re_sl], st[d, :, im_sl]
            for i in range(T):
                t = i if d == 0 else T - 1 - i
                rows = pl.ds(t * B, B)
                s_re, s_im = (lr * s_re - li * s_im + sbuf[d, rows, re_sl],
                              lr * s_im + li * s_re + sbuf[d, rows, im_sl])
                sbuf[d, rows, re_sl] = s_re
                sbuf[d, rows, im_sl] = s_im
            st[d, :, re_sl] = s_re
            st[d, :, im_sl] = s_im
    for d, y_ref in enumerate((yf_ref, yb_ref)):
        y_ref[...] = _dotf(_bf(sbuf[d]), cw_ref[d])


def _backward_chunk(c, nc, ncc):
    return jnp.where(c < ncc, ncc - 1 - c, nc + ncc - 1 - c)


def _s5_scan(uf, lam, bblk, cblk, bsz, n_ctx):
    T = S5_CHUNK
    nc = uf.shape[0] // (T * bsz)
    ncc = n_ctx // T
    bw, cw = _bf(bblk), _bf(cblk)
    bmap = functools.partial(_backward_chunk, nc=nc, ncc=ncc)
    blk = (T * bsz, S5_WIDTH)

    def full(shape):
        return pl.BlockSpec(shape, lambda c: (0,) * len(shape))

    return pl.pallas_call(
        functools.partial(_s5_kernel, T=T, B=bsz),
        grid=(nc,),
        in_specs=[pl.BlockSpec(blk, lambda c: (c, 0)), pl.BlockSpec(blk, lambda c: (bmap(c), 0)),
                  full(lam.shape), full(bw.shape), full(cw.shape)],
        out_specs=[pl.BlockSpec(blk, lambda c: (c, 0)), pl.BlockSpec(blk, lambda c: (bmap(c), 0))],
        out_shape=[jax.ShapeDtypeStruct(uf.shape, jnp.float32)] * 2,
        scratch_shapes=[pltpu.VMEM((2, T * bsz, 2 * S5_GP), jnp.float32),
                        pltpu.VMEM((2, bsz, 2 * S5_GP), jnp.float32)],
        compiler_params=_compiler_params(("arbitrary",)),
    )(uf, uf, lam, bw, cw)


def _s5_params(lam_re, lam_im, log_dt, b_re, b_im, c_re, c_im):
    lams, bs, cs = [], [], []
    eye = jnp.eye(S5_GROUPS, dtype=jnp.float32)
    for d in range(2):
        dt = jnp.exp(log_dt[d])[:, None]
        mag = jnp.exp(lam_re[d] * dt)
        lb_re, lb_im = mag * jnp.cos(lam_im[d] * dt), mag * jnp.sin(lam_im[d] * dt)
        den = lam_re[d] ** 2 + lam_im[d] ** 2
        nr, ni = lb_re - 1.0, lb_im
        f_re = (nr * lam_re[d] + ni * lam_im[d]) / den
        f_im = (ni * lam_re[d] - nr * lam_im[d]) / den
        bb_re = f_re[..., None] * b_re - f_im[..., None] * b_im
        bb_im = f_re[..., None] * b_im + f_im[..., None] * b_re

        def blockdiag_in(m):
            return jnp.einsum('gph,gk->ghkp', m, eye).reshape(S5_WIDTH, S5_GP)

        def blockdiag_out(m):
            return jnp.einsum('ghp,gk->gpkh', m, eye).reshape(S5_GP, S5_WIDTH)

        bs.append(jnp.concatenate([blockdiag_in(bb_re), blockdiag_in(bb_im)], axis=1))
        cs.append(jnp.concatenate([blockdiag_out(c_re[d]), -blockdiag_out(c_im[d])], axis=0))
        lams.append(jnp.stack([lb_re.reshape(S5_GP), lb_im.reshape(S5_GP)]))
    return jnp.stack(lams), jnp.stack(bs), jnp.stack(cs)


def _na_offsets(rows):
    win_r = min(NA_WIN_R, rows)
    i = np.arange(rows)
    r0 = np.clip(i - win_r // 2, 0, rows - win_r)
    return r0 - i + NA_WIN_R - 1, win_r


def _na_bias_table(rpb, rows):
    off, win_r = _na_offsets(rows)
    offs = np.unique(off)
    qc = np.arange(GRID_W)[:, None]
    kc = np.arange(GRID_W)[None, :]
    c0 = np.clip(qc - NA_WIN_C // 2, 0, GRID_W - NA_WIN_C)
    valid = (kc >= c0) & (kc < c0 + NA_WIN_C)
    dc = np.clip(kc - qc, 1 - NA_WIN_C, NA_WIN_C - 1) + NA_WIN_C - 1
    full = jnp.where(valid[None, :, None, :], rpb[:, :, dc].transpose(0, 2, 1, 3), NEG_INF)
    full = full.reshape(rpb.shape[0], GRID_W, -1)
    tabs = [full[:, :, o * GRID_W:(o + win_r) * GRID_W] for o in offs]
    return jnp.stack(tabs, axis=1), jnp.asarray(off - offs[0], jnp.int32)


_NT_DIMS = (((1,), (1,)), ((), ()))


def _softmax_pv(s_list, v_list):
    units = range(len(s_list[0]))
    m = [functools.reduce(jnp.maximum, [jnp.max(s[u], axis=-1, keepdims=True) for s in s_list]) for u in units]
    p = [[jnp.exp(s[u] - m[u]) for u in units] for s in s_list]
    den = [sum(jnp.sum(pj[u], axis=-1, keepdims=True) for pj in p) for u in units]
    o = [sum(_dotf(_bf(pj[u]), vj[u]) for pj, vj in zip(p, v_list)) for u in units]
    return [o[u] / den[u] for u in units]


def _na_kernel(off_ref, q_ref, k_ref, v_ref, *rest, n_ctx, win_r, rows):
    bias_refs, o_ref = rest[:-1], rest[-1]
    Q = len(bias_refs)
    j = pl.program_id(1) * Q
    cb = n_ctx // GRID_W
    H = q_ref.shape[0]
    units = [(qi, h) for qi in range(Q) for h in range(H)]
    scale = HEAD_DIM ** -0.5

    def scores(qi, h, start, size):
        return lax.dot_general(q_ref[h, qi * GRID_W:(qi + 1) * GRID_W, :], k_ref[h, pl.ds(start, size), :], _NT_DIMS,
                               preferred_element_type=jnp.float32) * scale

    def store(o):
        for qi in range(Q):
            o_ref[qi * GRID_W:(qi + 1) * GRID_W, :] = jnp.concatenate(o[qi * H:(qi + 1) * H], axis=-1)

    @pl.when(j < cb)
    def _():
        s = [scores(qi, h, 0, n_ctx) for qi, h in units]
        store(_softmax_pv([s], [[v_ref[h, pl.ds(0, n_ctx), :] for _, h in units]]))

    @pl.when(j >= cb)
    def _():
        nk = win_r * GRID_W
        start = [pl.multiple_of(n_ctx + jnp.clip(j - cb + qi - win_r // 2, 0, rows - win_r) * GRID_W, GRID_W)
                 for qi in range(Q)]
        s_lat = [scores(qi, h, start[qi], nk) + bias_refs[qi][h] for qi, h in units]
        s_ctx = [scores(qi, h, 0, n_ctx) for qi, h in units]
        store(_softmax_pv([s_lat, s_ctx], [[v_ref[h, pl.ds(start[qi], nk), :] for qi, h in units],
                                           [v_ref[h, pl.ds(0, n_ctx), :] for _, h in units]]))


def _na_attention(qkv, bias, off_idx, n_ctx):
    bsz, _, H, nt, _ = qkv.shape
    rows = (nt - n_ctx) // GRID_W
    win_r = min(NA_WIN_R, rows)
    cb = n_ctx // GRID_W
    Q = NA_QROWS if (cb % NA_QROWS == 0 and rows % NA_QROWS == 0) else 1

    def bias_spec(qi):
        return pl.BlockSpec((H, None, GRID_W, win_r * GRID_W),
                            lambda b, j, off: (0, off[jnp.maximum(j * Q + qi - cb, 0)], 0, 0))

    return pl.pallas_call(
        functools.partial(_na_kernel, n_ctx=n_ctx, win_r=win_r, rows=rows),
        grid_spec=pltpu.PrefetchScalarGridSpec(
            num_scalar_prefetch=1,
            grid=(bsz, (cb + rows) // Q),
            in_specs=[pl.BlockSpec((None, None, H, Q * GRID_W, HEAD_DIM), lambda b, j, off: (b, 0, 0, j, 0)),
                      pl.BlockSpec((None, None, H, nt, HEAD_DIM), lambda b, j, off: (b, 1, 0, 0, 0)),
                      pl.BlockSpec((None, None, H, nt, HEAD_DIM), lambda b, j, off: (b, 2, 0, 0, 0)),
                      *[bias_spec(qi) for qi in range(Q)]],
            out_specs=pl.BlockSpec((None, Q * GRID_W, H * HEAD_DIM), lambda b, j, off: (b, j, 0)),
        ),
        out_shape=jax.ShapeDtypeStruct((bsz, nt, H * HEAD_DIM), jnp.float32),
        compiler_params=_compiler_params(("parallel", "arbitrary")),
    )(off_idx, qkv, qkv, qkv, *([bias] * Q))


def _head_sum_matrix():
    i = np.arange(RK_WIDTH)
    return jnp.asarray((i[:, None] // HEAD_DIM) == (i[None, :] // HEAD_DIM), jnp.bfloat16)


def _head_sums(x, ones):
    hi = x.astype(jnp.bfloat16)
    lo = (x - hi.astype(jnp.float32)).astype(jnp.bfloat16)
    return _dotf(hi, ones) + _dotf(lo, ones)


def _rk_prepare(z, z_prev, z_next, mu_ref, vec_ref, w2_ref, a2_ref, g2_ref, ones_ref,
                com_ref, dir_ref, g_ref, bonus_ref):
    tn = z.shape[0]
    row = lax.broadcasted_iota(jnp.int32, z.shape, 0)
    prev = jnp.where(row == 0, z_prev, pltpu.roll(z, 1, axis=0))
    nxt = jnp.where(row == tn - 1, z_next, pltpu.roll(z, tn - 1, axis=0))
    zs = z + (0.5 * (prev + nxt) - z) * mu_ref[...]
    W, R = RK_WIDTH, RK_DECAY_RANK
    r, k, v = zs[:, :W], zs[:, W:2 * W], zs[:, 2 * W:3 * W]
    o = 3 * W
    zw = (zs[:, o:o + R], zs[:, o + R:o + 2 * R])
    za = (zs[:, o + 2 * R:o + 3 * R], zs[:, o + 3 * R:o + 4 * R])
    zg = zs[:, o + 4 * R:]
    k_k, k_a, r_k = vec_ref[0:1, :], vec_ref[1:2, :], vec_ref[2:3, :]
    ones = ones_ref[...]
    g_ref[...] = _dotf(_bf(jax.nn.sigmoid(zg)), g2_ref[...])
    kk = k * k_k
    kk = kk * lax.rsqrt(jnp.maximum(_head_sums(kk * kk, ones), 1e-24))
    bonus = 0.0
    fields = [r, kk, v]
    for d in range(2):
        w = -jax.nn.softplus(-(vec_ref[3 + d:4 + d, :] + _dotf(_bf(jnp.tanh(zw[d])), w2_ref[d]))) - 0.5
        a = jax.nn.sigmoid(vec_ref[5 + d:6 + d, :] + _dotf(_bf(za[d]), a2_ref[d]))
        kd = k * (1.0 + (a - 1.0) * k_a)
        bonus = bonus + _head_sums(r * kd * r_k, ones)
        fields += [-jnp.exp(w), kd, kk * a]
    bonus_ref[...] = bonus * v
    for j, t in enumerate(fields):
        for h in range(RK_HEADS):
            blk = t[:, h * HEAD_DIM:(h + 1) * HEAD_DIM]
            if j < 3:
                com_ref[j, h] = blk
            else:
                dir_ref[(j - 3) // 3, (j - 3) % 3, h] = blk


def _bdot(x, y):
    return _dotf(_bf(x), _bf(y))


def _bdot_nt(x, y):
    return lax.dot_general(_bf(x), _bf(y), _NT_DIMS, preferred_element_type=jnp.float32)


def _bdot_tn(x, y):
    return lax.dot_general(_bf(x), _bf(y), (((0,), (0,)), ((), ())), preferred_element_type=jnp.float32)


def _rwkv_units(units):
    T = units[0][0].shape[0]
    U = range(len(units))
    steps = int(math.log2(T))
    ti = lax.broadcasted_iota(jnp.int32, (T, T), 0)
    si = lax.broadcasted_iota(jnp.int32, (T, T), 1)
    ti2 = lax.broadcasted_iota(jnp.int32, (T, 2 * T), 0)
    si2 = lax.broadcasted_iota(jnp.int32, (T, 2 * T), 1) & (T - 1)
    eye = (lax.broadcasted_iota(jnp.int32, (HEAD_DIM, HEAD_DIM), 0)
           == lax.broadcasted_iota(jnp.int32, (HEAD_DIM, HEAD_DIM), 1))
    masks = {}
    for rev in (False, True):
        incl = (si >= ti) if rev else (si <= ti)
        masks[rev] = (incl.astype(jnp.bfloat16), (si2 >= ti2) if rev else (si2 <= ti2),
                      (si2 > ti2) if rev else (si2 < ti2))
    cum = []
    for (r, kk, v, lw, kd, b, h0, rev) in units:
        l1 = lw.astype(jnp.bfloat16)
        rem = lw - l1.astype(jnp.float32)
        l2 = rem.astype(jnp.bfloat16)
        l3 = (rem - l2.astype(jnp.float32)).astype(jnp.bfloat16)
        tri = masks[rev][0]
        cum.append(_dotf(tri, l1) + (_dotf(tri, l2) + _dotf(tri, l3)))
    at, rt, p, cl = [], [], [], []
    for u, (r, kk, v, lw, kd, b, h0, rev) in enumerate(units):
        c = cum[u]
        cl.append(c[0:1, :] if rev else c[T - 1:T, :])
        e_neg = jnp.exp(-c)
        at.append(-kk * jnp.exp(c - lw))
        rt.append(r * jnp.exp(c))
        p.append(_bdot_nt(jnp.concatenate([at[u], rt[u]], axis=0), jnp.concatenate([b * e_neg, kd * e_neg], axis=0)))
    top = [jnp.where(masks[units[u][7]][2], p[u][:T], 0.0) for u in U]
    l2m = [jnp.where(masks[units[u][7]][1], p[u][T:], 0.0) for u in U]
    npow = [top[u][:, :T] for u in U]
    x = [jnp.concatenate([at[u], _bdot(top[u][:, T:], units[u][2])], axis=1) for u in U]
    for i in range(steps):
        x = [x[u] + _bdot(npow[u], x[u]) for u in U]
        if i < steps - 1:
            npow = [_bdot(npow[u], npow[u]) for u in U]
    z = [jnp.concatenate([x[u], jnp.concatenate([jnp.zeros_like(units[u][2]), units[u][2]], axis=1)], axis=0)
         for u in U]
    ry = [_bdot(l2m[u], z[u]) for u in U]
    gj = []
    for u, (r, kk, v, lw, kd, b, h0, rev) in enumerate(units):
        e_end = jnp.exp(cl[u] - cum[u])
        gj.append(_bdot_tn(jnp.concatenate([b * e_end, kd * e_end], axis=0), z[u]))
    out = []
    for u in U:
        g = jnp.where(eye, jnp.exp(cl[u]), 0.0) + gj[u][:, :HEAD_DIM]
        yh = _bdot(jnp.concatenate([rt[u] + ry[u][:, :HEAD_DIM], g], axis=0), units[u][6])
        out.append((yh[:T] + ry[u][:, HEAD_DIM:], yh[T:] + gj[u][:, HEAD_DIM:]))
    return out


def _rwkv_kernel(cf_ref, cb_ref, df_ref, db_ref, yf_ref, yb_ref, hs):
    c = pl.program_id(1)
    S, H = cf_ref.shape[0], cf_ref.shape[2]

    @pl.when(c == 0)
    def _():
        hs[...] = jnp.zeros_like(hs)

    units, where = [], []
    for s in range(S):
        for d, (c_ref, d_ref) in enumerate(((cf_ref, df_ref), (cb_ref, db_ref))):
            for h in range(H):
                units.append((c_ref[s, 0, h], c_ref[s, 1, h], c_ref[s, 2, h],
                              d_ref[s, 0, h], d_ref[s, 1, h], d_ref[s, 2, h], hs[s, d, h], d == 1))
                where.append((s, d, h))
    for (s, d, h), (y, hn) in zip(where, _rwkv_units(units)):
        (yf_ref, yb_ref)[d][s, h] = y
        hs[s, d, h] = hn


def _rwkv_scan(com, dirs, n_ctx):
    bsz, _, H, nt, _ = com.shape
    T = RK_CHUNK
    S = RK_SAMPLES if bsz % RK_SAMPLES == 0 else 1
    bmap = functools.partial(_backward_chunk, nc=nt // T, ncc=n_ctx // T)
    cblk = (S, 3, H, T, HEAD_DIM)
    dblk = (S, None, 3, H, T, HEAD_DIM)
    oblk = (S, H, T, HEAD_DIM)
    return pl.pallas_call(
        _rwkv_kernel,
        grid=(bsz // S, nt // T),
        in_specs=[pl.BlockSpec(cblk, lambda b, c: (b, 0, 0, c, 0)),
                  pl.BlockSpec(cblk, lambda b, c: (b, 0, 0, bmap(c), 0)),
                  pl.BlockSpec(dblk, lambda b, c: (b, 0, 0, 0, c, 0)),
                  pl.BlockSpec(dblk, lambda b, c: (b, 1, 0, 0, bmap(c), 0))],
        out_specs=[pl.BlockSpec(oblk, lambda b, c: (b, 0, c, 0)),
                   pl.BlockSpec(oblk, lambda b, c: (b, 0, bmap(c), 0))],
        out_shape=[jax.ShapeDtypeStruct((bsz, H, nt, HEAD_DIM), jnp.float32)] * 2,
        scratch_shapes=[pltpu.VMEM((S, 2, H, HEAD_DIM, HEAD_DIM), jnp.float32)],
        compiler_params=_compiler_params(("parallel", "arbitrary")),
    )(com, com, dirs, dirs)


def _mixout_kernel(u_ref, sf_ref, sb_ref, s5v_ref, gw_ref, na_ref, rf_ref, rb_ref, grk_ref, bonus_ref, ln_ref,
                   w_ref, ng_ref, gt_ref, x_ref, sc_ref, sh_ref, wrt_ref, o_ref, h_ref, aff_ref):
    y5 = jax.nn.gelu(s5v_ref[0:1, :] * u_ref[...] + sf_ref[...] + sb_ref[...])
    y5 = y5 * jax.nn.sigmoid(_dotf(_bf(y5), gw_ref[...]) + s5v_ref[1:2, :])
    outs = []
    for h in range(RK_HEADS):
        y = rf_ref[h] + rb_ref[h]
        yc = y - jnp.mean(y, axis=-1, keepdims=True)
        var = jnp.mean(yc * yc, axis=-1, keepdims=True)
        outs.append(yc * lax.rsqrt(var + RK_GN_EPS))
    yrk = (jnp.concatenate(outs, axis=-1) * ln_ref[0:1, :] + ln_ref[1:2, :] + bonus_ref[...]) * grk_ref[...]
    z = (_dotf(_bf(y5), w_ref[:S5_WIDTH, :])
         + _dotf(_bf(na_ref[...]), w_ref[S5_WIDTH:S5_WIDTH + NA_WIDTH, :])
         + _dotf(_bf(yrk), w_ref[S5_WIDTH + NA_WIDTH:, :]))
    zn = z * lax.rsqrt(jnp.mean(z * z, axis=-1, keepdims=True) + RMS_EPS)
    x = x_ref[...] + gt_ref[...] * (zn * ng_ref[0:1, :])
    o_ref[...] = x
    y = x * lax.rsqrt(jnp.mean(x * x, axis=-1, keepdims=True) + RMS_EPS)
    h = _bf((y * ng_ref[1:2, :]) * (1.0 + sc_ref[...]) + sh_ref[...])
    h_ref[...] = h
    logits = lax.dot_general(wrt_ref[...], h, _NT_DIMS, preferred_element_type=jnp.float32)
    e = jnp.exp(logits - jnp.max(logits, axis=0, keepdims=True))
    aff_ref[...] = e / jnp.sum(e, axis=0, keepdims=True)


def _mixout(u, sf, sb, s5v, glu_w, y_na, rf, rb, g_rk, bonus, ln, w_out, ng, gates, xc, mods, wrt, n_ctx):
    bsz, nt, d = xc.shape
    tn = _token_tile(n_ctx)
    cb = n_ctx // tn
    ne = wrt.shape[0]

    def tok(w):
        return pl.BlockSpec((None, tn, w), lambda b, i: (b, i, 0))

    def full(a):
        return pl.BlockSpec(a.shape, lambda b, i: (0,) * a.ndim)

    tm = pl.BlockSpec((tn, S5_WIDTH), lambda b, i: (i, b))
    yblk = pl.BlockSpec((None, RK_HEADS, tn, HEAD_DIM), lambda b, i: (b, 0, i, 0))
    return pl.pallas_call(
        _mixout_kernel,
        grid=(bsz, nt // tn),
        in_specs=[tm, tm, tm, full(s5v), full(glu_w), tok(NA_WIDTH), yblk, yblk, tok(RK_WIDTH), tok(RK_WIDTH),
                  full(ln), full(w_out), full(ng),
                  pl.BlockSpec((None, None, None, 1, d), lambda b, i: (b, jnp.minimum(i // cb, 1), 0, 0, 0)),
                  tok(d), *_mod_specs(bsz, d, n_ctx), full(wrt)],
        out_specs=[tok(d), tok(d), pl.BlockSpec((None, ne, tn), lambda b, i: (b, 0, i))],
        out_shape=[jax.ShapeDtypeStruct(xc.shape, jnp.float32), jax.ShapeDtypeStruct(xc.shape, jnp.bfloat16),
                   jax.ShapeDtypeStruct((bsz, ne, nt), jnp.float32)],
        compiler_params=_compiler_params(("parallel", "parallel")),
    )(u, sf, sb, s5v, glu_w, y_na, rf, rb, g_rk, bonus, ln, w_out, ng, gates, xc, mods, mods, wrt)


def _cast_kernel(w_ref, o_ref):
    o_ref[...] = _bf(w_ref[...])


def _expert_weights_bf16(w, l):
    _, ne, r, c = w.shape
    tr = CAST_ROWS
    return pl.pallas_call(
        _cast_kernel,
        grid=(ne, r // tr),
        in_specs=[pl.BlockSpec((None, None, tr, c), lambda e, i: (l, e, i, 0))],
        out_specs=pl.BlockSpec((None, tr, c), lambda e, i: (e, i, 0)),
        out_shape=jax.ShapeDtypeStruct((ne, r, c), jnp.bfloat16),
        compiler_params=_compiler_params(("parallel", "parallel")),
    )(w)


def _capacity(n):
    return EC_CAPACITY_FACTOR * n // N_EXPERTS


def _prefix_count(m):
    ne, n = m.shape
    s = lax.broadcasted_iota(jnp.int32, (SEL_LANES, SEL_LANES), 0)
    t = lax.broadcasted_iota(jnp.int32, (SEL_LANES, SEL_LANES), 1)
    tri = _bf(s < t)
    out, carry = [], jnp.zeros((ne, 1), jnp.float32)
    for j in range(n // SEL_LANES):
        blk = m[:, j * SEL_LANES:(j + 1) * SEL_LANES]
        out.append(_dotf(_bf(blk), tri) + carry)
        carry = carry + jnp.sum(blk, axis=1, keepdims=True)
    return jnp.concatenate(out, axis=1)


def _select_segment(a, k):
    key = pltpu.bitcast(a, jnp.int32)

    def step(i, tau):
        cand = tau | (1 << (30 - i))
        cnt = jnp.sum((key >= cand).astype(jnp.float32), axis=1, keepdims=True)
        return jnp.where(cnt >= k, cand, tau)

    tau = lax.fori_loop(0, 31, step, jnp.zeros((a.shape[0], 1), jnp.int32))
    gt = key > tau
    eq = key == tau
    need = k - jnp.sum(gt.astype(jnp.float32), axis=1, keepdims=True)
    sel = gt | (eq & (_prefix_count(eq.astype(jnp.float32)) < need))
    return jnp.where(sel, _prefix_count(sel.astype(jnp.float32)), -1.0)


def _select_kernel(a_ref, o_ref, *, segments):
    if sum(n for _, n in segments) < o_ref.shape[1]:
        o_ref[...] = jnp.full(o_ref.shape, -1.0, o_ref.dtype)
    for (t0, n) in segments:
        o_ref[:, t0:t0 + n] = _select_segment(a_ref[:, t0:t0 + n], _capacity(n))


def _select(aff, segments):
    bsz, ne, nt = aff.shape
    return pl.pallas_call(
        functools.partial(_select_kernel, segments=segments),
        grid=(bsz,),
        in_specs=[pl.BlockSpec((None, ne, nt), lambda b: (b, 0, 0))],
        out_specs=pl.BlockSpec((None, ne, nt), lambda b: (b, 0, 0)),
        out_shape=jax.ShapeDtypeStruct((bsz, ne, nt), jnp.float32),
        compiler_params=_compiler_params(("parallel",)),
    )(aff)


def _expert_kernel(slot_ref, aff_ref, h_ref, wg_ref, wu_ref, wd_ref, y_ref, *, G, cap, n, t_in):
    e = pl.program_id(0)
    t0 = t_in
    want = lax.broadcasted_iota(jnp.int32, (cap, n), 0).astype(jnp.float32)
    hit = [slot_ref[s, pl.ds(e, 1), t0:t0 + n] == want for s in range(G)]
    xin = _bf(jnp.concatenate([_dotf(_bf(hit[s]), h_ref[s, pl.ds(t_in, n), :]) for s in range(G)], axis=0))
    acc = jnp.zeros((G * cap, wd_ref.shape[1]), jnp.float32)
    for c0 in range(0, wg_ref.shape[1], FFN_FCHUNK):
        cols = slice(c0, min(c0 + FFN_FCHUNK, wg_ref.shape[1]))
        hid = jax.nn.silu(_dotf(xin, wg_ref[:, cols])) * _dotf(xin, wu_ref[:, cols])
        acc = acc + _dotf(_bf(hid), wd_ref[cols, :])
    gate = jnp.concatenate([jnp.sum(jnp.where(hit[s], aff_ref[s, pl.ds(e, 1), t0:t0 + n], 0.0), axis=1, keepdims=True)
                            for s in range(G)], axis=0)
    y_ref[...] = _bf(acc * gate)


def _experts(slots, aff, h2, wg, wu, wd, t0, n):
    bsz, ne, nt = slots.shape
    d = h2.shape[2]
    f = wg.shape[2]
    cap = _capacity(n)
    G = max(1, min(bsz, FFN_ROWS // cap))
    if t0 % n == 0:
        hspec = pl.BlockSpec((G, n, d), lambda e, b: (b, t0 // n, 0))
        sspec = pl.BlockSpec((G, ne, n), lambda e, b: (b, 0, t0 // n))
        t_in = 0
    else:
        hspec = pl.BlockSpec((G, nt, d), lambda e, b: (b, 0, 0))
        sspec = pl.BlockSpec((G, ne, nt), lambda e, b: (b, 0, 0))
        t_in = t0

    def wspec(shape):
        return pl.BlockSpec((None,) + shape, lambda e, b: (e, 0, 0))

    return pl.pallas_call(
        functools.partial(_expert_kernel, G=G, cap=cap, n=n, t_in=t_in),
        grid=(ne, bsz // G),
        in_specs=[sspec, sspec, hspec, wspec((d, f)), wspec((d, f)), wspec((f, d))],
        out_specs=pl.BlockSpec((None, None, G * cap, d), lambda e, b: (b, e, 0, 0)),
        out_shape=jax.ShapeDtypeStruct((bsz // G, ne, G * cap, d), jnp.bfloat16),
        compiler_params=_compiler_params(("arbitrary", "arbitrary"), EXPERT_VMEM_LIMIT_BYTES),
    )(slots, aff, h2, wg, wu, wd)


def _combine_kernel(slot_ref, y_ref, x_ref, g_ref, gt_ref, o_ref, *, t_lo, t_hi):
    i = pl.program_id(1)
    inside = (i >= t_lo) & (i < t_hi)
    ne, cap, d = y_ref.shape

    @pl.when(inside)
    def _():
        want = lax.broadcasted_iota(jnp.int32, (x_ref.shape[0], cap), 1).astype(jnp.float32)
        hit = jnp.concatenate([_bf(slot_ref[:, e:e + 1] == want) for e in range(ne)], axis=1)
        f = _dotf(hit, y_ref[...].reshape(ne * cap, d))
        fn = f * lax.rsqrt(jnp.mean(f * f, axis=-1, keepdims=True) + RMS_EPS)
        o_ref[...] = x_ref[...] + gt_ref[...] * (fn * g_ref[...])

    @pl.when(jnp.logical_not(inside))
    def _():
        o_ref[...] = x_ref[...]


def _combine(slots_t, y, xc, g, gt, t0, n, n_ctx, keep_rest):
    bsz, nt, ne = slots_t.shape
    d = xc.shape[2]
    cap = _capacity(n)
    G = bsz // y.shape[0]
    yb = y.reshape(bsz // G, ne, G, cap, d)
    tn = _token_tile(n_ctx)
    t_lo, t_hi = t0 // tn, (t0 + n) // tn
    first, tiles = (0, nt // tn) if keep_rest else (t_lo, t_hi - t_lo)
    return pl.pallas_call(
        functools.partial(_combine_kernel, t_lo=t_lo - first, t_hi=t_hi - first),
        grid=(bsz, tiles),
        in_specs=[pl.BlockSpec((None, tn, ne), lambda b, i: (b, i + first, 0)),
                  pl.BlockSpec((None, ne, None, cap, d), lambda b, i: (b // G, 0, b % G, 0, 0)),
                  pl.BlockSpec((None, tn, d), lambda b, i: (b, i + first, 0)),
                  pl.BlockSpec((1, d), lambda b, i: (0, 0)),
                  pl.BlockSpec((None, 1, d), lambda b, i: (b, 0, 0))],
        out_specs=pl.BlockSpec((None, tn, d), lambda b, i: (b, i, 0)),
        out_shape=jax.ShapeDtypeStruct((bsz, tiles * tn, d), jnp.float32),
        compiler_params=_compiler_params(("parallel", "arbitrary")),
    )(slots_t, yb, xc, g.reshape(1, d), gt.reshape(bsz, 1, d))


def _expert_choice(xc, h2, aff, slots, slots_t, wg, wu, wd, g, gt, t0, n, n_ctx, keep_rest=True):
    y = _experts(slots, aff, h2, wg, wu, wd, t0, n)
    return _combine(slots_t, y, xc, g, gt, t0, n, n_ctx, keep_rest)


def kernel(x, c, ctx, c_ctx, ada_w, ada_b, norm_g, w_in, w_out, s5_lam_re, s5_lam_im, s5_log_dt, s5_b_re, s5_b_im, s5_c_re, s5_c_im, s5_d, s5_glu_w, s5_glu_b, na_rpb, rk_mu, rk_w0, rk_w2, rk_a0, rk_a2, rk_g2, rk_k_k, rk_k_a, rk_r_k, rk_ln_w, rk_ln_b, ec_router, ec_w_gate, ec_w_up, ec_w_down):
    bsz, n, d = x.shape
    n_ctx = ctx.shape[1]
    rows = n // GRID_W
    xc = jnp.concatenate([ctx, x], axis=1)
    cos, sin = _rope_tables(n_ctx, n)

    for l in range(DEPTH):
        need_ctx = l < DEPTH - 1
        mod = jax.nn.silu(c) @ ada_w[l] + ada_b[l]
        mod_c = jnp.broadcast_to(jax.nn.silu(c_ctx) @ ada_w[l] + ada_b[l], mod.shape)
        both = jnp.stack([mod_c, mod], axis=1).reshape(bsz, 2, 6, 1, d)
        mods1 = jnp.stack([both[:, :, 1], both[:, :, 0]], axis=2)
        mods2 = jnp.stack([both[:, :, 4], both[:, :, 3]], axis=2)
        gates1 = both[:, :, 2:3]

        vec = jnp.stack([rk_k_k[l], rk_k_a[l], rk_r_k[l].reshape(-1), rk_w0[l, 0], rk_w0[l, 1],
                         rk_a0[l, 0], rk_a0[l, 1], jnp.zeros_like(rk_k_k[l])])
        u, qkv, com, dirs, g_rk, bonus = _inproj(xc, norm_g[l, 0], mods1, _bf(w_in[l]), cos, sin, rk_mu[l], vec,
                                                 _bf(rk_w2[l]), _bf(rk_a2[l]), _bf(rk_g2[l]), n_ctx)

        lam, bblk, cblk = _s5_params(s5_lam_re[l], s5_lam_im[l], s5_log_dt[l], s5_b_re[l], s5_b_im[l],
                                     s5_c_re[l], s5_c_im[l])
        sf, sb = _s5_scan(u.reshape(-1, S5_WIDTH), lam, bblk, cblk, bsz, n_ctx)

        bias, off_idx = _na_bias_table(na_rpb[l], rows)
        y_na = _na_attention(qkv, bias, off_idx, n_ctx)

        rf, rb = _rwkv_scan(com, dirs, n_ctx)

        xc, h2, aff = _mixout(u, sf.reshape(u.shape), sb.reshape(u.shape), jnp.stack([s5_d[l], s5_glu_b[l]]),
                              _bf(s5_glu_w[l]), y_na, rf, rb, g_rk, bonus, jnp.stack([rk_ln_w[l], rk_ln_b[l]]),
                              _bf(w_out[l]), norm_g[l, 1:4], gates1, xc, mods2, _bf(ec_router[l].T), n_ctx)
        wg, wu, wd = (_expert_weights_bf16(w, l) for w in (ec_w_gate, ec_w_up, ec_w_down))
        slots = _select(aff, ((n_ctx, n), (0, n_ctx)) if need_ctx else ((n_ctx, n),))
        slots_t = jnp.swapaxes(slots, 1, 2)
        xc = _expert_choice(xc, h2, aff, slots, slots_t, wg, wu, wd, norm_g[l, 3], both[:, 1, 5, 0], n_ctx, n, n_ctx,
                            keep_rest=need_ctx)
        if need_ctx:
            xc = _expert_choice(xc, h2, aff, slots, slots_t, wg, wu, wd, norm_g[l, 3], both[:, 0, 5, 0], 0, n_ctx,
                                n_ctx)
    return xc
```

```python
import functools
import math

import jax
import jax.numpy as jnp
import numpy as np
from jax import lax
from jax.experimental import pallas as pl
from jax.experimental.pallas import tpu as pltpu

D_MODEL = 1024
DEPTH = 2
GRID_W = 64
D_MIX = D_MODEL
HEAD_DIM = 64
S5_WIDTH = D_MIX // 4
S5_GROUP = 16
S5_GROUPS = S5_WIDTH // S5_GROUP
S5_STATE = 64
NA_WIDTH = (D_MIX - S5_WIDTH) // 2
NA_HEADS = NA_WIDTH // HEAD_DIM
NA_WIN_R = 8
NA_WIN_C = 16
ROPE_BASE = 10000.0
RK_WIDTH = D_MIX - S5_WIDTH - NA_WIDTH
RK_HEADS = RK_WIDTH // HEAD_DIM
RK_DECAY_RANK = 64
RK_A_RANK = 64
RK_GATE_RANK = 128
RK_IN_WIDTH = 3 * RK_WIDTH + 2 * RK_DECAY_RANK + 2 * RK_A_RANK + RK_GATE_RANK
RK_GN_EPS = 64e-5
N_IN = S5_WIDTH + 3 * NA_WIDTH + RK_IN_WIDTH
N_EXPERTS = 16
EC_CAPACITY_FACTOR = 2
RMS_EPS = 1e-6
NEG_INF = -1e30

VMEM_LIMIT_BYTES = 56 * 1024 * 1024
EXPERT_VMEM_LIMIT_BYTES = 62 * 1024 * 1024
TOKEN_TILE = 256
S5_GP = S5_GROUPS * S5_STATE
S5_CHUNK = 64
S5_LANES = 512
RK_CHUNK = 64
RK_SAMPLES = 4
NA_QROWS = 4
FFN_ROWS = 512
FFN_FCHUNK = 768
CAST_ELEMS = 1024 * 1408
SEL_LANES = 256


def _compiler_params(semantics, vmem_limit_bytes=VMEM_LIMIT_BYTES):
    return pltpu.CompilerParams(dimension_semantics=semantics, vmem_limit_bytes=vmem_limit_bytes)


def _token_tile(n_ctx):
    return min(TOKEN_TILE, n_ctx)


def _mod_specs(bsz, d, n_ctx):
    cb = n_ctx // _token_tile(n_ctx)
    return [pl.BlockSpec((None, None, None, 1, d), lambda b, i, j=j: (b, jnp.minimum(i // cb, 1), j, 0, 0))
            for j in range(2)]


def _dotf(x, y):
    return jnp.dot(x, y, preferred_element_type=jnp.float32)


def _bf(x):
    return x.astype(jnp.bfloat16)


def _rope_tables(n_ctx, n):
    t = np.arange(n)
    nf = HEAD_DIM // 4
    inv_freq = ROPE_BASE ** (-np.arange(nf, dtype=np.float32) / nf)
    pos = np.stack([(t // GRID_W).astype(np.float32), (t % GRID_W).astype(np.float32)], axis=1)
    ang = pos[:, :, None] * inv_freq[None, None, :]
    cos = np.repeat(np.cos(ang), 2, axis=1).reshape(n, HEAD_DIM)
    sin = np.sin(ang)
    sin = np.stack([-sin[:, 0], sin[:, 0], -sin[:, 1], sin[:, 1]], axis=1).reshape(n, HEAD_DIM)
    cos = np.concatenate([np.ones((n_ctx, HEAD_DIM), np.float32), cos.astype(np.float32)], axis=0)
    sin = np.concatenate([np.zeros((n_ctx, HEAD_DIM), np.float32), sin.astype(np.float32)], axis=0)
    reps = 2 * NA_HEADS
    return jnp.asarray(np.tile(cos, (1, reps))), jnp.asarray(np.tile(sin, (1, reps)))


def _inproj_kernel(x_ref, xp_ref, xn_ref, g_ref, sc_ref, sh_ref, w_ref, cos_ref, sin_ref,
                   mu_ref, vec_ref, w2_ref, a2_ref, g2_ref, ones_ref,
                   u_ref, qkv_ref, com_ref, dir_ref, grk_ref, bonus_ref, *, tn, n_ctx):
    x = jnp.concatenate([xp_ref[...], x_ref[...], xn_ref[...]], axis=0)
    y = x * lax.rsqrt(jnp.mean(x * x, axis=-1, keepdims=True) + RMS_EPS)
    h_ext = _bf((y * g_ref[...]) * (1.0 + sc_ref[...]) + sh_ref[...])
    z_ext = _dotf(h_ext, w_ref[:, S5_WIDTH + 3 * NA_WIDTH:])
    t0 = pl.program_id(1) * tn
    nt = pl.num_programs(1) * tn
    keep_prev = jnp.where((t0 == 0) | (t0 == n_ctx), 0.0, 1.0)
    keep_next = jnp.where((t0 + tn == n_ctx) | (t0 + tn == nt), 0.0, 1.0)
    _rk_prepare(z_ext[8:8 + tn], z_ext[7:8] * keep_prev, z_ext[8 + tn:9 + tn] * keep_next, mu_ref, vec_ref,
                w2_ref, a2_ref, g2_ref, ones_ref, com_ref, dir_ref, grk_ref, bonus_ref)
    h = h_ext[8:8 + tn]
    u_ref[...] = _dotf(h, w_ref[:, :S5_WIDTH])
    qk = _dotf(h, w_ref[:, S5_WIDTH:S5_WIDTH + 2 * NA_WIDTH])
    nf = HEAD_DIM // 4
    lane = lax.broadcasted_iota(jnp.int32, qk.shape, 1)
    first = (lane & (2 * nf - 1)) < nf
    partner = jnp.where(first, pltpu.roll(qk, qk.shape[1] - nf, axis=1), pltpu.roll(qk, nf, axis=1))
    qk = _bf(qk * cos_ref[...] + partner * sin_ref[...])
    v = _bf(_dotf(h, w_ref[:, S5_WIDTH + 2 * NA_WIDTH:S5_WIDTH + 3 * NA_WIDTH]))
    for hd in range(NA_HEADS):
        lo = hd * HEAD_DIM
        qkv_ref[0, hd] = qk[:, lo:lo + HEAD_DIM]
        qkv_ref[1, hd] = qk[:, NA_WIDTH + lo:NA_WIDTH + lo + HEAD_DIM]
        qkv_ref[2, hd] = v[:, lo:lo + HEAD_DIM]


def _inproj(xc, g, mods, w_bf16, cos, sin, mu, vec, w2, a2, g2, n_ctx):
    bsz, nt, d = xc.shape
    tn = _token_tile(n_ctx)
    tb = tn // 8
    nb = nt // 8
    ones = _head_sum_matrix()
    mu2 = mu.reshape(1, RK_IN_WIDTH)

    def full(a):
        return pl.BlockSpec(a.shape, lambda b, i: (0,) * a.ndim)

    S = jax.ShapeDtypeStruct
    return pl.pallas_call(
        functools.partial(_inproj_kernel, tn=tn, n_ctx=n_ctx),
        grid=(bsz, nt // tn),
        in_specs=[pl.BlockSpec((None, tn, d), lambda b, i: (b, i, 0)),
                  pl.BlockSpec((None, 8, d), lambda b, i: (b, jnp.maximum(i * tb - 1, 0), 0)),
                  pl.BlockSpec((None, 8, d), lambda b, i: (b, jnp.minimum((i + 1) * tb, nb - 1), 0)),
                  pl.BlockSpec((1, d), lambda b, i: (0, 0)),
                  *_mod_specs(bsz, d, n_ctx),
                  pl.BlockSpec((d, N_IN), lambda b, i: (0, 0)),
                  pl.BlockSpec((tn, 2 * NA_WIDTH), lambda b, i: (i, 0)),
                  pl.BlockSpec((tn, 2 * NA_WIDTH), lambda b, i: (i, 0)),
                  full(mu2), full(vec), full(w2), full(a2), full(g2), full(ones)],
        out_specs=[pl.BlockSpec((tn, S5_WIDTH), lambda b, i: (i, b)),
                   pl.BlockSpec((None, 3, NA_HEADS, tn, HEAD_DIM), lambda b, i: (b, 0, 0, i, 0)),
                   pl.BlockSpec((None, 3, RK_HEADS, tn, HEAD_DIM), lambda b, i: (b, 0, 0, i, 0)),
                   pl.BlockSpec((None, 2, 3, RK_HEADS, tn, HEAD_DIM), lambda b, i: (b, 0, 0, 0, i, 0)),
                   pl.BlockSpec((None, tn, RK_WIDTH), lambda b, i: (b, i, 0)),
                   pl.BlockSpec((None, tn, RK_WIDTH), lambda b, i: (b, i, 0))],
        out_shape=[S((nt, bsz * S5_WIDTH), jnp.float32),
                   S((bsz, 3, NA_HEADS, nt, HEAD_DIM), jnp.bfloat16),
                   S((bsz, 3, RK_HEADS, nt, HEAD_DIM), jnp.float32),
                   S((bsz, 2, 3, RK_HEADS, nt, HEAD_DIM), jnp.float32),
                   S((bsz, nt, RK_WIDTH), jnp.float32), S((bsz, nt, RK_WIDTH), jnp.float32)],
        compiler_params=_compiler_params(("parallel", "parallel")),
    )(xc, xc, xc, g.reshape(1, d), mods, mods, w_bf16, cos, sin, mu2, vec, w2, a2, g2, ones)


def _s5_kernel(uf_ref, ub_ref, lam_ref, bw_ref, cw_ref, yf_ref, yb_ref, sbuf, st, *, T, B):
    c = pl.program_id(0)

    @pl.when(c == 0)
    def _():
        st[...] = jnp.zeros_like(st)

    for d, u_ref in enumerate((uf_ref, ub_ref)):
        sbuf[d] = _dotf(_bf(u_ref[...].reshape(T * B, S5_WIDTH)), bw_ref[d])
    for d in range(2):
        for h in range(S5_GP // S5_LANES):
            re_sl = pl.ds(h * S5_LANES, S5_LANES)
            im_sl = pl.ds(S5_GP + h * S5_LANES, S5_LANES)
            lr = jnp.broadcast_to(lam_ref[d, 0:1, h * S5_LANES:(h + 1) * S5_LANES], (B, S5_LANES))
            li = jnp.broadcast_to(lam_ref[d, 1:2, h * S5_LANES:(h + 1) * S5_LANES], (B, S5_LANES))
            s_re, s_im = st[d, :, re_sl], st[d, :, im_sl]
            for i in range(T):
                t = i if d == 0 else T - 1 - i
                rows = pl.ds(t * B, B)
                s_re, s_im = (lr * s_re - li * s_im + sbuf[d, rows, re_sl],
                              lr * s_im + li * s_re + sbuf[d, rows, im_sl])
                sbuf[d, rows, re_sl] = s_re
                sbuf[d, rows, im_sl] = s_im
            st[d, :, re_sl] = s_re
            st[d, :, im_sl] = s_im
    for d, y_ref in enumerate((yf_ref, yb_ref)):
        y_ref[...] = _dotf(_bf(sbuf[d]), cw_ref[d]).reshape(T, B * S5_WIDTH)


def _backward_chunk(c, nc, ncc):
    return jnp.where(c < ncc, ncc - 1 - c, nc + ncc - 1 - c)


def _s5_scan(uf, lam, bblk, cblk, bsz, n_ctx):
    T = S5_CHUNK
    nc = uf.shape[0] // T
    ncc = n_ctx // T
    bw, cw = _bf(bblk), _bf(cblk)
    bmap = functools.partial(_backward_chunk, nc=nc, ncc=ncc)
    blk = (T, bsz * S5_WIDTH)

    def full(shape):
        return pl.BlockSpec(shape, lambda c: (0,) * len(shape))

    return pl.pallas_call(
        functools.partial(_s5_kernel, T=T, B=bsz),
        grid=(nc,),
        in_specs=[pl.BlockSpec(blk, lambda c: (c, 0)), pl.BlockSpec(blk, lambda c: (bmap(c), 0)),
                  full(lam.shape), full(bw.shape), full(cw.shape)],
        out_specs=[pl.BlockSpec(blk, lambda c: (c, 0)), pl.BlockSpec(blk, lambda c: (bmap(c), 0))],
        out_shape=[jax.ShapeDtypeStruct(uf.shape, jnp.float32)] * 2,
        scratch_shapes=[pltpu.VMEM((2, T * bsz, 2 * S5_GP), jnp.float32),
                        pltpu.VMEM((2, bsz, 2 * S5_GP), jnp.float32)],
        compiler_params=_compiler_params(("arbitrary",)),
    )(uf, uf, lam, bw, cw)


def _s5_params(lam_re, lam_im, log_dt, b_re, b_im, c_re, c_im):
    lams, bs, cs = [], [], []
    eye = jnp.eye(S5_GROUPS, dtype=jnp.float32)
    for d in range(2):
        dt = jnp.exp(log_dt[d])[:, None]
        mag = jnp.exp(lam_re[d] * dt)
        lb_re, lb_im = mag * jnp.cos(lam_im[d] * dt), mag * jnp.sin(lam_im[d] * dt)
        den = lam_re[d] ** 2 + lam_im[d] ** 2
        nr, ni = lb_re - 1.0, lb_im
        f_re = (nr * lam_re[d] + ni * lam_im[d]) / den
        f_im = (ni * lam_re[d] - nr * lam_im[d]) / den
        bb_re = f_re[..., None] * b_re - f_im[..., None] * b_im
        bb_im = f_re[..., None] * b_im + f_im[..., None] * b_re

        def blockdiag_in(m):
            return jnp.einsum('gph,gk->ghkp', m, eye).reshape(S5_WIDTH, S5_GP)

        def blockdiag_out(m):
            return jnp.einsum('ghp,gk->gpkh', m, eye).reshape(S5_GP, S5_WIDTH)

        bs.append(jnp.concatenate([blockdiag_in(bb_re), blockdiag_in(bb_im)], axis=1))
        cs.append(jnp.concatenate([blockdiag_out(c_re[d]), -blockdiag_out(c_im[d])], axis=0))
        lams.append(jnp.stack([lb_re.reshape(S5_GP), lb_im.reshape(S5_GP)]))
    return jnp.stack(lams), jnp.stack(bs), jnp.stack(cs)


def _na_offsets(rows):
    win_r = min(NA_WIN_R, rows)
    i = np.arange(rows)
    r0 = np.clip(i - win_r // 2, 0, rows - win_r)
    return r0 - i + NA_WIN_R - 1, win_r


def _na_bias_table(rpb, rows):
    off, win_r = _na_offsets(rows)
    offs = np.unique(off)
    qc = np.arange(GRID_W)[:, None]
    kc = np.arange(GRID_W)[None, :]
    c0 = np.clip(qc - NA_WIN_C // 2, 0, GRID_W - NA_WIN_C)
    valid = (kc >= c0) & (kc < c0 + NA_WIN_C)
    dc = np.clip(kc - qc, 1 - NA_WIN_C, NA_WIN_C - 1) + NA_WIN_C - 1
    full = jnp.where(valid[None, :, None, :], rpb[:, :, dc].transpose(0, 2, 1, 3), NEG_INF)
    full = full.reshape(rpb.shape[0], GRID_W, -1)
    tabs = [full[:, :, o * GRID_W:(o + win_r) * GRID_W] for o in offs]
    return jnp.stack(tabs, axis=1), jnp.asarray(off - offs[0], jnp.int32)


_NT_DIMS = (((1,), (1,)), ((), ()))


def _softmax_pv(s_list, v_list):
    units = range(len(s_list[0]))
    m = [functools.reduce(jnp.maximum, [jnp.max(s[u], axis=-1, keepdims=True) for s in s_list]) for u in units]
    p = [[jnp.exp(s[u] - m[u]) for u in units] for s in s_list]
    den = [sum(jnp.sum(pj[u], axis=-1, keepdims=True) for pj in p) for u in units]
    o = [sum(_dotf(_bf(pj[u]), vj[u]) for pj, vj in zip(p, v_list)) for u in units]
    return [o[u] / den[u] for u in units]


def _na_kernel(off_ref, q_ref, k_ref, v_ref, *rest, n_ctx, win_r, rows):
    bias_refs, o_ref = rest[:-1], rest[-1]
    Q = len(bias_refs)
    j = pl.program_id(1) * Q
    cb = n_ctx // GRID_W
    H = q_ref.shape[0]
    units = [(qi, h) for qi in range(Q) for h in range(H)]
    scale = HEAD_DIM ** -0.5

    def scores(qi, h, start, size):
        return lax.dot_general(q_ref[h, qi * GRID_W:(qi + 1) * GRID_W, :], k_ref[h, pl.ds(start, size), :], _NT_DIMS,
                               preferred_element_type=jnp.float32) * scale

    def store(o):
        for qi in range(Q):
            o_ref[qi * GRID_W:(qi + 1) * GRID_W, :] = jnp.concatenate(o[qi * H:(qi + 1) * H], axis=-1)

    @pl.when(j < cb)
    def _():
        s = [scores(qi, h, 0, n_ctx) for qi, h in units]
        store(_softmax_pv([s], [[v_ref[h, pl.ds(0, n_ctx), :] for _, h in units]]))

    @pl.when(j >= cb)
    def _():
        nk = win_r * GRID_W
        start = [pl.multiple_of(n_ctx + jnp.clip(j - cb + qi - win_r // 2, 0, rows - win_r) * GRID_W, GRID_W)
                 for qi in range(Q)]
        s_lat = [scores(qi, h, start[qi], nk) + bias_refs[qi][h] for qi, h in units]
        s_ctx = [scores(qi, h, 0, n_ctx) for qi, h in units]
        store(_softmax_pv([s_lat, s_ctx], [[v_ref[h, pl.ds(start[qi], nk), :] for qi, h in units],
                                           [v_ref[h, pl.ds(0, n_ctx), :] for _, h in units]]))


def _na_attention(qkv, bias, off_idx, n_ctx):
    bsz, _, H, nt, _ = qkv.shape
    rows = (nt - n_ctx) // GRID_W
    win_r = min(NA_WIN_R, rows)
    cb = n_ctx // GRID_W
    Q = NA_QROWS if (cb % NA_QROWS == 0 and rows % NA_QROWS == 0) else 1

    def bias_spec(qi):
        return pl.BlockSpec((H, None, GRID_W, win_r * GRID_W),
                            lambda b, j, off: (0, off[jnp.maximum(j * Q + qi - cb, 0)], 0, 0))

    return pl.pallas_call(
        functools.partial(_na_kernel, n_ctx=n_ctx, win_r=win_r, rows=rows),
        grid_spec=pltpu.PrefetchScalarGridSpec(
            num_scalar_prefetch=1,
            grid=(bsz, (cb + rows) // Q),
            in_specs=[pl.BlockSpec((None, None, H, Q * GRID_W, HEAD_DIM), lambda b, j, off: (b, 0, 0, j, 0)),
                      pl.BlockSpec((None, None, H, nt, HEAD_DIM), lambda b, j, off: (b, 1, 0, 0, 0)),
                      pl.BlockSpec((None, None, H, nt, HEAD_DIM), lambda b, j, off: (b, 2, 0, 0, 0)),
                      *[bias_spec(qi) for qi in range(Q)]],
            out_specs=pl.BlockSpec((None, Q * GRID_W, H * HEAD_DIM), lambda b, j, off: (b, j, 0)),
        ),
        out_shape=jax.ShapeDtypeStruct((bsz, nt, H * HEAD_DIM), jnp.float32),
        compiler_params=_compiler_params(("parallel", "arbitrary")),
    )(off_idx, qkv, qkv, qkv, *([bias] * Q))


def _head_sum_matrix():
    i = np.arange(RK_WIDTH)
    return jnp.asarray((i[:, None] // HEAD_DIM) == (i[None, :] // HEAD_DIM), jnp.bfloat16)


def _head_sums(x, ones):
    hi = x.astype(jnp.bfloat16)
    lo = (x - hi.astype(jnp.float32)).astype(jnp.bfloat16)
    return _dotf(hi, ones) + _dotf(lo, ones)


def _rk_prepare(z, z_prev, z_next, mu_ref, vec_ref, w2_ref, a2_ref, g2_ref, ones_ref,
                com_ref, dir_ref, g_ref, bonus_ref):
    tn = z.shape[0]
    row = lax.broadcasted_iota(jnp.int32, z.shape, 0)
    prev = jnp.where(row == 0, z_prev, pltpu.roll(z, 1, axis=0))
    nxt = jnp.where(row == tn - 1, z_next, pltpu.roll(z, tn - 1, axis=0))
    zs = z + (0.5 * (prev + nxt) - z) * mu_ref[...]
    W, R = RK_WIDTH, RK_DECAY_RANK
    r, k, v = zs[:, :W], zs[:, W:2 * W], zs[:, 2 * W:3 * W]
    o = 3 * W
    zw = (zs[:, o:o + R], zs[:, o + R:o + 2 * R])
    za = (zs[:, o + 2 * R:o + 3 * R], zs[:, o + 3 * R:o + 4 * R])
    zg = zs[:, o + 4 * R:]
    k_k, k_a, r_k = vec_ref[0:1, :], vec_ref[1:2, :], vec_ref[2:3, :]
    ones = ones_ref[...]
    g_ref[...] = _dotf(_bf(jax.nn.sigmoid(zg)), g2_ref[...])
    kk = k * k_k
    kk = kk * lax.rsqrt(jnp.maximum(_head_sums(kk * kk, ones), 1e-24))
    bonus = 0.0
    fields = [r, kk, v]
    for d in range(2):
        w = -jax.nn.softplus(-(vec_ref[3 + d:4 + d, :] + _dotf(_bf(jnp.tanh(zw[d])), w2_ref[d]))) - 0.5
        a = jax.nn.sigmoid(vec_ref[5 + d:6 + d, :] + _dotf(_bf(za[d]), a2_ref[d]))
        kd = k * (1.0 + (a - 1.0) * k_a)
        bonus = bonus + _head_sums(r * kd * r_k, ones)
        fields += [-jnp.exp(w), kd, kk * a]
    bonus_ref[...] = bonus * v
    for j, t in enumerate(fields):
        for h in range(RK_HEADS):
            blk = t[:, h * HEAD_DIM:(h + 1) * HEAD_DIM]
            if j < 3:
                com_ref[j, h] = blk
            else:
                dir_ref[(j - 3) // 3, (j - 3) % 3, h] = blk


def _bdot(x, y):
    return _dotf(_bf(x), _bf(y))


def _bdot_nt(x, y):
    return lax.dot_general(_bf(x), _bf(y), _NT_DIMS, preferred_element_type=jnp.float32)


def _bdot_tn(x, y):
    return lax.dot_general(_bf(x), _bf(y), (((0,), (0,)), ((), ())), preferred_element_type=jnp.float32)


def _rwkv_units(units):
    T = units[0][0].shape[0]
    U = range(len(units))
    steps = int(math.log2(T))
    ti = lax.broadcasted_iota(jnp.int32, (T, T), 0)
    si = lax.broadcasted_iota(jnp.int32, (T, T), 1)
    ti2 = lax.broadcasted_iota(jnp.int32, (T, 2 * T), 0)
    si2 = lax.broadcasted_iota(jnp.int32, (T, 2 * T), 1) & (T - 1)
    eye = (lax.broadcasted_iota(jnp.int32, (HEAD_DIM, HEAD_DIM), 0)
           == lax.broadcasted_iota(jnp.int32, (HEAD_DIM, HEAD_DIM), 1))
    masks = {}
    for rev in (False, True):
        incl = (si >= ti) if rev else (si <= ti)
        masks[rev] = (incl.astype(jnp.bfloat16), (si2 >= ti2) if rev else (si2 <= ti2),
                      (si2 > ti2) if rev else (si2 < ti2))
    cum = []
    for (r, kk, v, lw, kd, b, h0, rev) in units:
        l1 = lw.astype(jnp.bfloat16)
        rem = lw - l1.astype(jnp.float32)
        l2 = rem.astype(jnp.bfloat16)
        l3 = (rem - l2.astype(jnp.float32)).astype(jnp.bfloat16)
        tri = masks[rev][0]
        cum.append(_dotf(tri, l1) + (_dotf(tri, l2) + _dotf(tri, l3)))
    at, rt, p, cl = [], [], [], []
    for u, (r, kk, v, lw, kd, b, h0, rev) in enumerate(units):
        c = cum[u]
        cl.append(c[0:1, :] if rev else c[T - 1:T, :])
        e_neg = jnp.exp(-c)
        at.append(-kk * jnp.exp(c - lw))
        rt.append(r * jnp.exp(c))
        p.append(_bdot_nt(jnp.concatenate([at[u], rt[u]], axis=0), jnp.concatenate([b * e_neg, kd * e_neg], axis=0)))
    top = [jnp.where(masks[units[u][7]][2], p[u][:T], 0.0) for u in U]
    l2m = [jnp.where(masks[units[u][7]][1], p[u][T:], 0.0) for u in U]
    npow = [top[u][:, :T] for u in U]
    x = [jnp.concatenate([at[u], _bdot(top[u][:, T:], units[u][2])], axis=1) for u in U]
    for i in range(steps):
        x = [x[u] + _bdot(npow[u], x[u]) for u in U]
        if i < steps - 1:
            npow = [_bdot(npow[u], npow[u]) for u in U]
    z = [jnp.concatenate([x[u], jnp.concatenate([jnp.zeros_like(units[u][2]), units[u][2]], axis=1)], axis=0)
         for u in U]
    ry = [_bdot(l2m[u], z[u]) for u in U]
    gj = []
    for u, (r, kk, v, lw, kd, b, h0, rev) in enumerate(units):
        e_end = jnp.exp(cl[u] - cum[u])
        gj.append(_bdot_tn(jnp.concatenate([b * e_end, kd * e_end], axis=0), z[u]))
    out = []
    for u in U:
        g = jnp.where(eye, jnp.exp(cl[u]), 0.0) + gj[u][:, :HEAD_DIM]
        yh = _bdot(jnp.concatenate([rt[u] + ry[u][:, :HEAD_DIM], g], axis=0), units[u][6])
        out.append((yh[:T] + ry[u][:, HEAD_DIM:], yh[T:] + gj[u][:, HEAD_DIM:]))
    return out


def _rwkv_kernel(cf_ref, cb_ref, df_ref, db_ref, yf_ref, yb_ref, hs):
    c = pl.program_id(1)
    S, H = cf_ref.shape[0], cf_ref.shape[2]

    @pl.when(c == 0)
    def _():
        hs[...] = jnp.zeros_like(hs)

    units, where = [], []
    for s in range(S):
        for d, (c_ref, d_ref) in enumerate(((cf_ref, df_ref), (cb_ref, db_ref))):
            for h in range(H):
                units.append((c_ref[s, 0, h], c_ref[s, 1, h], c_ref[s, 2, h],
                              d_ref[s, 0, h], d_ref[s, 1, h], d_ref[s, 2, h], hs[s, d, h], d == 1))
                where.append((s, d, h))
    for (s, d, h), (y, hn) in zip(where, _rwkv_units(units)):
        (yf_ref, yb_ref)[d][s, h] = y
        hs[s, d, h] = hn


def _rwkv_scan(com, dirs, n_ctx):
    bsz, _, H, nt, _ = com.shape
    T = RK_CHUNK
    S = RK_SAMPLES if bsz % RK_SAMPLES == 0 else 1
    bmap = functools.partial(_backward_chunk, nc=nt // T, ncc=n_ctx // T)
    cblk = (S, 3, H, T, HEAD_DIM)
    dblk = (S, None, 3, H, T, HEAD_DIM)
    oblk = (S, H, T, HEAD_DIM)
    return pl.pallas_call(
        _rwkv_kernel,
        grid=(bsz // S, nt // T),
        in_specs=[pl.BlockSpec(cblk, lambda b, c: (b, 0, 0, c, 0)),
                  pl.BlockSpec(cblk, lambda b, c: (b, 0, 0, bmap(c), 0)),
                  pl.BlockSpec(dblk, lambda b, c: (b, 0, 0, 0, c, 0)),
                  pl.BlockSpec(dblk, lambda b, c: (b, 1, 0, 0, bmap(c), 0))],
        out_specs=[pl.BlockSpec(oblk, lambda b, c: (b, 0, c, 0)),
                   pl.BlockSpec(oblk, lambda b, c: (b, 0, bmap(c), 0))],
        out_shape=[jax.ShapeDtypeStruct((bsz, H, nt, HEAD_DIM), jnp.float32)] * 2,
        scratch_shapes=[pltpu.VMEM((S, 2, H, HEAD_DIM, HEAD_DIM), jnp.float32)],
        compiler_params=_compiler_params(("parallel", "arbitrary")),
    )(com, com, dirs, dirs)


def _mixout_kernel(u_ref, sf_ref, sb_ref, s5v_ref, gw_ref, na_ref, rf_ref, rb_ref, grk_ref, bonus_ref, ln_ref,
                   w_ref, ng_ref, gt_ref, x_ref, sc_ref, sh_ref, wrt_ref, o_ref, h_ref, aff_ref):
    y5 = jax.nn.gelu(s5v_ref[0:1, :] * u_ref[...] + sf_ref[...] + sb_ref[...])
    y5 = y5 * jax.nn.sigmoid(_dotf(_bf(y5), gw_ref[...]) + s5v_ref[1:2, :])
    outs = []
    for h in range(RK_HEADS):
        y = rf_ref[h] + rb_ref[h]
        yc = y - jnp.mean(y, axis=-1, keepdims=True)
        var = jnp.mean(yc * yc, axis=-1, keepdims=True)
        outs.append(yc * lax.rsqrt(var + RK_GN_EPS))
    yrk = (jnp.concatenate(outs, axis=-1) * ln_ref[0:1, :] + ln_ref[1:2, :] + bonus_ref[...]) * grk_ref[...]
    z = (_dotf(_bf(y5), w_ref[:S5_WIDTH, :])
         + _dotf(_bf(na_ref[...]), w_ref[S5_WIDTH:S5_WIDTH + NA_WIDTH, :])
         + _dotf(_bf(yrk), w_ref[S5_WIDTH + NA_WIDTH:, :]))
    zn = z * lax.rsqrt(jnp.mean(z * z, axis=-1, keepdims=True) + RMS_EPS)
    x = x_ref[...] + gt_ref[...] * (zn * ng_ref[0:1, :])
    o_ref[...] = x
    y = x * lax.rsqrt(jnp.mean(x * x, axis=-1, keepdims=True) + RMS_EPS)
    h = _bf((y * ng_ref[1:2, :]) * (1.0 + sc_ref[...]) + sh_ref[...])
    h_ref[...] = h
    logits = lax.dot_general(wrt_ref[...], h, _NT_DIMS, preferred_element_type=jnp.float32)
    e = jnp.exp(logits - jnp.max(logits, axis=0, keepdims=True))
    aff_ref[...] = e / jnp.sum(e, axis=0, keepdims=True)


def _mixout(u, sf, sb, s5v, glu_w, y_na, rf, rb, g_rk, bonus, ln, w_out, ng, gates, xc, mods, wrt, n_ctx):
    bsz, nt, d = xc.shape
    tn = _token_tile(n_ctx)
    cb = n_ctx // tn
    ne = wrt.shape[0]

    def tok(w):
        return pl.BlockSpec((None, tn, w), lambda b, i: (b, i, 0))

    def full(a):
        return pl.BlockSpec(a.shape, lambda b, i: (0,) * a.ndim)

    tm = pl.BlockSpec((tn, S5_WIDTH), lambda b, i: (i, b))
    yblk = pl.BlockSpec((None, RK_HEADS, tn, HEAD_DIM), lambda b, i: (b, 0, i, 0))
    return pl.pallas_call(
        _mixout_kernel,
        grid=(bsz, nt // tn),
        in_specs=[tm, tm, tm, full(s5v), full(glu_w), tok(NA_WIDTH), yblk, yblk, tok(RK_WIDTH), tok(RK_WIDTH),
                  full(ln), full(w_out), full(ng),
                  pl.BlockSpec((None, None, None, 1, d), lambda b, i: (b, jnp.minimum(i // cb, 1), 0, 0, 0)),
                  tok(d), *_mod_specs(bsz, d, n_ctx), full(wrt)],
        out_specs=[tok(d), tok(d), pl.BlockSpec((None, ne, tn), lambda b, i: (b, 0, i))],
        out_shape=[jax.ShapeDtypeStruct(xc.shape, jnp.float32), jax.ShapeDtypeStruct(xc.shape, jnp.bfloat16),
                   jax.ShapeDtypeStruct((bsz, ne, nt), jnp.float32)],
        compiler_params=_compiler_params(("parallel", "parallel")),
    )(u, sf, sb, s5v, glu_w, y_na, rf, rb, g_rk, bonus, ln, w_out, ng, gates, xc, mods, mods, wrt)


def _cast_kernel(w_ref, o_ref):
    o_ref[...] = _bf(w_ref[...])


def _expert_weights_bf16(w, l):
    _, ne, r, c = w.shape
    nblk = r * c // CAST_ELEMS
    tr = r // nblk
    return pl.pallas_call(
        _cast_kernel,
        grid=(ne, nblk),
        in_specs=[pl.BlockSpec((None, None, tr, c), lambda e, i: (l, e, i, 0))],
        out_specs=pl.BlockSpec((None, tr, c), lambda e, i: (e, i, 0)),
        out_shape=jax.ShapeDtypeStruct((ne, r, c), jnp.bfloat16),
        compiler_params=_compiler_params(("parallel", "parallel")),
    )(w)


def _capacity(n):
    return EC_CAPACITY_FACTOR * n // N_EXPERTS


def _prefix_count(m):
    ne, n = m.shape
    s = lax.broadcasted_iota(jnp.int32, (SEL_LANES, SEL_LANES), 0)
    t = lax.broadcasted_iota(jnp.int32, (SEL_LANES, SEL_LANES), 1)
    tri = _bf(s < t)
    out, carry = [], jnp.zeros((ne, 1), jnp.float32)
    for j in range(n // SEL_LANES):
        blk = m[:, j * SEL_LANES:(j + 1) * SEL_LANES]
        out.append(_dotf(_bf(blk), tri) + carry)
        carry = carry + jnp.sum(blk, axis=1, keepdims=True)
    return jnp.concatenate(out, axis=1)


def _select_segment(a, k):
    key = pltpu.bitcast(a, jnp.int32)

    def step(i, tau):
        cand = tau | (1 << (30 - i))
        cnt = jnp.sum((key >= cand).astype(jnp.float32), axis=1, keepdims=True)
        return jnp.where(cnt >= k, cand, tau)

    tau = lax.fori_loop(0, 31, step, jnp.zeros((a.shape[0], 1), jnp.int32))
    gt = key > tau
    eq = key == tau
    need = k - jnp.sum(gt.astype(jnp.float32), axis=1, keepdims=True)
    sel = gt | (eq & (_prefix_count(eq.astype(jnp.float32)) < need))
    return jnp.where(sel, _prefix_count(sel.astype(jnp.float32)), -1.0)


def _select_kernel(a_ref, o_ref, *, segments):
    if sum(n for _, n in segments) < o_ref.shape[1]:
        o_ref[...] = jnp.full(o_ref.shape, -1.0, o_ref.dtype)
    for (t0, n) in segments:
        o_ref[:, t0:t0 + n] = _select_segment(a_ref[:, t0:t0 + n], _capacity(n))


def _select(aff, segments):
    bsz, ne, nt = aff.shape
    return pl.pallas_call(
        functools.partial(_select_kernel, segments=segments),
        grid=(bsz,),
        in_specs=[pl.BlockSpec((None, ne, nt), lambda b: (b, 0, 0))],
        out_specs=pl.BlockSpec((None, ne, nt), lambda b: (b, 0, 0)),
        out_shape=jax.ShapeDtypeStruct((bsz, ne, nt), jnp.float32),
        compiler_params=_compiler_params(("parallel",)),
    )(aff)


def _expert_kernel(slot_ref, aff_ref, h_ref, wg_ref, wu_ref, wd_ref, y_ref, *, G, cap, n, t_in):
    e = pl.program_id(0)
    t0 = t_in
    want = lax.broadcasted_iota(jnp.int32, (cap, n), 0).astype(jnp.float32)
    hit = [slot_ref[s, pl.ds(e, 1), t0:t0 + n] == want for s in range(G)]
    xin = _bf(jnp.concatenate([_dotf(_bf(hit[s]), h_ref[s, pl.ds(t_in, n), :]) for s in range(G)], axis=0))
    acc = jnp.zeros((G * cap, wd_ref.shape[1]), jnp.float32)
    for c0 in range(0, wg_ref.shape[1], FFN_FCHUNK):
        cols = slice(c0, min(c0 + FFN_FCHUNK, wg_ref.shape[1]))
        hid = jax.nn.silu(_dotf(xin, wg_ref[:, cols])) * _dotf(xin, wu_ref[:, cols])
        acc = acc + _dotf(_bf(hid), wd_ref[cols, :])
    gate = jnp.concatenate([jnp.sum(jnp.where(hit[s], aff_ref[s, pl.ds(e, 1), t0:t0 + n], 0.0), axis=1, keepdims=True)
                            for s in range(G)], axis=0)
    y_ref[...] = _bf(acc * gate)


def _experts(slots, aff, h2, wg, wu, wd, t0, n):
    bsz, ne, nt = slots.shape
    d = h2.shape[2]
    f = wg.shape[2]
    cap = _capacity(n)
    G = max(1, min(bsz, FFN_ROWS // cap))
    if t0 % n == 0:
        hspec = pl.BlockSpec((G, n, d), lambda e, b: (b, t0 // n, 0))
        sspec = pl.BlockSpec((G, ne, n), lambda e, b: (b, 0, t0 // n))
        t_in = 0
    else:
        hspec = pl.BlockSpec((G, nt, d), lambda e, b: (b, 0, 0))
        sspec = pl.BlockSpec((G, ne, nt), lambda e, b: (b, 0, 0))
        t_in = t0

    def wspec(shape):
        return pl.BlockSpec((None,) + shape, lambda e, b: (e, 0, 0))

    return pl.pallas_call(
        functools.partial(_expert_kernel, G=G, cap=cap, n=n, t_in=t_in),
        grid=(ne, bsz // G),
        in_specs=[sspec, sspec, hspec, wspec((d, f)), wspec((d, f)), wspec((f, d))],
        out_specs=pl.BlockSpec((None, None, G * cap, d), lambda e, b: (b, e, 0, 0)),
        out_shape=jax.ShapeDtypeStruct((bsz // G, ne, G * cap, d), jnp.bfloat16),
        compiler_params=_compiler_params(("arbitrary", "arbitrary"), EXPERT_VMEM_LIMIT_BYTES),
    )(slots, aff, h2, wg, wu, wd)


def _combine_kernel(slot_ref, y_ref, x_ref, g_ref, gt_ref, o_ref, *, t_lo, t_hi):
    i = pl.program_id(1)
    inside = (i >= t_lo) & (i < t_hi)
    ne, cap, d = y_ref.shape

    @pl.when(inside)
    def _():
        want = lax.broadcasted_iota(jnp.int32, (x_ref.shape[0], cap), 1).astype(jnp.float32)
        hit = jnp.concatenate([_bf(slot_ref[:, e:e + 1] == want) for e in range(ne)], axis=1)
        f = _dotf(hit, y_ref[...].reshape(ne * cap, d))
        fn = f * lax.rsqrt(jnp.mean(f * f, axis=-1, keepdims=True) + RMS_EPS)
        o_ref[...] = x_ref[...] + gt_ref[...] * (fn * g_ref[...])

    @pl.when(jnp.logical_not(inside))
    def _():
        o_ref[...] = x_ref[...]


def _combine(slots_t, y, xc, g, gt, t0, n, n_ctx, keep_rest):
    bsz, nt, ne = slots_t.shape
    d = xc.shape[2]
    cap = _capacity(n)
    G = bsz // y.shape[0]
    yb = y.reshape(bsz // G, ne, G, cap, d)
    tn = _token_tile(n_ctx)
    t_lo, t_hi = t0 // tn, (t0 + n) // tn
    first, tiles = (0, nt // tn) if keep_rest else (t_lo, t_hi - t_lo)
    return pl.pallas_call(
        functools.partial(_combine_kernel, t_lo=t_lo - first, t_hi=t_hi - first),
        grid=(bsz, tiles),
        in_specs=[pl.BlockSpec((None, tn, ne), lambda b, i: (b, i + first, 0)),
                  pl.BlockSpec((None, ne, None, cap, d), lambda b, i: (b // G, 0, b % G, 0, 0)),
                  pl.BlockSpec((None, tn, d), lambda b, i: (b, i + first, 0)),
                  pl.BlockSpec((1, d), lambda b, i: (0, 0)),
                  pl.BlockSpec((None, 1, d), lambda b, i: (b, 0, 0))],
        out_specs=pl.BlockSpec((None, tn, d), lambda b, i: (b, i, 0)),
        out_shape=jax.ShapeDtypeStruct((bsz, tiles * tn, d), jnp.float32),
        compiler_params=_compiler_params(("parallel", "arbitrary")),
    )(slots_t, yb, xc, g.reshape(1, d), gt.reshape(bsz, 1, d))


def _expert_choice(xc, h2, aff, slots, slots_t, wg, wu, wd, g, gt, t0, n, n_ctx, keep_rest=True):
    y = _experts(slots, aff, h2, wg, wu, wd, t0, n)
    return _combine(slots_t, y, xc, g, gt, t0, n, n_ctx, keep_rest)


def kernel(x, c, ctx, c_ctx, ada_w, ada_b, norm_g, w_in, w_out, s5_lam_re, s5_lam_im, s5_log_dt, s5_b_re, s5_b_im, s5_c_re, s5_c_im, s5_d, s5_glu_w, s5_glu_b, na_rpb, rk_mu, rk_w0, rk_w2, rk_a0, rk_a2, rk_g2, rk_k_k, rk_k_a, rk_r_k, rk_ln_w, rk_ln_b, ec_router, ec_w_gate, ec_w_up, ec_w_down):
    bsz, n, d = x.shape
    n_ctx = ctx.shape[1]
    rows = n // GRID_W
    xc = jnp.concatenate([ctx, x], axis=1)
    cos, sin = _rope_tables(n_ctx, n)

    for l in range(DEPTH):
        need_ctx = l < DEPTH - 1
        mod = jax.nn.silu(c) @ ada_w[l] + ada_b[l]
        mod_c = jnp.broadcast_to(jax.nn.silu(c_ctx) @ ada_w[l] + ada_b[l], mod.shape)
        both = jnp.stack([mod_c, mod], axis=1).reshape(bsz, 2, 6, 1, d)
        mods1 = jnp.stack([both[:, :, 1], both[:, :, 0]], axis=2)
        mods2 = jnp.stack([both[:, :, 4], both[:, :, 3]], axis=2)
        gates1 = both[:, :, 2:3]

        vec = jnp.stack([rk_k_k[l], rk_k_a[l], rk_r_k[l].reshape(-1), rk_w0[l, 0], rk_w0[l, 1],
                         rk_a0[l, 0], rk_a0[l, 1], jnp.zeros_like(rk_k_k[l])])
        u, qkv, com, dirs, g_rk, bonus = _inproj(xc, norm_g[l, 0], mods1, _bf(w_in[l]), cos, sin, rk_mu[l], vec,
                                                 _bf(rk_w2[l]), _bf(rk_a2[l]), _bf(rk_g2[l]), n_ctx)

        lam, bblk, cblk = _s5_params(s5_lam_re[l], s5_lam_im[l], s5_log_dt[l], s5_b_re[l], s5_b_im[l],
                                     s5_c_re[l], s5_c_im[l])
        sf, sb = _s5_scan(u, lam, bblk, cblk, bsz, n_ctx)

        bias, off_idx = _na_bias_table(na_rpb[l], rows)
        y_na = _na_attention(qkv, bias, off_idx, n_ctx)

        rf, rb = _rwkv_scan(com, dirs, n_ctx)

        xc, h2, aff = _mixout(u, sf, sb, jnp.stack([s5_d[l], s5_glu_b[l]]),
                              _bf(s5_glu_w[l]), y_na, rf, rb, g_rk, bonus, jnp.stack([rk_ln_w[l], rk_ln_b[l]]),
                              _bf(w_out[l]), norm_g[l, 1:4], gates1, xc, mods2, _bf(ec_router[l].T), n_ctx)
        wg, wu, wd = (_expert_weights_bf16(w, l) for w in (ec_w_gate, ec_w_up, ec_w_down))
        slots = _select(aff, ((n_ctx, n), (0, n_ctx)) if need_ctx else ((n_ctx, n),))
        slots_t = jnp.swapaxes(slots, 1, 2)
        xc = _expert_choice(xc, h2, aff, slots, slots_t, wg, wu, wd, norm_g[l, 3], both[:, 1, 5, 0], n_ctx, n, n_ctx,
                            keep_rest=need_ctx)
        if need_ctx:
            xc = _expert_choice(xc, h2, aff, slots, slots_t, wg, wu, wd, norm_g[l, 3], both[:, 0, 5, 0], 0, n_ctx,
                                n_ctx)
    return xc
```

```python
import functools
import math

import jax
import jax.numpy as jnp
import numpy as np
from jax import lax
from jax.experimental import pallas as pl
from jax.experimental.pallas import tpu as pltpu

D_MODEL = 1024
DEPTH = 2
GRID_W = 64
D_MIX = D_MODEL
HEAD_DIM = 64
S5_WIDTH = D_MIX // 4
S5_GROUP = 16
S5_GROUPS = S5_WIDTH // S5_GROUP
S5_STATE = 64
NA_WIDTH = (D_MIX - S5_WIDTH) // 2
NA_HEADS = NA_WIDTH // HEAD_DIM
NA_WIN_R = 8
NA_WIN_C = 16
ROPE_BASE = 10000.0
RK_WIDTH = D_MIX - S5_WIDTH - NA_WIDTH
RK_HEADS = RK_WIDTH // HEAD_DIM
RK_DECAY_RANK = 64
RK_A_RANK = 64
RK_GATE_RANK = 128
RK_IN_WIDTH = 3 * RK_WIDTH + 2 * RK_DECAY_RANK + 2 * RK_A_RANK + RK_GATE_RANK
RK_GN_EPS = 64e-5
N_IN = S5_WIDTH + 3 * NA_WIDTH + RK_IN_WIDTH
N_EXPERTS = 16
EC_CAPACITY_FACTOR = 2
RMS_EPS = 1e-6
NEG_INF = -1e30

VMEM_LIMIT_BYTES = 56 * 1024 * 1024
EXPERT_VMEM_LIMIT_BYTES = 62 * 1024 * 1024
TOKEN_TILE = 256
S5_GP = S5_GROUPS * S5_STATE
S5_CHUNK = 64
S5_LANES = 512
RK_CHUNK = 64
RK_SAMPLES = 4
NA_QROWS = 4
FFN_ROWS = 512
FFN_FCHUNK = 768
CAST_ELEMS = 1024 * 1408
SEL_LANES = 256
COMBINE_WINDOW = 96


def _compiler_params(semantics, vmem_limit_bytes=VMEM_LIMIT_BYTES):
    return pltpu.CompilerParams(dimension_semantics=semantics, vmem_limit_bytes=vmem_limit_bytes)


def _token_tile(n_ctx):
    return min(TOKEN_TILE, n_ctx)


def _mod_specs(bsz, d, n_ctx):
    cb = n_ctx // _token_tile(n_ctx)
    return [pl.BlockSpec((None, None, None, 1, d), lambda b, i, j=j: (b, jnp.minimum(i // cb, 1), j, 0, 0))
            for j in range(2)]


def _dotf(x, y):
    return jnp.dot(x, y, preferred_element_type=jnp.float32)


def _bf(x):
    return x.astype(jnp.bfloat16)


def _rope_tables(n_ctx, n):
    t = np.arange(n)
    nf = HEAD_DIM // 4
    inv_freq = ROPE_BASE ** (-np.arange(nf, dtype=np.float32) / nf)
    pos = np.stack([(t // GRID_W).astype(np.float32), (t % GRID_W).astype(np.float32)], axis=1)
    ang = pos[:, :, None] * inv_freq[None, None, :]
    cos = np.repeat(np.cos(ang), 2, axis=1).reshape(n, HEAD_DIM)
    sin = np.sin(ang)
    sin = np.stack([-sin[:, 0], sin[:, 0], -sin[:, 1], sin[:, 1]], axis=1).reshape(n, HEAD_DIM)
    cos = np.concatenate([np.ones((n_ctx, HEAD_DIM), np.float32), cos.astype(np.float32)], axis=0)
    sin = np.concatenate([np.zeros((n_ctx, HEAD_DIM), np.float32), sin.astype(np.float32)], axis=0)
    reps = 2 * NA_HEADS
    return jnp.asarray(np.tile(cos, (1, reps))), jnp.asarray(np.tile(sin, (1, reps)))


def _inproj_kernel(x_ref, xp_ref, xn_ref, g_ref, sc_ref, sh_ref, w_ref, cos_ref, sin_ref,
                   mu_ref, vec_ref, w2_ref, a2_ref, g2_ref, ones_ref,
                   u_ref, qkv_ref, com_ref, dir_ref, grk_ref, bonus_ref, *, tn, n_ctx):
    x = jnp.concatenate([xp_ref[...], x_ref[...], xn_ref[...]], axis=0)
    y = x * lax.rsqrt(jnp.mean(x * x, axis=-1, keepdims=True) + RMS_EPS)
    h_ext = _bf((y * g_ref[...]) * (1.0 + sc_ref[...]) + sh_ref[...])
    z_ext = _dotf(h_ext, w_ref[:, S5_WIDTH + 3 * NA_WIDTH:])
    t0 = pl.program_id(1) * tn
    nt = pl.num_programs(1) * tn
    keep_prev = jnp.where((t0 == 0) | (t0 == n_ctx), 0.0, 1.0)
    keep_next = jnp.where((t0 + tn == n_ctx) | (t0 + tn == nt), 0.0, 1.0)
    _rk_prepare(z_ext[8:8 + tn], z_ext[7:8] * keep_prev, z_ext[8 + tn:9 + tn] * keep_next, mu_ref, vec_ref,
                w2_ref, a2_ref, g2_ref, ones_ref, com_ref, dir_ref, grk_ref, bonus_ref)
    h = h_ext[8:8 + tn]
    u_ref[...] = _dotf(h, w_ref[:, :S5_WIDTH])
    qk = _dotf(h, w_ref[:, S5_WIDTH:S5_WIDTH + 2 * NA_WIDTH])
    nf = HEAD_DIM // 4
    lane = lax.broadcasted_iota(jnp.int32, qk.shape, 1)
    first = (lane & (2 * nf - 1)) < nf
    partner = jnp.where(first, pltpu.roll(qk, qk.shape[1] - nf, axis=1), pltpu.roll(qk, nf, axis=1))
    qk = _bf(qk * cos_ref[...] + partner * sin_ref[...])
    v = _bf(_dotf(h, w_ref[:, S5_WIDTH + 2 * NA_WIDTH:S5_WIDTH + 3 * NA_WIDTH]))
    for hd in range(NA_HEADS):
        lo = hd * HEAD_DIM
        qkv_ref[0, hd] = qk[:, lo:lo + HEAD_DIM]
        qkv_ref[1, hd] = qk[:, NA_WIDTH + lo:NA_WIDTH + lo + HEAD_DIM]
        qkv_ref[2, hd] = v[:, lo:lo + HEAD_DIM]


def _inproj(xc, g, mods, w_bf16, cos, sin, mu, vec, w2, a2, g2, n_ctx):
    bsz, nt, d = xc.shape
    tn = _token_tile(n_ctx)
    tb = tn // 8
    nb = nt // 8
    ones = _head_sum_matrix()
    mu2 = mu.reshape(1, RK_IN_WIDTH)

    def full(a):
        return pl.BlockSpec(a.shape, lambda b, i: (0,) * a.ndim)

    S = jax.ShapeDtypeStruct
    return pl.pallas_call(
        functools.partial(_inproj_kernel, tn=tn, n_ctx=n_ctx),
        grid=(bsz, nt // tn),
        in_specs=[pl.BlockSpec((None, tn, d), lambda b, i: (b, i, 0)),
                  pl.BlockSpec((None, 8, d), lambda b, i: (b, jnp.maximum(i * tb - 1, 0), 0)),
                  pl.BlockSpec((None, 8, d), lambda b, i: (b, jnp.minimum((i + 1) * tb, nb - 1), 0)),
                  pl.BlockSpec((1, d), lambda b, i: (0, 0)),
                  *_mod_specs(bsz, d, n_ctx),
                  pl.BlockSpec((d, N_IN), lambda b, i: (0, 0)),
                  pl.BlockSpec((tn, 2 * NA_WIDTH), lambda b, i: (i, 0)),
                  pl.BlockSpec((tn, 2 * NA_WIDTH), lambda b, i: (i, 0)),
                  full(mu2), full(vec), full(w2), full(a2), full(g2), full(ones)],
        out_specs=[pl.BlockSpec((tn, S5_WIDTH), lambda b, i: (i, b)),
                   pl.BlockSpec((None, 3, NA_HEADS, tn, HEAD_DIM), lambda b, i: (b, 0, 0, i, 0)),
                   pl.BlockSpec((None, 3, RK_HEADS, tn, HEAD_DIM), lambda b, i: (b, 0, 0, i, 0)),
                   pl.BlockSpec((None, 2, 3, RK_HEADS, tn, HEAD_DIM), lambda b, i: (b, 0, 0, 0, i, 0)),
                   pl.BlockSpec((None, tn, RK_WIDTH), lambda b, i: (b, i, 0)),
                   pl.BlockSpec((None, tn, RK_WIDTH), lambda b, i: (b, i, 0))],
        out_shape=[S((nt, bsz * S5_WIDTH), jnp.float32),
                   S((bsz, 3, NA_HEADS, nt, HEAD_DIM), jnp.bfloat16),
                   S((bsz, 3, RK_HEADS, nt, HEAD_DIM), jnp.float32),
                   S((bsz, 2, 3, RK_HEADS, nt, HEAD_DIM), jnp.float32),
                   S((bsz, nt, RK_WIDTH), jnp.float32), S((bsz, nt, RK_WIDTH), jnp.float32)],
        compiler_params=_compiler_params(("parallel", "parallel")),
    )(xc, xc, xc, g.reshape(1, d), mods, mods, w_bf16, cos, sin, mu2, vec, w2, a2, g2, ones)


def _s5_kernel(uf_ref, ub_ref, lam_ref, bw_ref, cw_ref, yf_ref, yb_ref, sbuf, st, *, T, B):
    c = pl.program_id(0)

    @pl.when(c == 0)
    def _():
        st[...] = jnp.zeros_like(st)

    for d, u_ref in enumerate((uf_ref, ub_ref)):
        sbuf[d] = _dotf(_bf(u_ref[...].reshape(T * B, S5_WIDTH)), bw_ref[d])
    for d in range(2):
        for h in range(S5_GP // S5_LANES):
            re_sl = pl.ds(h * S5_LANES, S5_LANES)
            im_sl = pl.ds(S5_GP + h * S5_LANES, S5_LANES)
            lr = jnp.broadcast_to(lam_ref[d, 0:1, h * S5_LANES:(h + 1) * S5_LANES], (B, S5_LANES))
            li = jnp.broadcast_to(lam_ref[d, 1:2, h * S5_LANES:(h + 1) * S5_LANES], (B, S5_LANES))
            s_re, s_im = st[d, :, re_sl], st[d, :, im_sl]
            for i in range(T):
                t = i if d == 0 else T - 1 - i
                rows = pl.ds(t * B, B)
                s_re, s_im = (lr * s_re - li * s_im + sbuf[d, rows, re_sl],
                              lr * s_im + li * s_re + sbuf[d, rows, im_sl])
                sbuf[d, rows, re_sl] = s_re
                sbuf[d, rows, im_sl] = s_im
            st[d, :, re_sl] = s_re
            st[d, :, im_sl] = s_im
    for d, y_ref in enumerate((yf_ref, yb_ref)):
        y_ref[...] = _dotf(_bf(sbuf[d]), cw_ref[d]).reshape(T, B * S5_WIDTH)


def _backward_chunk(c, nc, ncc):
    return jnp.where(c < ncc, ncc - 1 - c, nc + ncc - 1 - c)


def _s5_scan(uf, lam, bblk, cblk, bsz, n_ctx):
    T = S5_CHUNK
    nc = uf.shape[0] // T
    ncc = n_ctx // T
    bw, cw = _bf(bblk), _bf(cblk)
    bmap = functools.partial(_backward_chunk, nc=nc, ncc=ncc)
    blk = (T, bsz * S5_WIDTH)

    def full(shape):
        return pl.BlockSpec(shape, lambda c: (0,) * len(shape))

    return pl.pallas_call(
        functools.partial(_s5_kernel, T=T, B=bsz),
        grid=(nc,),
        in_specs=[pl.BlockSpec(blk, lambda c: (c, 0)), pl.BlockSpec(blk, lambda c: (bmap(c), 0)),
                  full(lam.shape), full(bw.shape), full(cw.shape)],
        out_specs=[pl.BlockSpec(blk, lambda c: (c, 0)), pl.BlockSpec(blk, lambda c: (bmap(c), 0))],
        out_shape=[jax.ShapeDtypeStruct(uf.shape, jnp.float32)] * 2,
        scratch_shapes=[pltpu.VMEM((2, T * bsz, 2 * S5_GP), jnp.float32),
                        pltpu.VMEM((2, bsz, 2 * S5_GP), jnp.float32)],
        compiler_params=_compiler_params(("arbitrary",)),
    )(uf, uf, lam, bw, cw)


def _s5_params(lam_re, lam_im, log_dt, b_re, b_im, c_re, c_im):
    lams, bs, cs = [], [], []
    eye = jnp.eye(S5_GROUPS, dtype=jnp.float32)
    for d in range(2):
        dt = jnp.exp(log_dt[d])[:, None]
        mag = jnp.exp(lam_re[d] * dt)
        lb_re, lb_im = mag * jnp.cos(lam_im[d] * dt), mag * jnp.sin(lam_im[d] * dt)
        den = lam_re[d] ** 2 + lam_im[d] ** 2
        nr, ni = lb_re - 1.0, lb_im
        f_re = (nr * lam_re[d] + ni * lam_im[d]) / den
        f_im = (ni * lam_re[d] - nr * lam_im[d]) / den
        bb_re = f_re[..., None] * b_re - f_im[..., None] * b_im
        bb_im = f_re[..., None] * b_im + f_im[..., None] * b_re

        def blockdiag_in(m):
            return jnp.einsum('gph,gk->ghkp', m, eye).reshape(S5_WIDTH, S5_GP)

        def blockdiag_out(m):
            return jnp.einsum('ghp,gk->gpkh', m, eye).reshape(S5_GP, S5_WIDTH)

        bs.append(jnp.concatenate([blockdiag_in(bb_re), blockdiag_in(bb_im)], axis=1))
        cs.append(jnp.concatenate([blockdiag_out(c_re[d]), -blockdiag_out(c_im[d])], axis=0))
        lams.append(jnp.stack([lb_re.reshape(S5_GP), lb_im.reshape(S5_GP)]))
    return jnp.stack(lams), jnp.stack(bs), jnp.stack(cs)


def _na_offsets(rows):
    win_r = min(NA_WIN_R, rows)
    i = np.arange(rows)
    r0 = np.clip(i - win_r // 2, 0, rows - win_r)
    return r0 - i + NA_WIN_R - 1, win_r


def _na_bias_table(rpb, rows):
    off, win_r = _na_offsets(rows)
    offs = np.unique(off)
    qc = np.arange(GRID_W)[:, None]
    kc = np.arange(GRID_W)[None, :]
    c0 = np.clip(qc - NA_WIN_C // 2, 0, GRID_W - NA_WIN_C)
    valid = (kc >= c0) & (kc < c0 + NA_WIN_C)
    dc = np.clip(kc - qc, 1 - NA_WIN_C, NA_WIN_C - 1) + NA_WIN_C - 1
    full = jnp.where(valid[None, :, None, :], rpb[:, :, dc].transpose(0, 2, 1, 3), NEG_INF)
    full = full.reshape(rpb.shape[0], GRID_W, -1)
    tabs = [full[:, :, o * GRID_W:(o + win_r) * GRID_W] for o in offs]
    return jnp.stack(tabs, axis=1), jnp.asarray(off - offs[0], jnp.int32)


_NT_DIMS = (((1,), (1,)), ((), ()))


def _softmax_pv(s_list, v_list):
    units = range(len(s_list[0]))
    m = [functools.reduce(jnp.maximum, [jnp.max(s[u], axis=-1, keepdims=True) for s in s_list]) for u in units]
    p = [[jnp.exp(s[u] - m[u]) for u in units] for s in s_list]
    den = [sum(jnp.sum(pj[u], axis=-1, keepdims=True) for pj in p) for u in units]
    o = [sum(_dotf(_bf(pj[u]), vj[u]) for pj, vj in zip(p, v_list)) for u in units]
    return [o[u] / den[u] for u in units]


def _na_kernel(off_ref, q_ref, k_ref, v_ref, *rest, n_ctx, win_r, rows):
    bias_refs, o_ref = rest[:-1], rest[-1]
    Q = len(bias_refs)
    j = pl.program_id(1) * Q
    cb = n_ctx // GRID_W
    H = q_ref.shape[0]
    units = [(qi, h) for qi in range(Q) for h in range(H)]
    scale = HEAD_DIM ** -0.5

    def scores(qi, h, start, size):
        return lax.dot_general(q_ref[h, qi * GRID_W:(qi + 1) * GRID_W, :], k_ref[h, pl.ds(start, size), :], _NT_DIMS,
                               preferred_element_type=jnp.float32) * scale

    def store(o):
        for qi in range(Q):
            o_ref[qi * GRID_W:(qi + 1) * GRID_W, :] = jnp.concatenate(o[qi * H:(qi + 1) * H], axis=-1)

    @pl.when(j < cb)
    def _():
        s = [scores(qi, h, 0, n_ctx) for qi, h in units]
        store(_softmax_pv([s], [[v_ref[h, pl.ds(0, n_ctx), :] for _, h in units]]))

    @pl.when(j >= cb)
    def _():
        nk = win_r * GRID_W
        start = [pl.multiple_of(n_ctx + jnp.clip(j - cb + qi - win_r // 2, 0, rows - win_r) * GRID_W, GRID_W)
                 for qi in range(Q)]
        s_lat = [scores(qi, h, start[qi], nk) + bias_refs[qi][h] for qi, h in units]
        s_ctx = [scores(qi, h, 0, n_ctx) for qi, h in units]
        store(_softmax_pv([s_lat, s_ctx], [[v_ref[h, pl.ds(start[qi], nk), :] for qi, h in units],
                                           [v_ref[h, pl.ds(0, n_ctx), :] for _, h in units]]))


def _na_attention(qkv, bias, off_idx, n_ctx):
    bsz, _, H, nt, _ = qkv.shape
    rows = (nt - n_ctx) // GRID_W
    win_r = min(NA_WIN_R, rows)
    cb = n_ctx // GRID_W
    Q = NA_QROWS if (cb % NA_QROWS == 0 and rows % NA_QROWS == 0) else 1

    def bias_spec(qi):
        return pl.BlockSpec((H, None, GRID_W, win_r * GRID_W),
                            lambda b, j, off: (0, off[jnp.maximum(j * Q + qi - cb, 0)], 0, 0))

    return pl.pallas_call(
        functools.partial(_na_kernel, n_ctx=n_ctx, win_r=win_r, rows=rows),
        grid_spec=pltpu.PrefetchScalarGridSpec(
            num_scalar_prefetch=1,
            grid=(bsz, (cb + rows) // Q),
            in_specs=[pl.BlockSpec((None, None, H, Q * GRID_W, HEAD_DIM), lambda b, j, off: (b, 0, 0, j, 0)),
                      pl.BlockSpec((None, None, H, nt, HEAD_DIM), lambda b, j, off: (b, 1, 0, 0, 0)),
                      pl.BlockSpec((None, None, H, nt, HEAD_DIM), lambda b, j, off: (b, 2, 0, 0, 0)),
                      *[bias_spec(qi) for qi in range(Q)]],
            out_specs=pl.BlockSpec((None, Q * GRID_W, H * HEAD_DIM), lambda b, j, off: (b, j, 0)),
        ),
        out_shape=jax.ShapeDtypeStruct((bsz, nt, H * HEAD_DIM), jnp.float32),
        compiler_params=_compiler_params(("parallel", "arbitrary")),
    )(off_idx, qkv, qkv, qkv, *([bias] * Q))


def _head_sum_matrix():
    i = np.arange(RK_WIDTH)
    return jnp.asarray((i[:, None] // HEAD_DIM) == (i[None, :] // HEAD_DIM), jnp.bfloat16)


def _head_sums(x, ones):
    hi = x.astype(jnp.bfloat16)
    lo = (x - hi.astype(jnp.float32)).astype(jnp.bfloat16)
    return _dotf(hi, ones) + _dotf(lo, ones)


def _rk_prepare(z, z_prev, z_next, mu_ref, vec_ref, w2_ref, a2_ref, g2_ref, ones_ref,
                com_ref, dir_ref, g_ref, bonus_ref):
    tn = z.shape[0]
    row = lax.broadcasted_iota(jnp.int32, z.shape, 0)
    prev = jnp.where(row == 0, z_prev, pltpu.roll(z, 1, axis=0))
    nxt = jnp.where(row == tn - 1, z_next, pltpu.roll(z, tn - 1, axis=0))
    zs = z + (0.5 * (prev + nxt) - z) * mu_ref[...]
    W, R = RK_WIDTH, RK_DECAY_RANK
    r, k, v = zs[:, :W], zs[:, W:2 * W], zs[:, 2 * W:3 * W]
    o = 3 * W
    zw = (zs[:, o:o + R], zs[:, o + R:o + 2 * R])
    za = (zs[:, o + 2 * R:o + 3 * R], zs[:, o + 3 * R:o + 4 * R])
    zg = zs[:, o + 4 * R:]
    k_k, k_a, r_k = vec_ref[0:1, :], vec_ref[1:2, :], vec_ref[2:3, :]
    ones = ones_ref[...]
    g_ref[...] = _dotf(_bf(jax.nn.sigmoid(zg)), g2_ref[...])
    kk = k * k_k
    kk = kk * lax.rsqrt(jnp.maximum(_head_sums(kk * kk, ones), 1e-24))
    bonus = 0.0
    fields = [r, kk, v]
    for d in range(2):
        w = -jax.nn.softplus(-(vec_ref[3 + d:4 + d, :] + _dotf(_bf(jnp.tanh(zw[d])), w2_ref[d]))) - 0.5
        a = jax.nn.sigmoid(vec_ref[5 + d:6 + d, :] + _dotf(_bf(za[d]), a2_ref[d]))
        kd = k * (1.0 + (a - 1.0) * k_a)
        bonus = bonus + _head_sums(r * kd * r_k, ones)
        fields += [-jnp.exp(w), kd, kk * a]
    bonus_ref[...] = bonus * v
    for j, t in enumerate(fields):
        for h in range(RK_HEADS):
            blk = t[:, h * HEAD_DIM:(h + 1) * HEAD_DIM]
            if j < 3:
                com_ref[j, h] = blk
            else:
                dir_ref[(j - 3) // 3, (j - 3) % 3, h] = blk


def _bdot(x, y):
    return _dotf(_bf(x), _bf(y))


def _bdot_nt(x, y):
    return lax.dot_general(_bf(x), _bf(y), _NT_DIMS, preferred_element_type=jnp.float32)


def _bdot_tn(x, y):
    return lax.dot_general(_bf(x), _bf(y), (((0,), (0,)), ((), ())), preferred_element_type=jnp.float32)


def _rwkv_units(units):
    T = units[0][0].shape[0]
    U = range(len(units))
    steps = int(math.log2(T))
    ti = lax.broadcasted_iota(jnp.int32, (T, T), 0)
    si = lax.broadcasted_iota(jnp.int32, (T, T), 1)
    ti2 = lax.broadcasted_iota(jnp.int32, (T, 2 * T), 0)
    si2 = lax.broadcasted_iota(jnp.int32, (T, 2 * T), 1) & (T - 1)
    eye = (lax.broadcasted_iota(jnp.int32, (HEAD_DIM, HEAD_DIM), 0)
           == lax.broadcasted_iota(jnp.int32, (HEAD_DIM, HEAD_DIM), 1))
    masks = {}
    for rev in (False, True):
        incl = (si >= ti) if rev else (si <= ti)
        masks[rev] = (incl.astype(jnp.bfloat16), (si2 >= ti2) if rev else (si2 <= ti2),
                      (si2 > ti2) if rev else (si2 < ti2))
    cum = []
    for (r, kk, v, lw, kd, b, h0, rev) in units:
        l1 = lw.astype(jnp.bfloat16)
        rem = lw - l1.astype(jnp.float32)
        l2 = rem.astype(jnp.bfloat16)
        l3 = (rem - l2.astype(jnp.float32)).astype(jnp.bfloat16)
        tri = masks[rev][0]
        cum.append(_dotf(tri, l1) + (_dotf(tri, l2) + _dotf(tri, l3)))
    at, rt, p, cl = [], [], [], []
    for u, (r, kk, v, lw, kd, b, h0, rev) in enumerate(units):
        c = cum[u]
        cl.append(c[0:1, :] if rev else c[T - 1:T, :])
        e_neg = jnp.exp(-c)
        at.append(-kk * jnp.exp(c - lw))
        rt.append(r * jnp.exp(c))
        p.append(_bdot_nt(jnp.concatenate([at[u], rt[u]], axis=0), jnp.concatenate([b * e_neg, kd * e_neg], axis=0)))
    top = [jnp.where(masks[units[u][7]][2], p[u][:T], 0.0) for u in U]
    l2m = [jnp.where(masks[units[u][7]][1], p[u][T:], 0.0) for u in U]
    npow = [top[u][:, :T] for u in U]
    x = [jnp.concatenate([at[u], _bdot(top[u][:, T:], units[u][2])], axis=1) for u in U]
    for i in range(steps):
        x = [x[u] + _bdot(npow[u], x[u]) for u in U]
        if i < steps - 1:
            npow = [_bdot(npow[u], npow[u]) for u in U]
    z = [jnp.concatenate([x[u], jnp.concatenate([jnp.zeros_like(units[u][2]), units[u][2]], axis=1)], axis=0)
         for u in U]
    ry = [_bdot(l2m[u], z[u]) for u in U]
    gj = []
    for u, (r, kk, v, lw, kd, b, h0, rev) in enumerate(units):
        e_end = jnp.exp(cl[u] - cum[u])
        gj.append(_bdot_tn(jnp.concatenate([b * e_end, kd * e_end], axis=0), z[u]))
    out = []
    for u in U:
        g = jnp.where(eye, jnp.exp(cl[u]), 0.0) + gj[u][:, :HEAD_DIM]
        yh = _bdot(jnp.concatenate([rt[u] + ry[u][:, :HEAD_DIM], g], axis=0), units[u][6])
        out.append((yh[:T] + ry[u][:, HEAD_DIM:], yh[T:] + gj[u][:, HEAD_DIM:]))
    return out


def _rwkv_kernel(cf_ref, cb_ref, df_ref, db_ref, yf_ref, yb_ref, hs):
    c = pl.program_id(1)
    S, H = cf_ref.shape[0], cf_ref.shape[2]

    @pl.when(c == 0)
    def _():
        hs[...] = jnp.zeros_like(hs)

    units, where = [], []
    for s in range(S):
        for d, (c_ref, d_ref) in enumerate(((cf_ref, df_ref), (cb_ref, db_ref))):
            for h in range(H):
                units.append((c_ref[s, 0, h], c_ref[s, 1, h], c_ref[s, 2, h],
                              d_ref[s, 0, h], d_ref[s, 1, h], d_ref[s, 2, h], hs[s, d, h], d == 1))
                where.append((s, d, h))
    for (s, d, h), (y, hn) in zip(where, _rwkv_units(units)):
        (yf_ref, yb_ref)[d][s, h] = y
        hs[s, d, h] = hn


def _rwkv_scan(com, dirs, n_ctx):
    bsz, _, H, nt, _ = com.shape
    T = RK_CHUNK
    S = RK_SAMPLES if bsz % RK_SAMPLES == 0 else 1
    bmap = functools.partial(_backward_chunk, nc=nt // T, ncc=n_ctx // T)
    cblk = (S, 3, H, T, HEAD_DIM)
    dblk = (S, None, 3, H, T, HEAD_DIM)
    oblk = (S, H, T, HEAD_DIM)
    return pl.pallas_call(
        _rwkv_kernel,
        grid=(bsz // S, nt // T),
        in_specs=[pl.BlockSpec(cblk, lambda b, c: (b, 0, 0, c, 0)),
                  pl.BlockSpec(cblk, lambda b, c: (b, 0, 0, bmap(c), 0)),
                  pl.BlockSpec(dblk, lambda b, c: (b, 0, 0, 0, c, 0)),
                  pl.BlockSpec(dblk, lambda b, c: (b, 1, 0, 0, bmap(c), 0))],
        out_specs=[pl.BlockSpec(oblk, lambda b, c: (b, 0, c, 0)),
                   pl.BlockSpec(oblk, lambda b, c: (b, 0, bmap(c), 0))],
        out_shape=[jax.ShapeDtypeStruct((bsz, H, nt, HEAD_DIM), jnp.float32)] * 2,
        scratch_shapes=[pltpu.VMEM((S, 2, H, HEAD_DIM, HEAD_DIM), jnp.float32)],
        compiler_params=_compiler_params(("parallel", "arbitrary")),
    )(com, com, dirs, dirs)


def _mixout_kernel(u_ref, sf_ref, sb_ref, s5v_ref, gw_ref, na_ref, rf_ref, rb_ref, grk_ref, bonus_ref, ln_ref,
                   w_ref, ng_ref, gt_ref, x_ref, sc_ref, sh_ref, wrt_ref, o_ref, h_ref, aff_ref):
    y5 = jax.nn.gelu(s5v_ref[0:1, :] * u_ref[...] + sf_ref[...] + sb_ref[...])
    y5 = y5 * jax.nn.sigmoid(_dotf(_bf(y5), gw_ref[...]) + s5v_ref[1:2, :])
    outs = []
    for h in range(RK_HEADS):
        y = rf_ref[h] + rb_ref[h]
        yc = y - jnp.mean(y, axis=-1, keepdims=True)
        var = jnp.mean(yc * yc, axis=-1, keepdims=True)
        outs.append(yc * lax.rsqrt(var + RK_GN_EPS))
    yrk = (jnp.concatenate(outs, axis=-1) * ln_ref[0:1, :] + ln_ref[1:2, :] + bonus_ref[...]) * grk_ref[...]
    z = (_dotf(_bf(y5), w_ref[:S5_WIDTH, :])
         + _dotf(_bf(na_ref[...]), w_ref[S5_WIDTH:S5_WIDTH + NA_WIDTH, :])
         + _dotf(_bf(yrk), w_ref[S5_WIDTH + NA_WIDTH:, :]))
    zn = z * lax.rsqrt(jnp.mean(z * z, axis=-1, keepdims=True) + RMS_EPS)
    x = x_ref[...] + gt_ref[...] * (zn * ng_ref[0:1, :])
    o_ref[...] = x
    y = x * lax.rsqrt(jnp.mean(x * x, axis=-1, keepdims=True) + RMS_EPS)
    h = _bf((y * ng_ref[1:2, :]) * (1.0 + sc_ref[...]) + sh_ref[...])
    h_ref[...] = h
    logits = lax.dot_general(wrt_ref[...], h, _NT_DIMS, preferred_element_type=jnp.float32)
    e = jnp.exp(logits - jnp.max(logits, axis=0, keepdims=True))
    aff_ref[...] = e / jnp.sum(e, axis=0, keepdims=True)


def _mixout(u, sf, sb, s5v, glu_w, y_na, rf, rb, g_rk, bonus, ln, w_out, ng, gates, xc, mods, wrt, n_ctx):
    bsz, nt, d = xc.shape
    tn = _token_tile(n_ctx)
    cb = n_ctx // tn
    ne = wrt.shape[0]

    def tok(w):
        return pl.BlockSpec((None, tn, w), lambda b, i: (b, i, 0))

    def full(a):
        return pl.BlockSpec(a.shape, lambda b, i: (0,) * a.ndim)

    tm = pl.BlockSpec((tn, S5_WIDTH), lambda b, i: (i, b))
    yblk = pl.BlockSpec((None, RK_HEADS, tn, HEAD_DIM), lambda b, i: (b, 0, i, 0))
    return pl.pallas_call(
        _mixout_kernel,
        grid=(bsz, nt // tn),
        in_specs=[tm, tm, tm, full(s5v), full(glu_w), tok(NA_WIDTH), yblk, yblk, tok(RK_WIDTH), tok(RK_WIDTH),
                  full(ln), full(w_out), full(ng),
                  pl.BlockSpec((None, None, None, 1, d), lambda b, i: (b, jnp.minimum(i // cb, 1), 0, 0, 0)),
                  tok(d), *_mod_specs(bsz, d, n_ctx), full(wrt)],
        out_specs=[tok(d), tok(d), pl.BlockSpec((None, ne, tn), lambda b, i: (b, 0, i))],
        out_shape=[jax.ShapeDtypeStruct(xc.shape, jnp.float32), jax.ShapeDtypeStruct(xc.shape, jnp.bfloat16),
                   jax.ShapeDtypeStruct((bsz, ne, nt), jnp.float32)],
        compiler_params=_compiler_params(("parallel", "parallel")),
    )(u, sf, sb, s5v, glu_w, y_na, rf, rb, g_rk, bonus, ln, w_out, ng, gates, xc, mods, mods, wrt)


def _cast_kernel(w_ref, o_ref):
    o_ref[...] = _bf(w_ref[...])


def _expert_weights_bf16(w, l):
    _, ne, r, c = w.shape
    nblk = r * c // CAST_ELEMS
    tr = r // nblk
    return pl.pallas_call(
        _cast_kernel,
        grid=(ne, nblk),
        in_specs=[pl.BlockSpec((None, None, tr, c), lambda e, i: (l, e, i, 0))],
        out_specs=pl.BlockSpec((None, tr, c), lambda e, i: (e, i, 0)),
        out_shape=jax.ShapeDtypeStruct((ne, r, c), jnp.bfloat16),
        compiler_params=_compiler_params(("parallel", "parallel")),
    )(w)


def _capacity(n):
    return EC_CAPACITY_FACTOR * n // N_EXPERTS


def _prefix_count(m):
    ne, n = m.shape
    s = lax.broadcasted_iota(jnp.int32, (SEL_LANES, SEL_LANES), 0)
    t = lax.broadcasted_iota(jnp.int32, (SEL_LANES, SEL_LANES), 1)
    tri = _bf(s < t)
    out, carry = [], jnp.zeros((ne, 1), jnp.float32)
    for j in range(n // SEL_LANES):
        blk = m[:, j * SEL_LANES:(j + 1) * SEL_LANES]
        out.append(_dotf(_bf(blk), tri) + carry)
        carry = carry + jnp.sum(blk, axis=1, keepdims=True)
    return jnp.concatenate(out, axis=1)


def _select_segment(a, k):
    key = pltpu.bitcast(a, jnp.int32)

    def step(i, tau):
        cand = tau | (1 << (30 - i))
        cnt = jnp.sum((key >= cand).astype(jnp.float32), axis=1, keepdims=True)
        return jnp.where(cnt >= k, cand, tau)

    tau = lax.fori_loop(0, 31, step, jnp.zeros((a.shape[0], 1), jnp.int32))
    gt = key > tau
    eq = key == tau
    need = k - jnp.sum(gt.astype(jnp.float32), axis=1, keepdims=True)
    sel = gt | (eq & (_prefix_count(eq.astype(jnp.float32)) < need))
    pos = _prefix_count(sel.astype(jnp.float32))
    return jnp.where(sel, pos, -1.0), pos


def _select_kernel(a_ref, o_ref, lo_ref, *, segments, tn):
    if sum(n for _, n in segments) < o_ref.shape[1]:
        o_ref[...] = jnp.full(o_ref.shape, -1.0, o_ref.dtype)
        lo_ref[...] = jnp.zeros(lo_ref.shape, lo_ref.dtype)
    for (t0, n) in segments:
        slots, pos = _select_segment(a_ref[:, t0:t0 + n], _capacity(n))
        o_ref[:, t0:t0 + n] = slots
        for j in range(n // tn):
            lo_ref[:, t0 // tn + j:t0 // tn + j + 1] = pos[:, j * tn:j * tn + 1].astype(jnp.int32)


def _select(aff, segments, tn):
    bsz, ne, nt = aff.shape
    return pl.pallas_call(
        functools.partial(_select_kernel, segments=segments, tn=tn),
        grid=(bsz,),
        in_specs=[pl.BlockSpec((None, ne, nt), lambda b: (b, 0, 0))],
        out_specs=[pl.BlockSpec((None, ne, nt), lambda b: (b, 0, 0)),
                   pl.BlockSpec((None, ne, nt // tn), lambda b: (b, 0, 0))],
        out_shape=[jax.ShapeDtypeStruct((bsz, ne, nt), jnp.float32),
                   jax.ShapeDtypeStruct((bsz, ne, nt // tn), jnp.int32)],
        compiler_params=_compiler_params(("parallel",)),
    )(aff)


def _expert_kernel(slot_ref, aff_ref, h_ref, wg_ref, wu_ref, wd_ref, y_ref, *, G, cap, n, t_in):
    e = pl.program_id(0)
    t0 = t_in
    want = lax.broadcasted_iota(jnp.int32, (cap, n), 0).astype(jnp.float32)
    hit = [slot_ref[s, pl.ds(e, 1), t0:t0 + n] == want for s in range(G)]
    xin = _bf(jnp.concatenate([_dotf(_bf(hit[s]), h_ref[s, pl.ds(t_in, n), :]) for s in range(G)], axis=0))
    acc = jnp.zeros((G * cap, wd_ref.shape[1]), jnp.float32)
    for c0 in range(0, wg_ref.shape[1], FFN_FCHUNK):
        cols = slice(c0, min(c0 + FFN_FCHUNK, wg_ref.shape[1]))
        hid = jax.nn.silu(_dotf(xin, wg_ref[:, cols])) * _dotf(xin, wu_ref[:, cols])
        acc = acc + _dotf(_bf(hid), wd_ref[cols, :])
    gate = jnp.concatenate([jnp.sum(jnp.where(hit[s], aff_ref[s, pl.ds(e, 1), t0:t0 + n], 0.0), axis=1, keepdims=True)
                            for s in range(G)], axis=0)
    y_ref[...] = _bf(acc * gate)


def _experts(slots, aff, h2, wg, wu, wd, t0, n):
    bsz, ne, nt = slots.shape
    d = h2.shape[2]
    f = wg.shape[2]
    cap = _capacity(n)
    G = max(1, min(bsz, FFN_ROWS // cap))
    if t0 % n == 0:
        hspec = pl.BlockSpec((G, n, d), lambda e, b: (b, t0 // n, 0))
        sspec = pl.BlockSpec((G, ne, n), lambda e, b: (b, 0, t0 // n))
        t_in = 0
    else:
        hspec = pl.BlockSpec((G, nt, d), lambda e, b: (b, 0, 0))
        sspec = pl.BlockSpec((G, ne, nt), lambda e, b: (b, 0, 0))
        t_in = t0

    def wspec(shape):
        return pl.BlockSpec((None,) + shape, lambda e, b: (e, 0, 0))

    return pl.pallas_call(
        functools.partial(_expert_kernel, G=G, cap=cap, n=n, t_in=t_in),
        grid=(ne, bsz // G),
        in_specs=[sspec, sspec, hspec, wspec((d, f)), wspec((d, f)), wspec((f, d))],
        out_specs=pl.BlockSpec((None, None, G * cap, d), lambda e, b: (b, e, 0, 0)),
        out_shape=jax.ShapeDtypeStruct((bsz // G, ne, G * cap, d), jnp.bfloat16),
        compiler_params=_compiler_params(("arbitrary", "arbitrary"), EXPERT_VMEM_LIMIT_BYTES),
    )(slots, aff, h2, wg, wu, wd)


def _combine_kernel(st_ref, ok_ref, slot_ref, y_ref, x_ref, g_ref, gt_ref, o_ref, *, t_lo, t_hi, first, ntiles, W):
    b, i = pl.program_id(0), pl.program_id(1)
    inside = (i >= t_lo) & (i < t_hi)
    ne, cap, d = y_ref.shape
    tn = x_ref.shape[0]
    tile = i + first
    fits = ok_ref[b * ntiles + tile] == 1

    def finish(f):
        fn = f * lax.rsqrt(jnp.mean(f * f, axis=-1, keepdims=True) + RMS_EPS)
        o_ref[...] = x_ref[...] + gt_ref[...] * (fn * g_ref[...])

    @pl.when(inside & fits)
    def _():
        lane = lax.broadcasted_iota(jnp.int32, (tn, W), 1)
        hits, rows = [], []
        for e in range(ne):
            st = pl.multiple_of(st_ref[(b * ne + e) * ntiles + tile], 16)
            hits.append(_bf(slot_ref[:, e:e + 1] == (lane + st).astype(jnp.float32)))
            rows.append(y_ref[e, pl.ds(st, W), :])
        finish(_dotf(jnp.concatenate(hits, axis=1), jnp.concatenate(rows, axis=0)))

    @pl.when(inside & jnp.logical_not(fits))
    def _():
        want = lax.broadcasted_iota(jnp.int32, (tn, cap), 1).astype(jnp.float32)
        hit = jnp.concatenate([_bf(slot_ref[:, e:e + 1] == want) for e in range(ne)], axis=1)
        finish(_dotf(hit, y_ref[...].reshape(ne * cap, d)))

    @pl.when(jnp.logical_not(inside))
    def _():
        o_ref[...] = x_ref[...]


def _combine(slots_t, lo, y, xc, g, gt, t0, n, n_ctx, keep_rest):
    bsz, nt, ne = slots_t.shape
    d = xc.shape[2]
    cap = _capacity(n)
    G = bsz // y.shape[0]
    yb = y.reshape(bsz // G, ne, G, cap, d)
    tn = _token_tile(n_ctx)
    ntiles = nt // tn
    t_lo, t_hi = t0 // tn, (t0 + n) // tn
    W = min(COMBINE_WINDOW, cap)
    st = jnp.clip(lo // 16 * 16, 0, cap - W)
    end = jnp.concatenate([lo[:, :, 1:], jnp.full_like(lo[:, :, :1], cap)], axis=2)
    end = jnp.where(jnp.arange(ntiles) == t_hi - 1, cap, end)
    ok = jnp.all(end - st <= W, axis=1).astype(jnp.int32)
    first, tiles = (0, ntiles) if keep_rest else (t_lo, t_hi - t_lo)
    return pl.pallas_call(
        functools.partial(_combine_kernel, t_lo=t_lo - first, t_hi=t_hi - first, first=first, ntiles=ntiles, W=W),
        grid_spec=pltpu.PrefetchScalarGridSpec(
            num_scalar_prefetch=2,
            grid=(bsz, tiles),
            in_specs=[pl.BlockSpec((None, tn, ne), lambda b, i, *_: (b, i + first, 0)),
                      pl.BlockSpec((None, ne, None, cap, d), lambda b, i, *_: (b // G, 0, b % G, 0, 0)),
                      pl.BlockSpec((None, tn, d), lambda b, i, *_: (b, i + first, 0)),
                      pl.BlockSpec((1, d), lambda b, i, *_: (0, 0)),
                      pl.BlockSpec((None, 1, d), lambda b, i, *_: (b, 0, 0))],
            out_specs=pl.BlockSpec((None, tn, d), lambda b, i, *_: (b, i, 0)),
        ),
        out_shape=jax.ShapeDtypeStruct((bsz, tiles * tn, d), jnp.float32),
        compiler_params=_compiler_params(("parallel", "arbitrary")),
    )(st.reshape(-1), ok.reshape(-1), slots_t, yb, xc, g.reshape(1, d), gt.reshape(bsz, 1, d))


def _expert_choice(xc, h2, aff, slots, slots_t, lo, wg, wu, wd, g, gt, t0, n, n_ctx, keep_rest=True):
    y = _experts(slots, aff, h2, wg, wu, wd, t0, n)
    return _combine(slots_t, lo, y, xc, g, gt, t0, n, n_ctx, keep_rest)


def kernel(x, c, ctx, c_ctx, ada_w, ada_b, norm_g, w_in, w_out, s5_lam_re, s5_lam_im, s5_log_dt, s5_b_re, s5_b_im, s5_c_re, s5_c_im, s5_d, s5_glu_w, s5_glu_b, na_rpb, rk_mu, rk_w0, rk_w2, rk_a0, rk_a2, rk_g2, rk_k_k, rk_k_a, rk_r_k, rk_ln_w, rk_ln_b, ec_router, ec_w_gate, ec_w_up, ec_w_down):
    bsz, n, d = x.shape
    n_ctx = ctx.shape[1]
    rows = n // GRID_W
    xc = jnp.concatenate([ctx, x], axis=1)
    cos, sin = _rope_tables(n_ctx, n)

    for l in range(DEPTH):
        need_ctx = l < DEPTH - 1
        mod = jax.nn.silu(c) @ ada_w[l] + ada_b[l]
        mod_c = jnp.broadcast_to(jax.nn.silu(c_ctx) @ ada_w[l] + ada_b[l], mod.shape)
        both = jnp.stack([mod_c, mod], axis=1).reshape(bsz, 2, 6, 1, d)
        mods1 = jnp.stack([both[:, :, 1], both[:, :, 0]], axis=2)
        mods2 = jnp.stack([both[:, :, 4], both[:, :, 3]], axis=2)
        gates1 = both[:, :, 2:3]

        vec = jnp.stack([rk_k_k[l], rk_k_a[l], rk_r_k[l].reshape(-1), rk_w0[l, 0], rk_w0[l, 1],
                         rk_a0[l, 0], rk_a0[l, 1], jnp.zeros_like(rk_k_k[l])])
        u, qkv, com, dirs, g_rk, bonus = _inproj(xc, norm_g[l, 0], mods1, _bf(w_in[l]), cos, sin, rk_mu[l], vec,
                                                 _bf(rk_w2[l]), _bf(rk_a2[l]), _bf(rk_g2[l]), n_ctx)

        lam, bblk, cblk = _s5_params(s5_lam_re[l], s5_lam_im[l], s5_log_dt[l], s5_b_re[l], s5_b_im[l],
                                     s5_c_re[l], s5_c_im[l])
        sf, sb = _s5_scan(u, lam, bblk, cblk, bsz, n_ctx)

        bias, off_idx = _na_bias_table(na_rpb[l], rows)
        y_na = _na_attention(qkv, bias, off_idx, n_ctx)

        rf, rb = _rwkv_scan(com, dirs, n_ctx)

        xc, h2, aff = _mixout(u, sf, sb, jnp.stack([s5_d[l], s5_glu_b[l]]),
                              _bf(s5_glu_w[l]), y_na, rf, rb, g_rk, bonus, jnp.stack([rk_ln_w[l], rk_ln_b[l]]),
                              _bf(w_out[l]), norm_g[l, 1:4], gates1, xc, mods2, _bf(ec_router[l].T), n_ctx)
        wg, wu, wd = (_expert_weights_bf16(w, l) for w in (ec_w_gate, ec_w_up, ec_w_down))
        slots, lo = _select(aff, ((n_ctx, n), (0, n_ctx)) if need_ctx else ((n_ctx, n),), _token_tile(n_ctx))
        slots_t = jnp.swapaxes(slots, 1, 2)
        xc = _expert_choice(xc, h2, aff, slots, slots_t, lo, wg, wu, wd, norm_g[l, 3], both[:, 1, 5, 0], n_ctx, n, n_ctx,
                            keep_rest=need_ctx)
        if need_ctx:
            xc = _expert_choice(xc, h2, aff, slots, slots_t, lo, wg, wu, wd, norm_g[l, 3], both[:, 0, 5, 0], 0, n_ctx,
                                n_ctx)
    return xc
```

```python
import functools
import math

import jax
import jax.numpy as jnp
import numpy as np
from jax import lax
from jax.experimental import pallas as pl
from jax.experimental.pallas import tpu as pltpu

D_MODEL = 1024
DEPTH = 2
GRID_W = 64
D_MIX = D_MODEL
HEAD_DIM = 64
S5_WIDTH = D_MIX // 4
S5_GROUP = 16
S5_GROUPS = S5_WIDTH // S5_GROUP
S5_STATE = 64
NA_WIDTH = (D_MIX - S5_WIDTH) // 2
NA_HEADS = NA_WIDTH // HEAD_DIM
NA_WIN_R = 8
NA_WIN_C = 16
ROPE_BASE = 10000.0
RK_WIDTH = D_MIX - S5_WIDTH - NA_WIDTH
RK_HEADS = RK_WIDTH // HEAD_DIM
RK_DECAY_RANK = 64
RK_A_RANK = 64
RK_GATE_RANK = 128
RK_IN_WIDTH = 3 * RK_WIDTH + 2 * RK_DECAY_RANK + 2 * RK_A_RANK + RK_GATE_RANK
RK_GN_EPS = 64e-5
N_IN = S5_WIDTH + 3 * NA_WIDTH + RK_IN_WIDTH
N_EXPERTS = 16
EC_CAPACITY_FACTOR = 2
RMS_EPS = 1e-6
NEG_INF = -1e30

VMEM_LIMIT_BYTES = 56 * 1024 * 1024
EXPERT_VMEM_LIMIT_BYTES = 62 * 1024 * 1024
TOKEN_TILE = 256
S5_GP = S5_GROUPS * S5_STATE
S5_CHUNK = 64
S5_LANES = 512
RK_CHUNK = 64
RK_SAMPLES = 4
NA_QROWS = 4
FFN_ROWS = 512
FFN_FCHUNK = 768
CAST_ELEMS = 1024 * 1408
SEL_LANES = 256
COMBINE_WINDOW = 96


def _compiler_params(semantics, vmem_limit_bytes=VMEM_LIMIT_BYTES):
    return pltpu.CompilerParams(dimension_semantics=semantics, vmem_limit_bytes=vmem_limit_bytes)


def _token_tile(n_ctx):
    return min(TOKEN_TILE, n_ctx)


def _mod_specs(bsz, d, n_ctx):
    cb = n_ctx // _token_tile(n_ctx)
    return [pl.BlockSpec((None, None, None, 1, d), lambda b, i, j=j: (b, jnp.minimum(i // cb, 1), j, 0, 0))
            for j in range(2)]


def _dotf(x, y):
    return jnp.dot(x, y, preferred_element_type=jnp.float32)


def _bf(x):
    return x.astype(jnp.bfloat16)


def _rope_tables(n_ctx, n):
    t = np.arange(n)
    nf = HEAD_DIM // 4
    inv_freq = ROPE_BASE ** (-np.arange(nf, dtype=np.float32) / nf)
    pos = np.stack([(t // GRID_W).astype(np.float32), (t % GRID_W).astype(np.float32)], axis=1)
    ang = pos[:, :, None] * inv_freq[None, None, :]
    cos = np.repeat(np.cos(ang), 2, axis=1).reshape(n, HEAD_DIM)
    sin = np.sin(ang)
    sin = np.stack([-sin[:, 0], sin[:, 0], -sin[:, 1], sin[:, 1]], axis=1).reshape(n, HEAD_DIM)
    cos = np.concatenate([np.ones((n_ctx, HEAD_DIM), np.float32), cos.astype(np.float32)], axis=0)
    sin = np.concatenate([np.zeros((n_ctx, HEAD_DIM), np.float32), sin.astype(np.float32)], axis=0)
    reps = 2 * NA_HEADS
    return jnp.asarray(np.tile(cos, (1, reps))), jnp.asarray(np.tile(sin, (1, reps)))


def _inproj_kernel(x_ref, xp_ref, xn_ref, g_ref, sc_ref, sh_ref, w_ref, cos_ref, sin_ref,
                   mu_ref, vec_ref, w2_ref, a2_ref, g2_ref, ones_ref,
                   u_ref, qkv_ref, com_ref, dir_ref, grk_ref, bonus_ref, *, tn, n_ctx):
    x = jnp.concatenate([xp_ref[...], x_ref[...], xn_ref[...]], axis=0)
    y = x * lax.rsqrt(jnp.mean(x * x, axis=-1, keepdims=True) + RMS_EPS)
    h_ext = _bf((y * g_ref[...]) * (1.0 + sc_ref[...]) + sh_ref[...])
    z_ext = _dotf(h_ext, w_ref[:, S5_WIDTH + 3 * NA_WIDTH:])
    t0 = pl.program_id(1) * tn
    nt = pl.num_programs(1) * tn
    keep_prev = jnp.where((t0 == 0) | (t0 == n_ctx), 0.0, 1.0)
    keep_next = jnp.where((t0 + tn == n_ctx) | (t0 + tn == nt), 0.0, 1.0)
    _rk_prepare(z_ext[8:8 + tn], z_ext[7:8] * keep_prev, z_ext[8 + tn:9 + tn] * keep_next, mu_ref, vec_ref,
                w2_ref, a2_ref, g2_ref, ones_ref, com_ref, dir_ref, grk_ref, bonus_ref)
    h = h_ext[8:8 + tn]
    u_ref[...] = _dotf(h, w_ref[:, :S5_WIDTH])
    qk = _dotf(h, w_ref[:, S5_WIDTH:S5_WIDTH + 2 * NA_WIDTH])
    nf = HEAD_DIM // 4
    lane = lax.broadcasted_iota(jnp.int32, qk.shape, 1)
    first = (lane & (2 * nf - 1)) < nf
    partner = jnp.where(first, pltpu.roll(qk, qk.shape[1] - nf, axis=1), pltpu.roll(qk, nf, axis=1))
    qk = _bf(qk * cos_ref[...] + partner * sin_ref[...])
    v = _bf(_dotf(h, w_ref[:, S5_WIDTH + 2 * NA_WIDTH:S5_WIDTH + 3 * NA_WIDTH]))
    for hd in range(NA_HEADS):
        lo = hd * HEAD_DIM
        qkv_ref[0, hd] = qk[:, lo:lo + HEAD_DIM]
        qkv_ref[1, hd] = qk[:, NA_WIDTH + lo:NA_WIDTH + lo + HEAD_DIM]
        qkv_ref[2, hd] = v[:, lo:lo + HEAD_DIM]


def _inproj(xc, g, mods, w_bf16, cos, sin, mu, vec, w2, a2, g2, n_ctx):
    bsz, nt, d = xc.shape
    tn = _token_tile(n_ctx)
    tb = tn // 8
    nb = nt // 8
    ones = _head_sum_matrix()
    mu2 = mu.reshape(1, RK_IN_WIDTH)

    def full(a):
        return pl.BlockSpec(a.shape, lambda b, i: (0,) * a.ndim)

    S = jax.ShapeDtypeStruct
    return pl.pallas_call(
        functools.partial(_inproj_kernel, tn=tn, n_ctx=n_ctx),
        grid=(bsz, nt // tn),
        in_specs=[pl.BlockSpec((None, tn, d), lambda b, i: (b, i, 0)),
                  pl.BlockSpec((None, 8, d), lambda b, i: (b, jnp.maximum(i * tb - 1, 0), 0)),
                  pl.BlockSpec((None, 8, d), lambda b, i: (b, jnp.minimum((i + 1) * tb, nb - 1), 0)),
                  pl.BlockSpec((1, d), lambda b, i: (0, 0)),
                  *_mod_specs(bsz, d, n_ctx),
                  pl.BlockSpec((d, N_IN), lambda b, i: (0, 0)),
                  pl.BlockSpec((tn, 2 * NA_WIDTH), lambda b, i: (i, 0)),
                  pl.BlockSpec((tn, 2 * NA_WIDTH), lambda b, i: (i, 0)),
                  full(mu2), full(vec), full(w2), full(a2), full(g2), full(ones)],
        out_specs=[pl.BlockSpec((tn, S5_WIDTH), lambda b, i: (i, b)),
                   pl.BlockSpec((None, 3, NA_HEADS, tn, HEAD_DIM), lambda b, i: (b, 0, 0, i, 0)),
                   pl.BlockSpec((None, 3, RK_HEADS, tn, HEAD_DIM), lambda b, i: (b, 0, 0, i, 0)),
                   pl.BlockSpec((None, 2, 3, RK_HEADS, tn, HEAD_DIM), lambda b, i: (b, 0, 0, 0, i, 0)),
                   pl.BlockSpec((None, tn, RK_WIDTH), lambda b, i: (b, i, 0)),
                   pl.BlockSpec((None, tn, RK_WIDTH), lambda b, i: (b, i, 0))],
        out_shape=[S((nt, bsz * S5_WIDTH), jnp.float32),
                   S((bsz, 3, NA_HEADS, nt, HEAD_DIM), jnp.bfloat16),
                   S((bsz, 3, RK_HEADS, nt, HEAD_DIM), jnp.float32),
                   S((bsz, 2, 3, RK_HEADS, nt, HEAD_DIM), jnp.float32),
                   S((bsz, nt, RK_WIDTH), jnp.float32), S((bsz, nt, RK_WIDTH), jnp.float32)],
        compiler_params=_compiler_params(("parallel", "parallel")),
    )(xc, xc, xc, g.reshape(1, d), mods, mods, w_bf16, cos, sin, mu2, vec, w2, a2, g2, ones)


def _s5_kernel(uf_ref, ub_ref, lam_ref, bw_ref, cw_ref, yf_ref, yb_ref, sbuf, st, *, T, B):
    c = pl.program_id(0)

    @pl.when(c == 0)
    def _():
        st[...] = jnp.zeros_like(st)

    for d, u_ref in enumerate((uf_ref, ub_ref)):
        sbuf[d] = _dotf(_bf(u_ref[...].reshape(T * B, S5_WIDTH)), bw_ref[d])
    for d in range(2):
        for h in range(S5_GP // S5_LANES):
            re_sl = pl.ds(h * S5_LANES, S5_LANES)
            im_sl = pl.ds(S5_GP + h * S5_LANES, S5_LANES)
            lr = jnp.broadcast_to(lam_ref[d, 0:1, h * S5_LANES:(h + 1) * S5_LANES], (B, S5_LANES))
            li = jnp.broadcast_to(lam_ref[d, 1:2, h * S5_LANES:(h + 1) * S5_LANES], (B, S5_LANES))
            s_re, s_im = st[d, :, re_sl], st[d, :, im_sl]
            for i in range(T):
                t = i if d == 0 else T - 1 - i
                rows = pl.ds(t * B, B)
                s_re, s_im = (lr * s_re - li * s_im + sbuf[d, rows, re_sl],
                              lr * s_im + li * s_re + sbuf[d, rows, im_sl])
                sbuf[d, rows, re_sl] = s_re
                sbuf[d, rows, im_sl] = s_im
            st[d, :, re_sl] = s_re
            st[d, :, im_sl] = s_im
    for d, y_ref in enumerate((yf_ref, yb_ref)):
        y_ref[...] = _dotf(_bf(sbuf[d]), cw_ref[d]).reshape(T, B * S5_WIDTH)


def _backward_chunk(c, nc, ncc):
    return jnp.where(c < ncc, ncc - 1 - c, nc + ncc - 1 - c)


def _s5_scan(uf, lam, bblk, cblk, bsz, n_ctx):
    T = S5_CHUNK
    nc = uf.shape[0] // T
    ncc = n_ctx // T
    bw, cw = _bf(bblk), _bf(cblk)
    bmap = functools.partial(_backward_chunk, nc=nc, ncc=ncc)
    blk = (T, bsz * S5_WIDTH)

    def full(shape):
        return pl.BlockSpec(shape, lambda c: (0,) * len(shape))

    return pl.pallas_call(
        functools.partial(_s5_kernel, T=T, B=bsz),
        grid=(nc,),
        in_specs=[pl.BlockSpec(blk, lambda c: (c, 0)), pl.BlockSpec(blk, lambda c: (bmap(c), 0)),
                  full(lam.shape), full(bw.shape), full(cw.shape)],
        out_specs=[pl.BlockSpec(blk, lambda c: (c, 0)), pl.BlockSpec(blk, lambda c: (bmap(c), 0))],
        out_shape=[jax.ShapeDtypeStruct(uf.shape, jnp.float32)] * 2,
        scratch_shapes=[pltpu.VMEM((2, T * bsz, 2 * S5_GP), jnp.float32),
                        pltpu.VMEM((2, bsz, 2 * S5_GP), jnp.float32)],
        compiler_params=_compiler_params(("arbitrary",)),
    )(uf, uf, lam, bw, cw)


def _s5_params(lam_re, lam_im, log_dt, b_re, b_im, c_re, c_im):
    lams, bs, cs = [], [], []
    eye = jnp.eye(S5_GROUPS, dtype=jnp.float32)
    for d in range(2):
        dt = jnp.exp(log_dt[d])[:, None]
        mag = jnp.exp(lam_re[d] * dt)
        lb_re, lb_im = mag * jnp.cos(lam_im[d] * dt), mag * jnp.sin(lam_im[d] * dt)
        den = lam_re[d] ** 2 + lam_im[d] ** 2
        nr, ni = lb_re - 1.0, lb_im
        f_re = (nr * lam_re[d] + ni * lam_im[d]) / den
        f_im = (ni * lam_re[d] - nr * lam_im[d]) / den
        bb_re = f_re[..., None] * b_re - f_im[..., None] * b_im
        bb_im = f_re[..., None] * b_im + f_im[..., None] * b_re

        def blockdiag_in(m):
            return jnp.einsum('gph,gk->ghkp', m, eye).reshape(S5_WIDTH, S5_GP)

        def blockdiag_out(m):
            return jnp.einsum('ghp,gk->gpkh', m, eye).reshape(S5_GP, S5_WIDTH)

        bs.append(jnp.concatenate([blockdiag_in(bb_re), blockdiag_in(bb_im)], axis=1))
        cs.append(jnp.concatenate([blockdiag_out(c_re[d]), -blockdiag_out(c_im[d])], axis=0))
        lams.append(jnp.stack([lb_re.reshape(S5_GP), lb_im.reshape(S5_GP)]))
    return jnp.stack(lams), jnp.stack(bs), jnp.stack(cs)


def _na_offsets(rows):
    win_r = min(NA_WIN_R, rows)
    i = np.arange(rows)
    r0 = np.clip(i - win_r // 2, 0, rows - win_r)
    return r0 - i + NA_WIN_R - 1, win_r


def _na_bias_table(rpb, rows):
    off, win_r = _na_offsets(rows)
    offs = np.unique(off)
    qc = np.arange(GRID_W)[:, None]
    kc = np.arange(GRID_W)[None, :]
    c0 = np.clip(qc - NA_WIN_C // 2, 0, GRID_W - NA_WIN_C)
    valid = (kc >= c0) & (kc < c0 + NA_WIN_C)
    dc = np.clip(kc - qc, 1 - NA_WIN_C, NA_WIN_C - 1) + NA_WIN_C - 1
    full = jnp.where(valid[None, :, None, :], rpb[:, :, dc].transpose(0, 2, 1, 3), NEG_INF)
    full = full.reshape(rpb.shape[0], GRID_W, -1)
    tabs = [full[:, :, o * GRID_W:(o + win_r) * GRID_W] for o in offs]
    return jnp.stack(tabs, axis=1), jnp.asarray(off - offs[0], jnp.int32)


_NT_DIMS = (((1,), (1,)), ((), ()))


def _softmax_pv(s_list, v_list):
    units = range(len(s_list[0]))
    m = [functools.reduce(jnp.maximum, [jnp.max(s[u], axis=-1, keepdims=True) for s in s_list]) for u in units]
    p = [[jnp.exp(s[u] - m[u]) for u in units] for s in s_list]
    den = [sum(jnp.sum(pj[u], axis=-1, keepdims=True) for pj in p) for u in units]
    o = [sum(_dotf(_bf(pj[u]), vj[u]) for pj, vj in zip(p, v_list)) for u in units]
    return [o[u] / den[u] for u in units]


def _na_kernel(off_ref, q_ref, k_ref, v_ref, *rest, n_ctx, win_r, rows):
    bias_refs, o_ref = rest[:-1], rest[-1]
    Q = len(bias_refs)
    j = pl.program_id(1) * Q
    cb = n_ctx // GRID_W
    H = q_ref.shape[0]
    units = [(qi, h) for qi in range(Q) for h in range(H)]
    scale = HEAD_DIM ** -0.5

    def scores(qi, h, start, size):
        return lax.dot_general(q_ref[h, qi * GRID_W:(qi + 1) * GRID_W, :], k_ref[h, pl.ds(start, size), :], _NT_DIMS,
                               preferred_element_type=jnp.float32) * scale

    def store(o):
        for qi in range(Q):
            o_ref[qi * GRID_W:(qi + 1) * GRID_W, :] = jnp.concatenate(o[qi * H:(qi + 1) * H], axis=-1)

    @pl.when(j < cb)
    def _():
        s = [scores(qi, h, 0, n_ctx) for qi, h in units]
        store(_softmax_pv([s], [[v_ref[h, pl.ds(0, n_ctx), :] for _, h in units]]))

    @pl.when(j >= cb)
    def _():
        nk = win_r * GRID_W
        start = [pl.multiple_of(n_ctx + jnp.clip(j - cb + qi - win_r // 2, 0, rows - win_r) * GRID_W, GRID_W)
                 for qi in range(Q)]
        s_lat = [scores(qi, h, start[qi], nk) + bias_refs[qi][h] for qi, h in units]
        s_ctx = [scores(qi, h, 0, n_ctx) for qi, h in units]
        store(_softmax_pv([s_lat, s_ctx], [[v_ref[h, pl.ds(start[qi], nk), :] for qi, h in units],
                                           [v_ref[h, pl.ds(0, n_ctx), :] for _, h in units]]))


def _na_attention(qkv, bias, off_idx, n_ctx):
    bsz, _, H, nt, _ = qkv.shape
    rows = (nt - n_ctx) // GRID_W
    win_r = min(NA_WIN_R, rows)
    cb = n_ctx // GRID_W
    Q = NA_QROWS if (cb % NA_QROWS == 0 and rows % NA_QROWS == 0) else 1

    def bias_spec(qi):
        return pl.BlockSpec((H, None, GRID_W, win_r * GRID_W),
                            lambda b, j, off: (0, off[jnp.maximum(j * Q + qi - cb, 0)], 0, 0))

    return pl.pallas_call(
        functools.partial(_na_kernel, n_ctx=n_ctx, win_r=win_r, rows=rows),
        grid_spec=pltpu.PrefetchScalarGridSpec(
            num_scalar_prefetch=1,
            grid=(bsz, (cb + rows) // Q),
            in_specs=[pl.BlockSpec((None, None, H, Q * GRID_W, HEAD_DIM), lambda b, j, off: (b, 0, 0, j, 0)),
                      pl.BlockSpec((None, None, H, nt, HEAD_DIM), lambda b, j, off: (b, 1, 0, 0, 0)),
                      pl.BlockSpec((None, None, H, nt, HEAD_DIM), lambda b, j, off: (b, 2, 0, 0, 0)),
                      *[bias_spec(qi) for qi in range(Q)]],
            out_specs=pl.BlockSpec((None, Q * GRID_W, H * HEAD_DIM), lambda b, j, off: (b, j, 0)),
        ),
        out_shape=jax.ShapeDtypeStruct((bsz, nt, H * HEAD_DIM), jnp.float32),
        compiler_params=_compiler_params(("parallel", "arbitrary")),
    )(off_idx, qkv, qkv, qkv, *([bias] * Q))


def _head_sum_matrix():
    i = np.arange(RK_WIDTH)
    return jnp.asarray((i[:, None] // HEAD_DIM) == (i[None, :] // HEAD_DIM), jnp.bfloat16)


def _head_sums(x, ones):
    hi = x.astype(jnp.bfloat16)
    lo = (x - hi.astype(jnp.float32)).astype(jnp.bfloat16)
    return _dotf(hi, ones) + _dotf(lo, ones)


def _rk_prepare(z, z_prev, z_next, mu_ref, vec_ref, w2_ref, a2_ref, g2_ref, ones_ref,
                com_ref, dir_ref, g_ref, bonus_ref):
    tn = z.shape[0]
    row = lax.broadcasted_iota(jnp.int32, z.shape, 0)
    prev = jnp.where(row == 0, z_prev, pltpu.roll(z, 1, axis=0))
    nxt = jnp.where(row == tn - 1, z_next, pltpu.roll(z, tn - 1, axis=0))
    zs = z + (0.5 * (prev + nxt) - z) * mu_ref[...]
    W, R = RK_WIDTH, RK_DECAY_RANK
    r, k, v = zs[:, :W], zs[:, W:2 * W], zs[:, 2 * W:3 * W]
    o = 3 * W
    zw = (zs[:, o:o + R], zs[:, o + R:o + 2 * R])
    za = (zs[:, o + 2 * R:o + 3 * R], zs[:, o + 3 * R:o + 4 * R])
    zg = zs[:, o + 4 * R:]
    k_k, k_a, r_k = vec_ref[0:1, :], vec_ref[1:2, :], vec_ref[2:3, :]
    ones = ones_ref[...]
    g_ref[...] = _dotf(_bf(jax.nn.sigmoid(zg)), g2_ref[...])
    kk = k * k_k
    kk = kk * lax.rsqrt(jnp.maximum(_head_sums(kk * kk, ones), 1e-24))
    bonus = 0.0
    fields = [r, kk, v]
    for d in range(2):
        w = -jax.nn.softplus(-(vec_ref[3 + d:4 + d, :] + _dotf(_bf(jnp.tanh(zw[d])), w2_ref[d]))) - 0.5
        a = jax.nn.sigmoid(vec_ref[5 + d:6 + d, :] + _dotf(_bf(za[d]), a2_ref[d]))
        kd = k * (1.0 + (a - 1.0) * k_a)
        bonus = bonus + _head_sums(r * kd * r_k, ones)
        fields += [-jnp.exp(w), kd, kk * a]
    bonus_ref[...] = bonus * v
    for j, t in enumerate(fields):
        for h in range(RK_HEADS):
            blk = t[:, h * HEAD_DIM:(h + 1) * HEAD_DIM]
            if j < 3:
                com_ref[j, h] = blk
            else:
                dir_ref[(j - 3) // 3, (j - 3) % 3, h] = blk


def _bdot(x, y):
    return _dotf(_bf(x), _bf(y))


def _bdot_nt(x, y):
    return lax.dot_general(_bf(x), _bf(y), _NT_DIMS, preferred_element_type=jnp.float32)


def _bdot_tn(x, y):
    return lax.dot_general(_bf(x), _bf(y), (((0,), (0,)), ((), ())), preferred_element_type=jnp.float32)


def _rwkv_units(units):
    T = units[0][0].shape[0]
    U = range(len(units))
    steps = int(math.log2(T))
    ti = lax.broadcasted_iota(jnp.int32, (T, T), 0)
    si = lax.broadcasted_iota(jnp.int32, (T, T), 1)
    ti2 = lax.broadcasted_iota(jnp.int32, (T, 2 * T), 0)
    si2 = lax.broadcasted_iota(jnp.int32, (T, 2 * T), 1) & (T - 1)
    eye = (lax.broadcasted_iota(jnp.int32, (HEAD_DIM, HEAD_DIM), 0)
           == lax.broadcasted_iota(jnp.int32, (HEAD_DIM, HEAD_DIM), 1))
    masks = {}
    for rev in (False, True):
        incl = (si >= ti) if rev else (si <= ti)
        masks[rev] = (incl.astype(jnp.bfloat16), (si2 >= ti2) if rev else (si2 <= ti2),
                      (si2 > ti2) if rev else (si2 < ti2))
    cum = []
    for (r, kk, v, lw, kd, b, h0, rev) in units:
        l1 = lw.astype(jnp.bfloat16)
        rem = lw - l1.astype(jnp.float32)
        l2 = rem.astype(jnp.bfloat16)
        l3 = (rem - l2.astype(jnp.float32)).astype(jnp.bfloat16)
        tri = masks[rev][0]
        cum.append(_dotf(tri, l1) + (_dotf(tri, l2) + _dotf(tri, l3)))
    at, rt, p, cl = [], [], [], []
    for u, (r, kk, v, lw, kd, b, h0, rev) in enumerate(units):
        c = cum[u]
        cl.append(c[0:1, :] if rev else c[T - 1:T, :])
        e_neg = jnp.exp(-c)
        at.append(-kk * jnp.exp(c - lw))
        rt.append(r * jnp.exp(c))
        p.append(_bdot_nt(jnp.concatenate([at[u], rt[u]], axis=0), jnp.concatenate([b * e_neg, kd * e_neg], axis=0)))
    top = [jnp.where(masks[units[u][7]][2], p[u][:T], 0.0) for u in U]
    l2m = [jnp.where(masks[units[u][7]][1], p[u][T:], 0.0) for u in U]
    npow = [top[u][:, :T] for u in U]
    x = [jnp.concatenate([at[u], _bdot(top[u][:, T:], units[u][2])], axis=1) for u in U]
    for i in range(steps):
        x = [x[u] + _bdot(npow[u], x[u]) for u in U]
        if i < steps - 1:
            npow = [_bdot(npow[u], npow[u]) for u in U]
    z = [jnp.concatenate([x[u], jnp.concatenate([jnp.zeros_like(units[u][2]), units[u][2]], axis=1)], axis=0)
         for u in U]
    ry = [_bdot(l2m[u], z[u]) for u in U]
    gj = []
    for u, (r, kk, v, lw, kd, b, h0, rev) in enumerate(units):
        e_end = jnp.exp(cl[u] - cum[u])
        gj.append(_bdot_tn(jnp.concatenate([b * e_end, kd * e_end], axis=0), z[u]))
    out = []
    for u in U:
        g = jnp.where(eye, jnp.exp(cl[u]), 0.0) + gj[u][:, :HEAD_DIM]
        yh = _bdot(jnp.concatenate([rt[u] + ry[u][:, :HEAD_DIM], g], axis=0), units[u][6])
        out.append((yh[:T] + ry[u][:, HEAD_DIM:], yh[T:] + gj[u][:, HEAD_DIM:]))
    return out


def _rwkv_kernel(cf_ref, cb_ref, df_ref, db_ref, yf_ref, yb_ref, hs):
    c = pl.program_id(1)
    S, H = cf_ref.shape[0], cf_ref.shape[2]

    @pl.when(c == 0)
    def _():
        hs[...] = jnp.zeros_like(hs)

    units, where = [], []
    for s in range(S):
        for d, (c_ref, d_ref) in enumerate(((cf_ref, df_ref), (cb_ref, db_ref))):
            for h in range(H):
                units.append((c_ref[s, 0, h], c_ref[s, 1, h], c_ref[s, 2, h],
                              d_ref[s, 0, h], d_ref[s, 1, h], d_ref[s, 2, h], hs[s, d, h], d == 1))
                where.append((s, d, h))
    for (s, d, h), (y, hn) in zip(where, _rwkv_units(units)):
        (yf_ref, yb_ref)[d][s, h] = y
        hs[s, d, h] = hn


def _rwkv_scan(com, dirs, n_ctx):
    bsz, _, H, nt, _ = com.shape
    T = RK_CHUNK
    S = RK_SAMPLES if bsz % RK_SAMPLES == 0 else 1
    bmap = functools.partial(_backward_chunk, nc=nt // T, ncc=n_ctx // T)
    cblk = (S, 3, H, T, HEAD_DIM)
    dblk = (S, None, 3, H, T, HEAD_DIM)
    oblk = (S, H, T, HEAD_DIM)
    return pl.pallas_call(
        _rwkv_kernel,
        grid=(bsz // S, nt // T),
        in_specs=[pl.BlockSpec(cblk, lambda b, c: (b, 0, 0, c, 0)),
                  pl.BlockSpec(cblk, lambda b, c: (b, 0, 0, bmap(c), 0)),
                  pl.BlockSpec(dblk, lambda b, c: (b, 0, 0, 0, c, 0)),
                  pl.BlockSpec(dblk, lambda b, c: (b, 1, 0, 0, bmap(c), 0))],
        out_specs=[pl.BlockSpec(oblk, lambda b, c: (b, 0, c, 0)),
                   pl.BlockSpec(oblk, lambda b, c: (b, 0, bmap(c), 0))],
        out_shape=[jax.ShapeDtypeStruct((bsz, H, nt, HEAD_DIM), jnp.float32)] * 2,
        scratch_shapes=[pltpu.VMEM((S, 2, H, HEAD_DIM, HEAD_DIM), jnp.float32)],
        compiler_params=_compiler_params(("parallel", "arbitrary")),
    )(com, com, dirs, dirs)


def _mixout_kernel(u_ref, sf_ref, sb_ref, s5v_ref, gw_ref, na_ref, rf_ref, rb_ref, grk_ref, bonus_ref, ln_ref,
                   w_ref, ng_ref, gt_ref, x_ref, sc_ref, sh_ref, wrt_ref, o_ref, h_ref, aff_ref):
    y5 = jax.nn.gelu(s5v_ref[0:1, :] * u_ref[...] + sf_ref[...] + sb_ref[...])
    y5 = y5 * jax.nn.sigmoid(_dotf(_bf(y5), gw_ref[...]) + s5v_ref[1:2, :])
    outs = []
    for h in range(RK_HEADS):
        y = rf_ref[h] + rb_ref[h]
        yc = y - jnp.mean(y, axis=-1, keepdims=True)
        var = jnp.mean(yc * yc, axis=-1, keepdims=True)
        outs.append(yc * lax.rsqrt(var + RK_GN_EPS))
    yrk = (jnp.concatenate(outs, axis=-1) * ln_ref[0:1, :] + ln_ref[1:2, :] + bonus_ref[...]) * grk_ref[...]
    z = (_dotf(_bf(y5), w_ref[:S5_WIDTH, :])
         + _dotf(_bf(na_ref[...]), w_ref[S5_WIDTH:S5_WIDTH + NA_WIDTH, :])
         + _dotf(_bf(yrk), w_ref[S5_WIDTH + NA_WIDTH:, :]))
    zn = z * lax.rsqrt(jnp.mean(z * z, axis=-1, keepdims=True) + RMS_EPS)
    x = x_ref[...] + gt_ref[...] * (zn * ng_ref[0:1, :])
    o_ref[...] = x
    y = x * lax.rsqrt(jnp.mean(x * x, axis=-1, keepdims=True) + RMS_EPS)
    h = _bf((y * ng_ref[1:2, :]) * (1.0 + sc_ref[...]) + sh_ref[...])
    h_ref[...] = h
    logits = lax.dot_general(wrt_ref[...], h, _NT_DIMS, preferred_element_type=jnp.float32)
    e = jnp.exp(logits - jnp.max(logits, axis=0, keepdims=True))
    aff_ref[...] = e / jnp.sum(e, axis=0, keepdims=True)


def _mixout(u, sf, sb, s5v, glu_w, y_na, rf, rb, g_rk, bonus, ln, w_out, ng, gates, xc, mods, wrt, n_ctx):
    bsz, nt, d = xc.shape
    tn = _token_tile(n_ctx)
    cb = n_ctx // tn
    ne = wrt.shape[0]

    def tok(w):
        return pl.BlockSpec((None, tn, w), lambda b, i: (b, i, 0))

    def full(a):
        return pl.BlockSpec(a.shape, lambda b, i: (0,) * a.ndim)

    tm = pl.BlockSpec((tn, S5_WIDTH), lambda b, i: (i, b))
    yblk = pl.BlockSpec((None, RK_HEADS, tn, HEAD_DIM), lambda b, i: (b, 0, i, 0))
    return pl.pallas_call(
        _mixout_kernel,
        grid=(bsz, nt // tn),
        in_specs=[tm, tm, tm, full(s5v), full(glu_w), tok(NA_WIDTH), yblk, yblk, tok(RK_WIDTH), tok(RK_WIDTH),
                  full(ln), full(w_out), full(ng),
                  pl.BlockSpec((None, None, None, 1, d), lambda b, i: (b, jnp.minimum(i // cb, 1), 0, 0, 0)),
                  tok(d), *_mod_specs(bsz, d, n_ctx), full(wrt)],
        out_specs=[tok(d), tok(d), pl.BlockSpec((None, ne, tn), lambda b, i: (b, 0, i))],
        out_shape=[jax.ShapeDtypeStruct(xc.shape, jnp.float32), jax.ShapeDtypeStruct(xc.shape, jnp.bfloat16),
                   jax.ShapeDtypeStruct((bsz, ne, nt), jnp.float32)],
        compiler_params=_compiler_params(("parallel", "parallel")),
    )(u, sf, sb, s5v, glu_w, y_na, rf, rb, g_rk, bonus, ln, w_out, ng, gates, xc, mods, mods, wrt)


def _cast_kernel(w_ref, o_ref):
    o_ref[...] = _bf(w_ref[...])


def _expert_weights_bf16(w, l):
    _, ne, r, c = w.shape
    nblk = r * c // CAST_ELEMS
    tr = r // nblk
    return pl.pallas_call(
        _cast_kernel,
        grid=(ne, nblk),
        in_specs=[pl.BlockSpec((None, None, tr, c), lambda e, i: (l, e, i, 0))],
        out_specs=pl.BlockSpec((None, tr, c), lambda e, i: (e, i, 0)),
        out_shape=jax.ShapeDtypeStruct((ne, r, c), jnp.bfloat16),
        compiler_params=_compiler_params(("parallel", "parallel")),
    )(w)


def _capacity(n):
    return EC_CAPACITY_FACTOR * n // N_EXPERTS


def _prefix_count(m):
    ne, n = m.shape
    s = lax.broadcasted_iota(jnp.int32, (SEL_LANES, SEL_LANES), 0)
    t = lax.broadcasted_iota(jnp.int32, (SEL_LANES, SEL_LANES), 1)
    tri = _bf(s < t)
    out, carry = [], jnp.zeros((ne, 1), jnp.float32)
    for j in range(n // SEL_LANES):
        blk = m[:, j * SEL_LANES:(j + 1) * SEL_LANES]
        out.append(_dotf(_bf(blk), tri) + carry)
        carry = carry + jnp.sum(blk, axis=1, keepdims=True)
    return jnp.concatenate(out, axis=1)


def _select_segment(a, k):
    key = pltpu.bitcast(a, jnp.int32)

    def step(i, tau):
        cand = tau | (1 << (30 - i))
        cnt = jnp.sum((key >= cand).astype(jnp.float32), axis=1, keepdims=True)
        return jnp.where(cnt >= k, cand, tau)

    tau = lax.fori_loop(0, 31, step, jnp.zeros((a.shape[0], 1), jnp.int32))
    gt = key > tau
    eq = key == tau
    need = k - jnp.sum(gt.astype(jnp.float32), axis=1, keepdims=True)
    sel = gt | (eq & (_prefix_count(eq.astype(jnp.float32)) < need))
    pos = _prefix_count(sel.astype(jnp.float32))
    return jnp.where(sel, pos, -1.0), pos


def _select_kernel(a_ref, o_ref, lo_ref, *, segments, tn):
    if sum(n for _, n in segments) < o_ref.shape[1]:
        o_ref[...] = jnp.full(o_ref.shape, -1.0, o_ref.dtype)
        lo_ref[...] = jnp.zeros(lo_ref.shape, lo_ref.dtype)
    for (t0, n) in segments:
        slots, pos = _select_segment(a_ref[:, t0:t0 + n], _capacity(n))
        o_ref[:, t0:t0 + n] = slots
        for j in range(n // tn):
            lo_ref[:, t0 // tn + j:t0 // tn + j + 1] = pos[:, j * tn:j * tn + 1].astype(jnp.int32)


def _select(aff, segments, tn):
    bsz, ne, nt = aff.shape
    return pl.pallas_call(
        functools.partial(_select_kernel, segments=segments, tn=tn),
        grid=(bsz,),
        in_specs=[pl.BlockSpec((None, ne, nt), lambda b: (b, 0, 0))],
        out_specs=[pl.BlockSpec((None, ne, nt), lambda b: (b, 0, 0)),
                   pl.BlockSpec((None, ne, nt // tn), lambda b: (b, 0, 0))],
        out_shape=[jax.ShapeDtypeStruct((bsz, ne, nt), jnp.float32),
                   jax.ShapeDtypeStruct((bsz, ne, nt // tn), jnp.int32)],
        compiler_params=_compiler_params(("parallel",)),
    )(aff)


def _expert_kernel(slot_ref, aff_ref, h_ref, wg_ref, wu_ref, wd_ref, y_ref, *, G, cap, n, t_in):
    e = pl.program_id(0)
    t0 = t_in
    want = lax.broadcasted_iota(jnp.int32, (cap, n), 0).astype(jnp.float32)
    hit = [slot_ref[s, pl.ds(e, 1), t0:t0 + n] == want for s in range(G)]
    xin = _bf(jnp.concatenate([_dotf(_bf(hit[s]), h_ref[s, pl.ds(t_in, n), :]) for s in range(G)], axis=0))
    acc = jnp.zeros((G * cap, wd_ref.shape[1]), jnp.float32)
    for c0 in range(0, wg_ref.shape[1], FFN_FCHUNK):
        cols = slice(c0, min(c0 + FFN_FCHUNK, wg_ref.shape[1]))
        hid = jax.nn.silu(_dotf(xin, wg_ref[:, cols])) * _dotf(xin, wu_ref[:, cols])
        acc = acc + _dotf(_bf(hid), wd_ref[cols, :])
    gate = jnp.concatenate([jnp.sum(jnp.where(hit[s], aff_ref[s, pl.ds(e, 1), t0:t0 + n], 0.0), axis=1, keepdims=True)
                            for s in range(G)], axis=0)
    y_ref[...] = _bf(acc * gate)


def _experts(slots, aff, h2, wg, wu, wd, t0, n):
    bsz, ne, nt = slots.shape
    d = h2.shape[2]
    f = wg.shape[2]
    cap = _capacity(n)
    G = max(1, min(bsz, FFN_ROWS // cap))
    if t0 % n == 0:
        hspec = pl.BlockSpec((G, n, d), lambda e, b: (b, t0 // n, 0))
        sspec = pl.BlockSpec((G, ne, n), lambda e, b: (b, 0, t0 // n))
        t_in = 0
    else:
        hspec = pl.BlockSpec((G, nt, d), lambda e, b: (b, 0, 0))
        sspec = pl.BlockSpec((G, ne, nt), lambda e, b: (b, 0, 0))
        t_in = t0

    def wspec(shape):
        return pl.BlockSpec((None,) + shape, lambda e, b: (e, 0, 0))

    return pl.pallas_call(
        functools.partial(_expert_kernel, G=G, cap=cap, n=n, t_in=t_in),
        grid=(ne, bsz // G),
        in_specs=[sspec, sspec, hspec, wspec((d, f)), wspec((d, f)), wspec((f, d))],
        out_specs=pl.BlockSpec((None, None, G * cap, d), lambda e, b: (b, e, 0, 0)),
        out_shape=jax.ShapeDtypeStruct((bsz // G, ne, G * cap, d), jnp.bfloat16),
        compiler_params=_compiler_params(("arbitrary", "arbitrary"), EXPERT_VMEM_LIMIT_BYTES),
    )(slots, aff, h2, wg, wu, wd)


def _combine_kernel(st_ref, ok_ref, slot_ref, y_ref, x_ref, g_ref, gt_ref, o_ref, *, t_lo, t_hi, first, ntiles, W):
    b, i = pl.program_id(0), pl.program_id(1)
    inside = (i >= t_lo) & (i < t_hi)
    ne, cap, d = y_ref.shape
    tn = x_ref.shape[0]
    tile = i + first
    fits = ok_ref[b * ntiles + tile] == 1

    def finish(f):
        fn = f * lax.rsqrt(jnp.mean(f * f, axis=-1, keepdims=True) + RMS_EPS)
        o_ref[...] = x_ref[...] + gt_ref[...] * (fn * g_ref[...])

    @pl.when(inside & fits)
    def _():
        lane = lax.broadcasted_iota(jnp.int32, (tn, W), 1)
        hits, rows = [], []
        for e in range(ne):
            st = pl.multiple_of(st_ref[(b * ne + e) * ntiles + tile], 16)
            hits.append(_bf(slot_ref[:, e:e + 1] == (lane + st).astype(jnp.float32)))
            rows.append(y_ref[e, pl.ds(st, W), :])
        finish(_dotf(jnp.concatenate(hits, axis=1), jnp.concatenate(rows, axis=0)))

    @pl.when(inside & jnp.logical_not(fits))
    def _():
        want = lax.broadcasted_iota(jnp.int32, (tn, cap), 1).astype(jnp.float32)
        hit = jnp.concatenate([_bf(slot_ref[:, e:e + 1] == want) for e in range(ne)], axis=1)
        finish(_dotf(hit, y_ref[...].reshape(ne * cap, d)))

    @pl.when(jnp.logical_not(inside))
    def _():
        o_ref[...] = x_ref[...]


def _combine(slots_t, lo, y, xc, g, gt, t0, n, n_ctx, rest):
    bsz, nt, ne = slots_t.shape
    d = xc.shape[2]
    cap = _capacity(n)
    G = bsz // y.shape[0]
    yb = y.reshape(bsz // G, ne, G, cap, d)
    tn = _token_tile(n_ctx)
    ntiles = nt // tn
    t_lo, t_hi = t0 // tn, (t0 + n) // tn
    W = min(COMBINE_WINDOW, cap)
    st = jnp.clip(lo // 16 * 16, 0, cap - W)
    end = jnp.concatenate([lo[:, :, 1:], jnp.full_like(lo[:, :, :1], cap)], axis=2)
    end = jnp.where(jnp.arange(ntiles) == t_hi - 1, cap, end)
    ok = jnp.all(end - st <= W, axis=1).astype(jnp.int32)
    first, tiles = (0, ntiles) if rest == "copy" else (t_lo, t_hi - t_lo)
    out_first = first if rest == "alias" else 0
    out_rows = nt if rest == "alias" else tiles * tn
    return pl.pallas_call(
        functools.partial(_combine_kernel, t_lo=t_lo - first, t_hi=t_hi - first, first=first, ntiles=ntiles, W=W),
        grid_spec=pltpu.PrefetchScalarGridSpec(
            num_scalar_prefetch=2,
            grid=(bsz, tiles),
            in_specs=[pl.BlockSpec((None, tn, ne), lambda b, i, *_: (b, i + first, 0)),
                      pl.BlockSpec((None, ne, None, cap, d), lambda b, i, *_: (b // G, 0, b % G, 0, 0)),
                      pl.BlockSpec((None, tn, d), lambda b, i, *_: (b, i + first, 0)),
                      pl.BlockSpec((1, d), lambda b, i, *_: (0, 0)),
                      pl.BlockSpec((None, 1, d), lambda b, i, *_: (b, 0, 0))],
            out_specs=pl.BlockSpec((None, tn, d), lambda b, i, *_: (b, i + out_first, 0)),
        ),
        out_shape=jax.ShapeDtypeStruct((bsz, out_rows, d), jnp.float32),
        input_output_aliases={4: 0} if rest == "alias" else {},
        compiler_params=_compiler_params(("parallel", "arbitrary")),
    )(st.reshape(-1), ok.reshape(-1), slots_t, yb, xc, g.reshape(1, d), gt.reshape(bsz, 1, d))


def _expert_choice(xc, h2, aff, slots, slots_t, lo, wg, wu, wd, g, gt, t0, n, n_ctx, rest):
    y = _experts(slots, aff, h2, wg, wu, wd, t0, n)
    return _combine(slots_t, lo, y, xc, g, gt, t0, n, n_ctx, rest)


def kernel(x, c, ctx, c_ctx, ada_w, ada_b, norm_g, w_in, w_out, s5_lam_re, s5_lam_im, s5_log_dt, s5_b_re, s5_b_im, s5_c_re, s5_c_im, s5_d, s5_glu_w, s5_glu_b, na_rpb, rk_mu, rk_w0, rk_w2, rk_a0, rk_a2, rk_g2, rk_k_k, rk_k_a, rk_r_k, rk_ln_w, rk_ln_b, ec_router, ec_w_gate, ec_w_up, ec_w_down):
    bsz, n, d = x.shape
    n_ctx = ctx.shape[1]
    rows = n // GRID_W
    xc = jnp.concatenate([ctx, x], axis=1)
    cos, sin = _rope_tables(n_ctx, n)

    for l in range(DEPTH):
        need_ctx = l < DEPTH - 1
        mod = jax.nn.silu(c) @ ada_w[l] + ada_b[l]
        mod_c = jnp.broadcast_to(jax.nn.silu(c_ctx) @ ada_w[l] + ada_b[l], mod.shape)
        both = jnp.stack([mod_c, mod], axis=1).reshape(bsz, 2, 6, 1, d)
        mods1 = jnp.stack([both[:, :, 1], both[:, :, 0]], axis=2)
        mods2 = jnp.stack([both[:, :, 4], both[:, :, 3]], axis=2)
        gates1 = both[:, :, 2:3]

        vec = jnp.stack([rk_k_k[l], rk_k_a[l], rk_r_k[l].reshape(-1), rk_w0[l, 0], rk_w0[l, 1],
                         rk_a0[l, 0], rk_a0[l, 1], jnp.zeros_like(rk_k_k[l])])
        u, qkv, com, dirs, g_rk, bonus = _inproj(xc, norm_g[l, 0], mods1, _bf(w_in[l]), cos, sin, rk_mu[l], vec,
                                                 _bf(rk_w2[l]), _bf(rk_a2[l]), _bf(rk_g2[l]), n_ctx)

        lam, bblk, cblk = _s5_params(s5_lam_re[l], s5_lam_im[l], s5_log_dt[l], s5_b_re[l], s5_b_im[l],
                                     s5_c_re[l], s5_c_im[l])
        sf, sb = _s5_scan(u, lam, bblk, cblk, bsz, n_ctx)

        bias, off_idx = _na_bias_table(na_rpb[l], rows)
        y_na = _na_attention(qkv, bias, off_idx, n_ctx)

        rf, rb = _rwkv_scan(com, dirs, n_ctx)

        xc, h2, aff = _mixout(u, sf, sb, jnp.stack([s5_d[l], s5_glu_b[l]]),
                              _bf(s5_glu_w[l]), y_na, rf, rb, g_rk, bonus, jnp.stack([rk_ln_w[l], rk_ln_b[l]]),
                              _bf(w_out[l]), norm_g[l, 1:4], gates1, xc, mods2, _bf(ec_router[l].T), n_ctx)
        wg, wu, wd = (_expert_weights_bf16(w, l) for w in (ec_w_gate, ec_w_up, ec_w_down))
        slots, lo = _select(aff, ((n_ctx, n), (0, n_ctx)) if need_ctx else ((n_ctx, n),), _token_tile(n_ctx))
        slots_t = jnp.swapaxes(slots, 1, 2)
        if need_ctx:
            xc = _expert_choice(xc, h2, aff, slots, slots_t, lo, wg, wu, wd, norm_g[l, 3], both[:, 0, 5, 0], 0, n_ctx,
                                n_ctx, "alias")
        xc = _expert_choice(xc, h2, aff, slots, slots_t, lo, wg, wu, wd, norm_g[l, 3], both[:, 1, 5, 0], n_ctx, n, n_ctx,
                            "copy" if need_ctx else "drop")
    return xc
```

```python
import functools
import math

import jax
import jax.numpy as jnp
import numpy as np
from jax import lax
from jax.experimental import pallas as pl
from jax.experimental.pallas import tpu as pltpu

D_MODEL = 1024
DEPTH = 2
GRID_W = 64
D_MIX = D_MODEL
HEAD_DIM = 64
S5_WIDTH = D_MIX // 4
S5_GROUP = 16
S5_GROUPS = S5_WIDTH // S5_GROUP
S5_STATE = 64
NA_WIDTH = (D_MIX - S5_WIDTH) // 2
NA_HEADS = NA_WIDTH // HEAD_DIM
NA_WIN_R = 8
NA_WIN_C = 16
ROPE_BASE = 10000.0
RK_WIDTH = D_MIX - S5_WIDTH - NA_WIDTH
RK_HEADS = RK_WIDTH // HEAD_DIM
RK_DECAY_RANK = 64
RK_A_RANK = 64
RK_GATE_RANK = 128
RK_IN_WIDTH = 3 * RK_WIDTH + 2 * RK_DECAY_RANK + 2 * RK_A_RANK + RK_GATE_RANK
RK_GN_EPS = 64e-5
N_IN = S5_WIDTH + 3 * NA_WIDTH + RK_IN_WIDTH
N_EXPERTS = 16
EC_CAPACITY_FACTOR = 2
RMS_EPS = 1e-6
NEG_INF = -1e30

VMEM_LIMIT_BYTES = 56 * 1024 * 1024
EXPERT_VMEM_LIMIT_BYTES = 62 * 1024 * 1024
TOKEN_TILE = 256
S5_GP = S5_GROUPS * S5_STATE
S5_CHUNK = 64
S5_LANES = 512
RK_CHUNK = 64
RK_SAMPLES = 4
NA_QROWS = 4
FFN_ROWS = 512
FFN_FCHUNK = 768
CAST_ELEMS = 1024 * 1408
SEL_LANES = 256
COMBINE_WINDOW = 96
GATHER_SPLIT_TILES = 8


def _compiler_params(semantics, vmem_limit_bytes=VMEM_LIMIT_BYTES):
    return pltpu.CompilerParams(dimension_semantics=semantics, vmem_limit_bytes=vmem_limit_bytes)


def _token_tile(n_ctx):
    return min(TOKEN_TILE, n_ctx)


def _mod_specs(bsz, d, n_ctx):
    cb = n_ctx // _token_tile(n_ctx)
    return [pl.BlockSpec((None, None, None, 1, d), lambda b, i, j=j: (b, jnp.minimum(i // cb, 1), j, 0, 0))
            for j in range(2)]


def _dotf(x, y):
    return jnp.dot(x, y, preferred_element_type=jnp.float32)


def _bf(x):
    return x.astype(jnp.bfloat16)


def _rope_tables(n_ctx, n):
    t = np.arange(n)
    nf = HEAD_DIM // 4
    inv_freq = ROPE_BASE ** (-np.arange(nf, dtype=np.float32) / nf)
    pos = np.stack([(t // GRID_W).astype(np.float32), (t % GRID_W).astype(np.float32)], axis=1)
    ang = pos[:, :, None] * inv_freq[None, None, :]
    cos = np.repeat(np.cos(ang), 2, axis=1).reshape(n, HEAD_DIM)
    sin = np.sin(ang)
    sin = np.stack([-sin[:, 0], sin[:, 0], -sin[:, 1], sin[:, 1]], axis=1).reshape(n, HEAD_DIM)
    cos = np.concatenate([np.ones((n_ctx, HEAD_DIM), np.float32), cos.astype(np.float32)], axis=0)
    sin = np.concatenate([np.zeros((n_ctx, HEAD_DIM), np.float32), sin.astype(np.float32)], axis=0)
    reps = 2 * NA_HEADS
    return jnp.asarray(np.tile(cos, (1, reps))), jnp.asarray(np.tile(sin, (1, reps)))


def _inproj_kernel(x_ref, xp_ref, xn_ref, g_ref, sc_ref, sh_ref, w_ref, cos_ref, sin_ref,
                   mu_ref, vec_ref, w2_ref, a2_ref, g2_ref, ones_ref,
                   u_ref, qkv_ref, com_ref, dir_ref, grk_ref, bonus_ref, *, tn, n_ctx):
    x = jnp.concatenate([xp_ref[...], x_ref[...], xn_ref[...]], axis=0)
    y = x * lax.rsqrt(jnp.mean(x * x, axis=-1, keepdims=True) + RMS_EPS)
    h_ext = _bf((y * g_ref[...]) * (1.0 + sc_ref[...]) + sh_ref[...])
    z_ext = _dotf(h_ext, w_ref[:, S5_WIDTH + 3 * NA_WIDTH:])
    t0 = pl.program_id(1) * tn
    nt = pl.num_programs(1) * tn
    keep_prev = jnp.where((t0 == 0) | (t0 == n_ctx), 0.0, 1.0)
    keep_next = jnp.where((t0 + tn == n_ctx) | (t0 + tn == nt), 0.0, 1.0)
    _rk_prepare(z_ext[8:8 + tn], z_ext[7:8] * keep_prev, z_ext[8 + tn:9 + tn] * keep_next, mu_ref, vec_ref,
                w2_ref, a2_ref, g2_ref, ones_ref, com_ref, dir_ref, grk_ref, bonus_ref)
    h = h_ext[8:8 + tn]
    u_ref[...] = _dotf(h, w_ref[:, :S5_WIDTH])
    qk = _dotf(h, w_ref[:, S5_WIDTH:S5_WIDTH + 2 * NA_WIDTH])
    nf = HEAD_DIM // 4
    lane = lax.broadcasted_iota(jnp.int32, qk.shape, 1)
    first = (lane & (2 * nf - 1)) < nf
    partner = jnp.where(first, pltpu.roll(qk, qk.shape[1] - nf, axis=1), pltpu.roll(qk, nf, axis=1))
    qk = _bf(qk * cos_ref[...] + partner * sin_ref[...])
    v = _bf(_dotf(h, w_ref[:, S5_WIDTH + 2 * NA_WIDTH:S5_WIDTH + 3 * NA_WIDTH]))
    for hd in range(NA_HEADS):
        lo = hd * HEAD_DIM
        qkv_ref[0, hd] = qk[:, lo:lo + HEAD_DIM]
        qkv_ref[1, hd] = qk[:, NA_WIDTH + lo:NA_WIDTH + lo + HEAD_DIM]
        qkv_ref[2, hd] = v[:, lo:lo + HEAD_DIM]


def _inproj(xc, g, mods, w_bf16, cos, sin, mu, vec, w2, a2, g2, n_ctx):
    bsz, nt, d = xc.shape
    tn = _token_tile(n_ctx)
    tb = tn // 8
    nb = nt // 8
    ones = _head_sum_matrix()
    mu2 = mu.reshape(1, RK_IN_WIDTH)

    def full(a):
        return pl.BlockSpec(a.shape, lambda b, i: (0,) * a.ndim)

    S = jax.ShapeDtypeStruct
    return pl.pallas_call(
        functools.partial(_inproj_kernel, tn=tn, n_ctx=n_ctx),
        grid=(bsz, nt // tn),
        in_specs=[pl.BlockSpec((None, tn, d), lambda b, i: (b, i, 0)),
                  pl.BlockSpec((None, 8, d), lambda b, i: (b, jnp.maximum(i * tb - 1, 0), 0)),
                  pl.BlockSpec((None, 8, d), lambda b, i: (b, jnp.minimum((i + 1) * tb, nb - 1), 0)),
                  pl.BlockSpec((1, d), lambda b, i: (0, 0)),
                  *_mod_specs(bsz, d, n_ctx),
                  pl.BlockSpec((d, N_IN), lambda b, i: (0, 0)),
                  pl.BlockSpec((tn, 2 * NA_WIDTH), lambda b, i: (i, 0)),
                  pl.BlockSpec((tn, 2 * NA_WIDTH), lambda b, i: (i, 0)),
                  full(mu2), full(vec), full(w2), full(a2), full(g2), full(ones)],
        out_specs=[pl.BlockSpec((tn, S5_WIDTH), lambda b, i: (i, b)),
                   pl.BlockSpec((None, 3, NA_HEADS, tn, HEAD_DIM), lambda b, i: (b, 0, 0, i, 0)),
                   pl.BlockSpec((None, 3, RK_HEADS, tn, HEAD_DIM), lambda b, i: (b, 0, 0, i, 0)),
                   pl.BlockSpec((None, 2, 3, RK_HEADS, tn, HEAD_DIM), lambda b, i: (b, 0, 0, 0, i, 0)),
                   pl.BlockSpec((None, tn, RK_WIDTH), lambda b, i: (b, i, 0)),
                   pl.BlockSpec((None, tn, RK_WIDTH), lambda b, i: (b, i, 0))],
        out_shape=[S((nt, bsz * S5_WIDTH), jnp.float32),
                   S((bsz, 3, NA_HEADS, nt, HEAD_DIM), jnp.bfloat16),
                   S((bsz, 3, RK_HEADS, nt, HEAD_DIM), jnp.float32),
                   S((bsz, 2, 3, RK_HEADS, nt, HEAD_DIM), jnp.float32),
                   S((bsz, nt, RK_WIDTH), jnp.float32), S((bsz, nt, RK_WIDTH), jnp.float32)],
        compiler_params=_compiler_params(("parallel", "parallel")),
    )(xc, xc, xc, g.reshape(1, d), mods, mods, w_bf16, cos, sin, mu2, vec, w2, a2, g2, ones)


def _s5_kernel(uf_ref, ub_ref, lam_ref, bw_ref, cw_ref, yf_ref, yb_ref, sbuf, st, *, T, B):
    c = pl.program_id(0)

    @pl.when(c == 0)
    def _():
        st[...] = jnp.zeros_like(st)

    for d, u_ref in enumerate((uf_ref, ub_ref)):
        sbuf[d] = _dotf(_bf(u_ref[...].reshape(T * B, S5_WIDTH)), bw_ref[d])
    for d in range(2):
        for h in range(S5_GP // S5_LANES):
            re_sl = pl.ds(h * S5_LANES, S5_LANES)
            im_sl = pl.ds(S5_GP + h * S5_LANES, S5_LANES)
            lr = jnp.broadcast_to(lam_ref[d, 0:1, h * S5_LANES:(h + 1) * S5_LANES], (B, S5_LANES))
            li = jnp.broadcast_to(lam_ref[d, 1:2, h * S5_LANES:(h + 1) * S5_LANES], (B, S5_LANES))
            s_re, s_im = st[d, :, re_sl], st[d, :, im_sl]
            for i in range(T):
                t = i if d == 0 else T - 1 - i
                rows = pl.ds(t * B, B)
                s_re, s_im = (lr * s_re - li * s_im + sbuf[d, rows, re_sl],
                              lr * s_im + li * s_re + sbuf[d, rows, im_sl])
                sbuf[d, rows, re_sl] = s_re
                sbuf[d, rows, im_sl] = s_im
            st[d, :, re_sl] = s_re
            st[d, :, im_sl] = s_im
    for d, y_ref in enumerate((yf_ref, yb_ref)):
        y_ref[...] = _dotf(_bf(sbuf[d]), cw_ref[d]).reshape(T, B * S5_WIDTH)


def _backward_chunk(c, nc, ncc):
    return jnp.where(c < ncc, ncc - 1 - c, nc + ncc - 1 - c)


def _s5_scan(uf, lam, bblk, cblk, bsz, n_ctx):
    T = S5_CHUNK
    nc = uf.shape[0] // T
    ncc = n_ctx // T
    bw, cw = _bf(bblk), _bf(cblk)
    bmap = functools.partial(_backward_chunk, nc=nc, ncc=ncc)
    blk = (T, bsz * S5_WIDTH)

    def full(shape):
        return pl.BlockSpec(shape, lambda c: (0,) * len(shape))

    return pl.pallas_call(
        functools.partial(_s5_kernel, T=T, B=bsz),
        grid=(nc,),
        in_specs=[pl.BlockSpec(blk, lambda c: (c, 0)), pl.BlockSpec(blk, lambda c: (bmap(c), 0)),
                  full(lam.shape), full(bw.shape), full(cw.shape)],
        out_specs=[pl.BlockSpec(blk, lambda c: (c, 0)), pl.BlockSpec(blk, lambda c: (bmap(c), 0))],
        out_shape=[jax.ShapeDtypeStruct(uf.shape, jnp.float32)] * 2,
        scratch_shapes=[pltpu.VMEM((2, T * bsz, 2 * S5_GP), jnp.float32),
                        pltpu.VMEM((2, bsz, 2 * S5_GP), jnp.float32)],
        compiler_params=_compiler_params(("arbitrary",)),
    )(uf, uf, lam, bw, cw)


def _s5_params(lam_re, lam_im, log_dt, b_re, b_im, c_re, c_im):
    lams, bs, cs = [], [], []
    eye = jnp.eye(S5_GROUPS, dtype=jnp.float32)
    for d in range(2):
        dt = jnp.exp(log_dt[d])[:, None]
        mag = jnp.exp(lam_re[d] * dt)
        lb_re, lb_im = mag * jnp.cos(lam_im[d] * dt), mag * jnp.sin(lam_im[d] * dt)
        den = lam_re[d] ** 2 + lam_im[d] ** 2
        nr, ni = lb_re - 1.0, lb_im
        f_re = (nr * lam_re[d] + ni * lam_im[d]) / den
        f_im = (ni * lam_re[d] - nr * lam_im[d]) / den
        bb_re = f_re[..., None] * b_re - f_im[..., None] * b_im
        bb_im = f_re[..., None] * b_im + f_im[..., None] * b_re

        def blockdiag_in(m):
            return jnp.einsum('gph,gk->ghkp', m, eye).reshape(S5_WIDTH, S5_GP)

        def blockdiag_out(m):
            return jnp.einsum('ghp,gk->gpkh', m, eye).reshape(S5_GP, S5_WIDTH)

        bs.append(jnp.concatenate([blockdiag_in(bb_re), blockdiag_in(bb_im)], axis=1))
        cs.append(jnp.concatenate([blockdiag_out(c_re[d]), -blockdiag_out(c_im[d])], axis=0))
        lams.append(jnp.stack([lb_re.reshape(S5_GP), lb_im.reshape(S5_GP)]))
    return jnp.stack(lams), jnp.stack(bs), jnp.stack(cs)


def _na_offsets(rows):
    win_r = min(NA_WIN_R, rows)
    i = np.arange(rows)
    r0 = np.clip(i - win_r // 2, 0, rows - win_r)
    return r0 - i + NA_WIN_R - 1, win_r


def _na_bias_table(rpb, rows):
    off, win_r = _na_offsets(rows)
    offs = np.unique(off)
    qc = np.arange(GRID_W)[:, None]
    kc = np.arange(GRID_W)[None, :]
    c0 = np.clip(qc - NA_WIN_C // 2, 0, GRID_W - NA_WIN_C)
    valid = (kc >= c0) & (kc < c0 + NA_WIN_C)
    dc = np.clip(kc - qc, 1 - NA_WIN_C, NA_WIN_C - 1) + NA_WIN_C - 1
    full = jnp.where(valid[None, :, None, :], rpb[:, :, dc].transpose(0, 2, 1, 3), NEG_INF)
    full = full.reshape(rpb.shape[0], GRID_W, -1)
    tabs = [full[:, :, o * GRID_W:(o + win_r) * GRID_W] for o in offs]
    return jnp.stack(tabs, axis=1), jnp.asarray(off - offs[0], jnp.int32)


_NT_DIMS = (((1,), (1,)), ((), ()))


def _softmax_pv(s_list, v_list):
    units = range(len(s_list[0]))
    m = [functools.reduce(jnp.maximum, [jnp.max(s[u], axis=-1, keepdims=True) for s in s_list]) for u in units]
    p = [[jnp.exp(s[u] - m[u]) for u in units] for s in s_list]
    den = [sum(jnp.sum(pj[u], axis=-1, keepdims=True) for pj in p) for u in units]
    o = [sum(_dotf(_bf(pj[u]), vj[u]) for pj, vj in zip(p, v_list)) for u in units]
    return [o[u] / den[u] for u in units]


def _na_kernel(off_ref, q_ref, k_ref, v_ref, *rest, n_ctx, win_r, rows):
    bias_refs, o_ref = rest[:-1], rest[-1]
    Q = len(bias_refs)
    j = pl.program_id(1) * Q
    cb = n_ctx // GRID_W
    H = q_ref.shape[0]
    units = [(qi, h) for qi in range(Q) for h in range(H)]
    scale = HEAD_DIM ** -0.5

    def scores(qi, h, start, size):
        return lax.dot_general(q_ref[h, qi * GRID_W:(qi + 1) * GRID_W, :], k_ref[h, pl.ds(start, size), :], _NT_DIMS,
                               preferred_element_type=jnp.float32) * scale

    def store(o):
        for qi in range(Q):
            o_ref[qi * GRID_W:(qi + 1) * GRID_W, :] = jnp.concatenate(o[qi * H:(qi + 1) * H], axis=-1)

    @pl.when(j < cb)
    def _():
        s = [scores(qi, h, 0, n_ctx) for qi, h in units]
        store(_softmax_pv([s], [[v_ref[h, pl.ds(0, n_ctx), :] for _, h in units]]))

    @pl.when(j >= cb)
    def _():
        nk = win_r * GRID_W
        start = [pl.multiple_of(n_ctx + jnp.clip(j - cb + qi - win_r // 2, 0, rows - win_r) * GRID_W, GRID_W)
                 for qi in range(Q)]
        s_lat = [scores(qi, h, start[qi], nk) + bias_refs[qi][h] for qi, h in units]
        s_ctx = [scores(qi, h, 0, n_ctx) for qi, h in units]
        store(_softmax_pv([s_lat, s_ctx], [[v_ref[h, pl.ds(start[qi], nk), :] for qi, h in units],
                                           [v_ref[h, pl.ds(0, n_ctx), :] for _, h in units]]))


def _na_attention(qkv, bias, off_idx, n_ctx):
    bsz, _, H, nt, _ = qkv.shape
    rows = (nt - n_ctx) // GRID_W
    win_r = min(NA_WIN_R, rows)
    cb = n_ctx // GRID_W
    Q = NA_QROWS if (cb % NA_QROWS == 0 and rows % NA_QROWS == 0) else 1

    def bias_spec(qi):
        return pl.BlockSpec((H, None, GRID_W, win_r * GRID_W),
                            lambda b, j, off: (0, off[jnp.maximum(j * Q + qi - cb, 0)], 0, 0))

    return pl.pallas_call(
        functools.partial(_na_kernel, n_ctx=n_ctx, win_r=win_r, rows=rows),
        grid_spec=pltpu.PrefetchScalarGridSpec(
            num_scalar_prefetch=1,
            grid=(bsz, (cb + rows) // Q),
            in_specs=[pl.BlockSpec((None, None, H, Q * GRID_W, HEAD_DIM), lambda b, j, off: (b, 0, 0, j, 0)),
                      pl.BlockSpec((None, None, H, nt, HEAD_DIM), lambda b, j, off: (b, 1, 0, 0, 0)),
                      pl.BlockSpec((None, None, H, nt, HEAD_DIM), lambda b, j, off: (b, 2, 0, 0, 0)),
                      *[bias_spec(qi) for qi in range(Q)]],
            out_specs=pl.BlockSpec((None, Q * GRID_W, H * HEAD_DIM), lambda b, j, off: (b, j, 0)),
        ),
        out_shape=jax.ShapeDtypeStruct((bsz, nt, H * HEAD_DIM), jnp.float32),
        compiler_params=_compiler_params(("parallel", "arbitrary")),
    )(off_idx, qkv, qkv, qkv, *([bias] * Q))


def _head_sum_matrix():
    i = np.arange(RK_WIDTH)
    return jnp.asarray((i[:, None] // HEAD_DIM) == (i[None, :] // HEAD_DIM), jnp.bfloat16)


def _head_sums(x, ones):
    hi = x.astype(jnp.bfloat16)
    lo = (x - hi.astype(jnp.float32)).astype(jnp.bfloat16)
    return _dotf(hi, ones) + _dotf(lo, ones)


def _rk_prepare(z, z_prev, z_next, mu_ref, vec_ref, w2_ref, a2_ref, g2_ref, ones_ref,
                com_ref, dir_ref, g_ref, bonus_ref):
    tn = z.shape[0]
    row = lax.broadcasted_iota(jnp.int32, z.shape, 0)
    prev = jnp.where(row == 0, z_prev, pltpu.roll(z, 1, axis=0))
    nxt = jnp.where(row == tn - 1, z_next, pltpu.roll(z, tn - 1, axis=0))
    zs = z + (0.5 * (prev + nxt) - z) * mu_ref[...]
    W, R = RK_WIDTH, RK_DECAY_RANK
    r, k, v = zs[:, :W], zs[:, W:2 * W], zs[:, 2 * W:3 * W]
    o = 3 * W
    zw = (zs[:, o:o + R], zs[:, o + R:o + 2 * R])
    za = (zs[:, o + 2 * R:o + 3 * R], zs[:, o + 3 * R:o + 4 * R])
    zg = zs[:, o + 4 * R:]
    k_k, k_a, r_k = vec_ref[0:1, :], vec_ref[1:2, :], vec_ref[2:3, :]
    ones = ones_ref[...]
    g_ref[...] = _dotf(_bf(jax.nn.sigmoid(zg)), g2_ref[...])
    kk = k * k_k
    kk = kk * lax.rsqrt(jnp.maximum(_head_sums(kk * kk, ones), 1e-24))
    bonus = 0.0
    fields = [r, kk, v]
    for d in range(2):
        w = -jax.nn.softplus(-(vec_ref[3 + d:4 + d, :] + _dotf(_bf(jnp.tanh(zw[d])), w2_ref[d]))) - 0.5
        a = jax.nn.sigmoid(vec_ref[5 + d:6 + d, :] + _dotf(_bf(za[d]), a2_ref[d]))
        kd = k * (1.0 + (a - 1.0) * k_a)
        bonus = bonus + _head_sums(r * kd * r_k, ones)
        fields += [-jnp.exp(w), kd, kk * a]
    bonus_ref[...] = bonus * v
    for j, t in enumerate(fields):
        for h in range(RK_HEADS):
            blk = t[:, h * HEAD_DIM:(h + 1) * HEAD_DIM]
            if j < 3:
                com_ref[j, h] = blk
            else:
                dir_ref[(j - 3) // 3, (j - 3) % 3, h] = blk


def _bdot(x, y):
    return _dotf(_bf(x), _bf(y))


def _bdot_nt(x, y):
    return lax.dot_general(_bf(x), _bf(y), _NT_DIMS, preferred_element_type=jnp.float32)


def _bdot_tn(x, y):
    return lax.dot_general(_bf(x), _bf(y), (((0,), (0,)), ((), ())), preferred_element_type=jnp.float32)


def _rwkv_units(units):
    T = units[0][0].shape[0]
    U = range(len(units))
    steps = int(math.log2(T))
    ti = lax.broadcasted_iota(jnp.int32, (T, T), 0)
    si = lax.broadcasted_iota(jnp.int32, (T, T), 1)
    ti2 = lax.broadcasted_iota(jnp.int32, (T, 2 * T), 0)
    si2 = lax.broadcasted_iota(jnp.int32, (T, 2 * T), 1) & (T - 1)
    eye = (lax.broadcasted_iota(jnp.int32, (HEAD_DIM, HEAD_DIM), 0)
           == lax.broadcasted_iota(jnp.int32, (HEAD_DIM, HEAD_DIM), 1))
    masks = {}
    for rev in (False, True):
        incl = (si >= ti) if rev else (si <= ti)
        masks[rev] = (incl.astype(jnp.bfloat16), (si2 >= ti2) if rev else (si2 <= ti2),
                      (si2 > ti2) if rev else (si2 < ti2))
    cum = []
    for (r, kk, v, lw, kd, b, h0, rev) in units:
        l1 = lw.astype(jnp.bfloat16)
        rem = lw - l1.astype(jnp.float32)
        l2 = rem.astype(jnp.bfloat16)
        l3 = (rem - l2.astype(jnp.float32)).astype(jnp.bfloat16)
        tri = masks[rev][0]
        cum.append(_dotf(tri, l1) + (_dotf(tri, l2) + _dotf(tri, l3)))
    at, rt, p, cl = [], [], [], []
    for u, (r, kk, v, lw, kd, b, h0, rev) in enumerate(units):
        c = cum[u]
        cl.append(c[0:1, :] if rev else c[T - 1:T, :])
        e_neg = jnp.exp(-c)
        at.append(-kk * jnp.exp(c - lw))
        rt.append(r * jnp.exp(c))
        p.append(_bdot_nt(jnp.concatenate([at[u], rt[u]], axis=0), jnp.concatenate([b * e_neg, kd * e_neg], axis=0)))
    top = [jnp.where(masks[units[u][7]][2], p[u][:T], 0.0) for u in U]
    l2m = [jnp.where(masks[units[u][7]][1], p[u][T:], 0.0) for u in U]
    npow = [top[u][:, :T] for u in U]
    x = [jnp.concatenate([at[u], _bdot(top[u][:, T:], units[u][2])], axis=1) for u in U]
    for i in range(steps):
        x = [x[u] + _bdot(npow[u], x[u]) for u in U]
        if i < steps - 1:
            npow = [_bdot(npow[u], npow[u]) for u in U]
    z = [jnp.concatenate([x[u], jnp.concatenate([jnp.zeros_like(units[u][2]), units[u][2]], axis=1)], axis=0)
         for u in U]
    ry = [_bdot(l2m[u], z[u]) for u in U]
    gj = []
    for u, (r, kk, v, lw, kd, b, h0, rev) in enumerate(units):
        e_end = jnp.exp(cl[u] - cum[u])
        gj.append(_bdot_tn(jnp.concatenate([b * e_end, kd * e_end], axis=0), z[u]))
    out = []
    for u in U:
        g = jnp.where(eye, jnp.exp(cl[u]), 0.0) + gj[u][:, :HEAD_DIM]
        yh = _bdot(jnp.concatenate([rt[u] + ry[u][:, :HEAD_DIM], g], axis=0), units[u][6])
        out.append((yh[:T] + ry[u][:, HEAD_DIM:], yh[T:] + gj[u][:, HEAD_DIM:]))
    return out


def _rwkv_kernel(cf_ref, cb_ref, df_ref, db_ref, yf_ref, yb_ref, hs):
    c = pl.program_id(1)
    S, H = cf_ref.shape[0], cf_ref.shape[2]

    @pl.when(c == 0)
    def _():
        hs[...] = jnp.zeros_like(hs)

    units, where = [], []
    for s in range(S):
        for d, (c_ref, d_ref) in enumerate(((cf_ref, df_ref), (cb_ref, db_ref))):
            for h in range(H):
                units.append((c_ref[s, 0, h], c_ref[s, 1, h], c_ref[s, 2, h],
                              d_ref[s, 0, h], d_ref[s, 1, h], d_ref[s, 2, h], hs[s, d, h], d == 1))
                where.append((s, d, h))
    for (s, d, h), (y, hn) in zip(where, _rwkv_units(units)):
        (yf_ref, yb_ref)[d][s, h] = y
        hs[s, d, h] = hn


def _rwkv_scan(com, dirs, n_ctx):
    bsz, _, H, nt, _ = com.shape
    T = RK_CHUNK
    S = RK_SAMPLES if bsz % RK_SAMPLES == 0 else 1
    bmap = functools.partial(_backward_chunk, nc=nt // T, ncc=n_ctx // T)
    cblk = (S, 3, H, T, HEAD_DIM)
    dblk = (S, None, 3, H, T, HEAD_DIM)
    oblk = (S, H, T, HEAD_DIM)
    return pl.pallas_call(
        _rwkv_kernel,
        grid=(bsz // S, nt // T),
        in_specs=[pl.BlockSpec(cblk, lambda b, c: (b, 0, 0, c, 0)),
                  pl.BlockSpec(cblk, lambda b, c: (b, 0, 0, bmap(c), 0)),
                  pl.BlockSpec(dblk, lambda b, c: (b, 0, 0, 0, c, 0)),
                  pl.BlockSpec(dblk, lambda b, c: (b, 1, 0, 0, bmap(c), 0))],
        out_specs=[pl.BlockSpec(oblk, lambda b, c: (b, 0, c, 0)),
                   pl.BlockSpec(oblk, lambda b, c: (b, 0, bmap(c), 0))],
        out_shape=[jax.ShapeDtypeStruct((bsz, H, nt, HEAD_DIM), jnp.float32)] * 2,
        scratch_shapes=[pltpu.VMEM((S, 2, H, HEAD_DIM, HEAD_DIM), jnp.float32)],
        compiler_params=_compiler_params(("parallel", "arbitrary")),
    )(com, com, dirs, dirs)


def _mixout_kernel(u_ref, sf_ref, sb_ref, s5v_ref, gw_ref, na_ref, rf_ref, rb_ref, grk_ref, bonus_ref, ln_ref,
                   w_ref, ng_ref, gt_ref, x_ref, sc_ref, sh_ref, wrt_ref, o_ref, h_ref, aff_ref):
    y5 = jax.nn.gelu(s5v_ref[0:1, :] * u_ref[...] + sf_ref[...] + sb_ref[...])
    y5 = y5 * jax.nn.sigmoid(_dotf(_bf(y5), gw_ref[...]) + s5v_ref[1:2, :])
    outs = []
    for h in range(RK_HEADS):
        y = rf_ref[h] + rb_ref[h]
        yc = y - jnp.mean(y, axis=-1, keepdims=True)
        var = jnp.mean(yc * yc, axis=-1, keepdims=True)
        outs.append(yc * lax.rsqrt(var + RK_GN_EPS))
    yrk = (jnp.concatenate(outs, axis=-1) * ln_ref[0:1, :] + ln_ref[1:2, :] + bonus_ref[...]) * grk_ref[...]
    z = (_dotf(_bf(y5), w_ref[:S5_WIDTH, :])
         + _dotf(_bf(na_ref[...]), w_ref[S5_WIDTH:S5_WIDTH + NA_WIDTH, :])
         + _dotf(_bf(yrk), w_ref[S5_WIDTH + NA_WIDTH:, :]))
    zn = z * lax.rsqrt(jnp.mean(z * z, axis=-1, keepdims=True) + RMS_EPS)
    x = x_ref[...] + gt_ref[...] * (zn * ng_ref[0:1, :])
    o_ref[...] = x
    y = x * lax.rsqrt(jnp.mean(x * x, axis=-1, keepdims=True) + RMS_EPS)
    h = _bf((y * ng_ref[1:2, :]) * (1.0 + sc_ref[...]) + sh_ref[...])
    h_ref[...] = h
    logits = lax.dot_general(wrt_ref[...], h, _NT_DIMS, preferred_element_type=jnp.float32)
    e = jnp.exp(logits - jnp.max(logits, axis=0, keepdims=True))
    aff_ref[...] = e / jnp.sum(e, axis=0, keepdims=True)


def _mixout(u, sf, sb, s5v, glu_w, y_na, rf, rb, g_rk, bonus, ln, w_out, ng, gates, xc, mods, wrt, n_ctx):
    bsz, nt, d = xc.shape
    tn = _token_tile(n_ctx)
    cb = n_ctx // tn
    ne = wrt.shape[0]

    def tok(w):
        return pl.BlockSpec((None, tn, w), lambda b, i: (b, i, 0))

    def full(a):
        return pl.BlockSpec(a.shape, lambda b, i: (0,) * a.ndim)

    tm = pl.BlockSpec((tn, S5_WIDTH), lambda b, i: (i, b))
    yblk = pl.BlockSpec((None, RK_HEADS, tn, HEAD_DIM), lambda b, i: (b, 0, i, 0))
    return pl.pallas_call(
        _mixout_kernel,
        grid=(bsz, nt // tn),
        in_specs=[tm, tm, tm, full(s5v), full(glu_w), tok(NA_WIDTH), yblk, yblk, tok(RK_WIDTH), tok(RK_WIDTH),
                  full(ln), full(w_out), full(ng),
                  pl.BlockSpec((None, None, None, 1, d), lambda b, i: (b, jnp.minimum(i // cb, 1), 0, 0, 0)),
                  tok(d), *_mod_specs(bsz, d, n_ctx), full(wrt)],
        out_specs=[tok(d), tok(d), pl.BlockSpec((None, ne, tn), lambda b, i: (b, 0, i))],
        out_shape=[jax.ShapeDtypeStruct(xc.shape, jnp.float32), jax.ShapeDtypeStruct(xc.shape, jnp.bfloat16),
                   jax.ShapeDtypeStruct((bsz, ne, nt), jnp.float32)],
        compiler_params=_compiler_params(("parallel", "parallel")),
    )(u, sf, sb, s5v, glu_w, y_na, rf, rb, g_rk, bonus, ln, w_out, ng, gates, xc, mods, mods, wrt)


def _cast_kernel(w_ref, o_ref):
    o_ref[...] = _bf(w_ref[...])


def _expert_weights_bf16(w, l):
    _, ne, r, c = w.shape
    nblk = r * c // CAST_ELEMS
    tr = r // nblk
    return pl.pallas_call(
        _cast_kernel,
        grid=(ne, nblk),
        in_specs=[pl.BlockSpec((None, None, tr, c), lambda e, i: (l, e, i, 0))],
        out_specs=pl.BlockSpec((None, tr, c), lambda e, i: (e, i, 0)),
        out_shape=jax.ShapeDtypeStruct((ne, r, c), jnp.bfloat16),
        compiler_params=_compiler_params(("parallel", "parallel")),
    )(w)


def _capacity(n):
    return EC_CAPACITY_FACTOR * n // N_EXPERTS


def _prefix_count(m):
    ne, n = m.shape
    s = lax.broadcasted_iota(jnp.int32, (SEL_LANES, SEL_LANES), 0)
    t = lax.broadcasted_iota(jnp.int32, (SEL_LANES, SEL_LANES), 1)
    tri = _bf(s < t)
    out, carry = [], jnp.zeros((ne, 1), jnp.float32)
    for j in range(n // SEL_LANES):
        blk = m[:, j * SEL_LANES:(j + 1) * SEL_LANES]
        out.append(_dotf(_bf(blk), tri) + carry)
        carry = carry + jnp.sum(blk, axis=1, keepdims=True)
    return jnp.concatenate(out, axis=1)


def _select_segment(a, k):
    key = pltpu.bitcast(a, jnp.int32)

    def step(i, tau):
        cand = tau | (1 << (30 - i))
        cnt = jnp.sum((key >= cand).astype(jnp.float32), axis=1, keepdims=True)
        return jnp.where(cnt >= k, cand, tau)

    tau = lax.fori_loop(0, 31, step, jnp.zeros((a.shape[0], 1), jnp.int32))
    gt = key > tau
    eq = key == tau
    need = k - jnp.sum(gt.astype(jnp.float32), axis=1, keepdims=True)
    sel = gt | (eq & (_prefix_count(eq.astype(jnp.float32)) < need))
    pos = _prefix_count(sel.astype(jnp.float32))
    return jnp.where(sel, pos, -1.0), pos


def _select_kernel(a_ref, o_ref, lo_ref, *, segments, tn):
    if sum(n for _, n in segments) < o_ref.shape[1]:
        o_ref[...] = jnp.full(o_ref.shape, -1.0, o_ref.dtype)
        lo_ref[...] = jnp.zeros(lo_ref.shape, lo_ref.dtype)
    for (t0, n) in segments:
        slots, pos = _select_segment(a_ref[:, t0:t0 + n], _capacity(n))
        o_ref[:, t0:t0 + n] = slots
        for j in range(n // tn):
            lo_ref[:, t0 // tn + j:t0 // tn + j + 1] = pos[:, j * tn:j * tn + 1].astype(jnp.int32)


def _select(aff, segments, tn):
    bsz, ne, nt = aff.shape
    return pl.pallas_call(
        functools.partial(_select_kernel, segments=segments, tn=tn),
        grid=(bsz,),
        in_specs=[pl.BlockSpec((None, ne, nt), lambda b: (b, 0, 0))],
        out_specs=[pl.BlockSpec((None, ne, nt), lambda b: (b, 0, 0)),
                   pl.BlockSpec((None, ne, nt // tn), lambda b: (b, 0, 0))],
        out_shape=[jax.ShapeDtypeStruct((bsz, ne, nt), jnp.float32),
                   jax.ShapeDtypeStruct((bsz, ne, nt // tn), jnp.int32)],
        compiler_params=_compiler_params(("parallel",)),
    )(aff)


def _expert_kernel(ok_ref, slot_ref, aff_ref, h_ref, wg_ref, wu_ref, wd_ref, y_ref, xin_ref, gate_ref, *,
                   G, cap, n, t_in, wt):
    e, b = pl.program_id(0), pl.program_id(1)

    def gather(s, rows, slot0, c0, width):
        cols = slice(t_in + c0, t_in + c0 + width)
        want = (lax.broadcasted_iota(jnp.int32, (rows, width), 0) + slot0).astype(jnp.float32)
        hit = slot_ref[s, pl.ds(e, 1), cols] == want
        out = slice(s * cap + slot0, s * cap + slot0 + rows)
        xin_ref[out, :] = _bf(_dotf(_bf(hit), h_ref[s, cols, :]))
        gate_ref[out, :] = jnp.sum(jnp.where(hit, aff_ref[s, pl.ds(e, 1), cols], 0.0), axis=1, keepdims=True)

    def gather_full():
        for s in range(G):
            gather(s, cap, 0, 0, n)

    if wt is None:
        gather_full()
    else:
        fits = ok_ref[e * pl.num_programs(1) + b] == 1

        @pl.when(fits)
        def _():
            for s in range(G):
                gather(s, cap // 2, 0, 0, wt)
                gather(s, cap // 2, cap // 2, n - wt, wt)

        pl.when(jnp.logical_not(fits))(gather_full)

    xin = xin_ref[...]
    acc = jnp.zeros((G * cap, wd_ref.shape[1]), jnp.float32)
    for c0 in range(0, wg_ref.shape[1], FFN_FCHUNK):
        cols = slice(c0, min(c0 + FFN_FCHUNK, wg_ref.shape[1]))
        hid = jax.nn.silu(_dotf(xin, wg_ref[:, cols])) * _dotf(xin, wu_ref[:, cols])
        acc = acc + _dotf(_bf(hid), wd_ref[cols, :])
    y_ref[...] = _bf(acc * gate_ref[...])


def _experts(slots, lo, aff, h2, wg, wu, wd, t0, n, tn):
    bsz, ne, nt = slots.shape
    d = h2.shape[2]
    f = wg.shape[2]
    cap = _capacity(n)
    G = max(1, min(bsz, FFN_ROWS // cap))
    nb = bsz // G
    tiles = n // tn
    if tiles >= GATHER_SPLIT_TILES:
        wt_tiles = tiles * 5 // 8
        wt = wt_tiles * tn
        first = t0 // tn
        ok = (lo[:, :, first + wt_tiles] >= cap // 2) & (lo[:, :, first + tiles - wt_tiles] <= cap // 2)
        ok = jnp.all(ok.reshape(nb, G, ne), axis=1).T.astype(jnp.int32).reshape(-1)
    else:
        wt = None
        ok = jnp.zeros((ne * nb,), jnp.int32)
    if t0 % n == 0:
        hspec = pl.BlockSpec((G, n, d), lambda e, b, *_: (b, t0 // n, 0))
        sspec = pl.BlockSpec((G, ne, n), lambda e, b, *_: (b, 0, t0 // n))
        t_in = 0
    else:
        hspec = pl.BlockSpec((G, nt, d), lambda e, b, *_: (b, 0, 0))
        sspec = pl.BlockSpec((G, ne, nt), lambda e, b, *_: (b, 0, 0))
        t_in = t0

    def wspec(shape):
        return pl.BlockSpec((None,) + shape, lambda e, b, *_: (e, 0, 0))

    return pl.pallas_call(
        functools.partial(_expert_kernel, G=G, cap=cap, n=n, t_in=t_in, wt=wt),
        grid_spec=pltpu.PrefetchScalarGridSpec(
            num_scalar_prefetch=1,
            grid=(ne, nb),
            in_specs=[sspec, sspec, hspec, wspec((d, f)), wspec((d, f)), wspec((f, d))],
            out_specs=pl.BlockSpec((None, None, G * cap, d), lambda e, b, *_: (b, e, 0, 0)),
            scratch_shapes=[pltpu.VMEM((G * cap, d), jnp.bfloat16), pltpu.VMEM((G * cap, 1), jnp.float32)],
        ),
        out_shape=jax.ShapeDtypeStruct((nb, ne, G * cap, d), jnp.bfloat16),
        compiler_params=_compiler_params(("arbitrary", "arbitrary"), EXPERT_VMEM_LIMIT_BYTES),
    )(ok, slots, aff, h2, wg, wu, wd)


def _combine_kernel(st_ref, ok_ref, slot_ref, y_ref, x_ref, g_ref, gt_ref, o_ref, *, t_lo, t_hi, first, ntiles, W):
    b, i = pl.program_id(0), pl.program_id(1)
    inside = (i >= t_lo) & (i < t_hi)
    ne, cap, d = y_ref.shape
    tn = x_ref.shape[0]
    tile = i + first
    fits = ok_ref[b * ntiles + tile] == 1

    def finish(f):
        fn = f * lax.rsqrt(jnp.mean(f * f, axis=-1, keepdims=True) + RMS_EPS)
        o_ref[...] = x_ref[...] + gt_ref[...] * (fn * g_ref[...])

    @pl.when(inside & fits)
    def _():
        lane = lax.broadcasted_iota(jnp.int32, (tn, W), 1)
        hits, rows = [], []
        for e in range(ne):
            st = pl.multiple_of(st_ref[(b * ne + e) * ntiles + tile], 16)
            hits.append(_bf(slot_ref[:, e:e + 1] == (lane + st).astype(jnp.float32)))
            rows.append(y_ref[e, pl.ds(st, W), :])
        finish(_dotf(jnp.concatenate(hits, axis=1), jnp.concatenate(rows, axis=0)))

    @pl.when(inside & jnp.logical_not(fits))
    def _():
        want = lax.broadcasted_iota(jnp.int32, (tn, cap), 1).astype(jnp.float32)
        hit = jnp.concatenate([_bf(slot_ref[:, e:e + 1] == want) for e in range(ne)], axis=1)
        finish(_dotf(hit, y_ref[...].reshape(ne * cap, d)))

    @pl.when(jnp.logical_not(inside))
    def _():
        o_ref[...] = x_ref[...]


def _combine(slots_t, lo, y, xc, g, gt, t0, n, n_ctx, rest):
    bsz, nt, ne = slots_t.shape
    d = xc.shape[2]
    cap = _capacity(n)
    G = bsz // y.shape[0]
    yb = y.reshape(bsz // G, ne, G, cap, d)
    tn = _token_tile(n_ctx)
    ntiles = nt // tn
    t_lo, t_hi = t0 // tn, (t0 + n) // tn
    W = min(COMBINE_WINDOW, cap)
    st = jnp.clip(lo // 16 * 16, 0, cap - W)
    end = jnp.concatenate([lo[:, :, 1:], jnp.full_like(lo[:, :, :1], cap)], axis=2)
    end = jnp.where(jnp.arange(ntiles) == t_hi - 1, cap, end)
    ok = jnp.all(end - st <= W, axis=1).astype(jnp.int32)
    first, tiles = (0, ntiles) if rest == "copy" else (t_lo, t_hi - t_lo)
    out_first = first if rest == "alias" else 0
    out_rows = nt if rest == "alias" else tiles * tn
    return pl.pallas_call(
        functools.partial(_combine_kernel, t_lo=t_lo - first, t_hi=t_hi - first, first=first, ntiles=ntiles, W=W),
        grid_spec=pltpu.PrefetchScalarGridSpec(
            num_scalar_prefetch=2,
            grid=(bsz, tiles),
            in_specs=[pl.BlockSpec((None, tn, ne), lambda b, i, *_: (b, i + first, 0)),
                      pl.BlockSpec((None, ne, None, cap, d), lambda b, i, *_: (b // G, 0, b % G, 0, 0)),
                      pl.BlockSpec((None, tn, d), lambda b, i, *_: (b, i + first, 0)),
                      pl.BlockSpec((1, d), lambda b, i, *_: (0, 0)),
                      pl.BlockSpec((None, 1, d), lambda b, i, *_: (b, 0, 0))],
            out_specs=pl.BlockSpec((None, tn, d), lambda b, i, *_: (b, i + out_first, 0)),
        ),
        out_shape=jax.ShapeDtypeStruct((bsz, out_rows, d), jnp.float32),
        input_output_aliases={4: 0} if rest == "alias" else {},
        compiler_params=_compiler_params(("parallel", "arbitrary")),
    )(st.reshape(-1), ok.reshape(-1), slots_t, yb, xc, g.reshape(1, d), gt.reshape(bsz, 1, d))


def _expert_choice(xc, h2, aff, slots, slots_t, lo, wg, wu, wd, g, gt, t0, n, n_ctx, rest):
    y = _experts(slots, lo, aff, h2, wg, wu, wd, t0, n, _token_tile(n_ctx))
    return _combine(slots_t, lo, y, xc, g, gt, t0, n, n_ctx, rest)


def kernel(x, c, ctx, c_ctx, ada_w, ada_b, norm_g, w_in, w_out, s5_lam_re, s5_lam_im, s5_log_dt, s5_b_re, s5_b_im, s5_c_re, s5_c_im, s5_d, s5_glu_w, s5_glu_b, na_rpb, rk_mu, rk_w0, rk_w2, rk_a0, rk_a2, rk_g2, rk_k_k, rk_k_a, rk_r_k, rk_ln_w, rk_ln_b, ec_router, ec_w_gate, ec_w_up, ec_w_down):
    bsz, n, d = x.shape
    n_ctx = ctx.shape[1]
    rows = n // GRID_W
    xc = jnp.concatenate([ctx, x], axis=1)
    cos, sin = _rope_tables(n_ctx, n)

    for l in range(DEPTH):
        need_ctx = l < DEPTH - 1
        mod = jax.nn.silu(c) @ ada_w[l] + ada_b[l]
        mod_c = jnp.broadcast_to(jax.nn.silu(c_ctx) @ ada_w[l] + ada_b[l], mod.shape)
        both = jnp.stack([mod_c, mod], axis=1).reshape(bsz, 2, 6, 1, d)
        mods1 = jnp.stack([both[:, :, 1], both[:, :, 0]], axis=2)
        mods2 = jnp.stack([both[:, :, 4], both[:, :, 3]], axis=2)
        gates1 = both[:, :, 2:3]

        vec = jnp.stack([rk_k_k[l], rk_k_a[l], rk_r_k[l].reshape(-1), rk_w0[l, 0], rk_w0[l, 1],
                         rk_a0[l, 0], rk_a0[l, 1], jnp.zeros_like(rk_k_k[l])])
        u, qkv, com, dirs, g_rk, bonus = _inproj(xc, norm_g[l, 0], mods1, _bf(w_in[l]), cos, sin, rk_mu[l], vec,
                                                 _bf(rk_w2[l]), _bf(rk_a2[l]), _bf(rk_g2[l]), n_ctx)

        lam, bblk, cblk = _s5_params(s5_lam_re[l], s5_lam_im[l], s5_log_dt[l], s5_b_re[l], s5_b_im[l],
                                     s5_c_re[l], s5_c_im[l])
        sf, sb = _s5_scan(u, lam, bblk, cblk, bsz, n_ctx)

        bias, off_idx = _na_bias_table(na_rpb[l], rows)
        y_na = _na_attention(qkv, bias, off_idx, n_ctx)

        rf, rb = _rwkv_scan(com, dirs, n_ctx)

        xc, h2, aff = _mixout(u, sf, sb, jnp.stack([s5_d[l], s5_glu_b[l]]),
                              _bf(s5_glu_w[l]), y_na, rf, rb, g_rk, bonus, jnp.stack([rk_ln_w[l], rk_ln_b[l]]),
                              _bf(w_out[l]), norm_g[l, 1:4], gates1, xc, mods2, _bf(ec_router[l].T), n_ctx)
        wg, wu, wd = (_expert_weights_bf16(w, l) for w in (ec_w_gate, ec_w_up, ec_w_down))
        slots, lo = _select(aff, ((n_ctx, n), (0, n_ctx)) if need_ctx else ((n_ctx, n),), _token_tile(n_ctx))
        slots_t = jnp.swapaxes(slots, 1, 2)
        if need_ctx:
            xc = _expert_choice(xc, h2, aff, slots, slots_t, lo, wg, wu, wd, norm_g[l, 3], both[:, 0, 5, 0], 0, n_ctx,
                                n_ctx, "alias")
        xc = _expert_choice(xc, h2, aff, slots, slots_t, lo, wg, wu, wd, norm_g[l, 3], both[:, 1, 5, 0], n_ctx, n, n_ctx,
                            "copy" if need_ctx else "drop")
    return xc
```
